```python
import math
import jax
import jax.numpy as jnp
from jax import lax
import numpy as np

D_MODEL = 1024
BATCH = 8
SEQ = 8192
DEPTH = 4
DEC_BATCH = 4
DEC_SEQ = 4096
PAST_LEN = 128

EPS = 1e-6
D_HY = D_MODEL // 2
HY_GROUPS = 8
N_FILT = 2
SHORT_K = 3
EMB_DIM = 33
N_BANDS = (EMB_DIM - 1) // 2
FILT_HID = 64
FILT_TAP_SCALE = 0.04
DECAY_FAST = 0.3
DECAY_SLOW = 1.5
DECAY_TARGET = 1e-2
HY_COLS = (N_FILT + 1) * D_HY
N_HEADS = 8
NOPE_DIM = 64
ROPE_DIM = 32
V_DIM = 64
Q_LORA = 384
KV_LORA = 256
ROPE_THETA = 10000.0
Q_BLOCK = 128
POOL_WINDOWS = (2, 4, 8, 16)
N_POOL_GROUPS = 4
POOL_GRP = D_MODEL // N_POOL_GROUPS
D_FF = 4 * D_MODEL
IN_COLS = HY_COLS + Q_LORA + KV_LORA + ROPE_DIM
MIX_WIDTH = D_HY + N_HEADS * V_DIM
N_EVEN = (DEPTH + 1) // 2
N_ODD = DEPTH // 2

kernel_name = 'hybrid_hyena_mla_pool_encoder'


def rmsnorm(x, g):
    xf = x.astype(jnp.float32)
    y = xf * lax.rsqrt(jnp.mean(xf * xf, axis=-1, keepdims=True) + EPS)
    return (y * g.astype(jnp.float32)).astype(x.dtype)


def short_conv(u, w, b):
    up = jnp.pad(u, ((0, 0), (1, 1), (0, 0)))
    return up[:, :-2] * w[0] + up[:, 1:-1] * w[1] + up[:, 2:] * w[2] + b


def hyena_filters(L, w1, b1, freq, w2, b2, w3):
    f32 = jnp.float32
    t = jnp.linspace(0.0, 1.0, L, dtype=f32)[:, None]
    wpos = (2.0 * math.pi / L) * jnp.arange(L, dtype=f32)[:, None]
    bands = jnp.linspace(1e-4, N_BANDS - 1, N_BANDS, dtype=f32)[None, :]
    z = jnp.concatenate([t, jnp.cos(bands * wpos), -jnp.sin(bands * wpos)], axis=-1)
    fr = freq.astype(f32)
    h = jnp.sin(fr * (z @ w1.astype(f32) + b1.astype(f32)))
    h = jnp.sin(fr * (h @ w2.astype(f32) + b2.astype(f32)))
    h = (h @ w3.astype(f32)).reshape(L, 2, N_FILT, D_HY)
    max_decay = math.log(DECAY_TARGET) / DECAY_FAST
    min_decay = math.log(DECAY_TARGET) / DECAY_SLOW
    deltas = jnp.linspace(min_decay, max_decay, D_HY, dtype=f32)
    decay = jnp.exp(-t * jnp.abs(deltas))
    h = h * decay[:, None, None, :]
    h_fwd, h_bwd = h[:, 0], h[:, 1]
    k = jnp.concatenate([h_fwd, jnp.zeros((1, N_FILT, D_HY), f32), h_bwd[:0:-1]], axis=0)
    return jnp.fft.rfft(k, axis=0)


def fft_conv(u, kf, bias):
    L = u.shape[1]
    uf32 = u.astype(jnp.float32)
    uf = jnp.fft.rfft(uf32, n=2 * L, axis=1)
    y = jnp.fft.irfft(uf * kf[None], n=2 * L, axis=1)[:, :L]
    return (y + uf32 * bias.astype(jnp.float32)).astype(u.dtype)


def hyena(u_proj, conv_w, conv_b, w1, b1, freq, w2, b2, w3, hy_bias):
    L = u_proj.shape[1]
    u = short_conv(u_proj, conv_w, conv_b)
    x1, x2, v = jnp.split(u, 3, axis=-1)
    kf = hyena_filters(L, w1, b1, freq, w2, b2, w3)
    z = fft_conv(v, kf[:, 0], hy_bias[0]) * x1
    z = fft_conv(z, kf[:, 1], hy_bias[1]) * x2
    return z


def rope(x, cos, sin):
    x1, x2 = jnp.split(x, 2, axis=-1)
    return jnp.concatenate([x1 * cos - x2 * sin, x1 * sin + x2 * cos], axis=-1)


def mla(c_q, c_kv, k_r, q_norm, w_uq, kv_norm, w_ukv):
    B, L, _ = c_q.shape
    q = (rmsnorm(c_q, q_norm) @ w_uq).reshape(B, L, N_HEADS, NOPE_DIM + ROPE_DIM)
    kv = (rmsnorm(c_kv, kv_norm) @ w_ukv).reshape(B, L, N_HEADS, NOPE_DIM + V_DIM)
    q_n, q_r = q[..., :NOPE_DIM], q[..., NOPE_DIM:]
    k_n, v = kv[..., :NOPE_DIM], kv[..., NOPE_DIM:]
    inv_freq = 1.0 / (ROPE_THETA ** (jnp.arange(0, ROPE_DIM, 2, dtype=jnp.float32) / ROPE_DIM))
    ang = jnp.arange(L, dtype=jnp.float32)[:, None] * inv_freq[None, :]
    cos = jnp.cos(ang).astype(q.dtype)
    sin = jnp.sin(ang).astype(q.dtype)
    q_r = rope(q_r, cos[:, None, :], sin[:, None, :])
    k_r = rope(k_r, cos, sin)
    scale = (NOPE_DIM + ROPE_DIM) ** -0.5
    nb = L // Q_BLOCK
    qn_b = q_n.reshape(B, nb, Q_BLOCK, N_HEADS, NOPE_DIM).transpose(1, 0, 2, 3, 4)
    qr_b = q_r.reshape(B, nb, Q_BLOCK, N_HEADS, ROPE_DIM).transpose(1, 0, 2, 3, 4)

    def attend(blk):
        qn, qr = blk
        s = (jnp.einsum('bqhd,bkhd->bhqk', qn, k_n, preferred_element_type=jnp.float32)
             + jnp.einsum('bqhr,bkr->bhqk', qr, k_r, preferred_element_type=jnp.float32))
        p = jax.nn.softmax(s * scale, axis=-1)
        return jnp.einsum('bhqk,bkhd->bqhd', p.astype(v.dtype), v)

    o = lax.map(attend, (qn_b, qr_b))
    return o.transpose(1, 0, 2, 3, 4).reshape(B, L, N_HEADS * V_DIM)


def pool_mixer(u, pool_w, pool_scale):
    B, L, _ = u.shape
    uf = u.astype(jnp.float32)
    cs = jnp.pad(jnp.cumsum(uf, axis=1), ((0, 0), (1, 0), (0, 0)))
    pos = jnp.arange(L, dtype=jnp.int32)
    outs = []
    for g, w in enumerate(POOL_WINDOWS):
        sl = slice(g * POOL_GRP, (g + 1) * POOL_GRP)
        csg = cs[..., sl]
        lo = jnp.clip(pos - w // 2, 0, L)
        hi = jnp.clip(pos + w // 2, 0, L)
        mean = (jnp.take(csg, hi, axis=1) - jnp.take(csg, lo, axis=1)) / (hi - lo).astype(jnp.float32)[None, :, None]
        d = (mean - uf[..., sl]).astype(u.dtype)
        outs.append(d @ pool_w[g])
    return jnp.concatenate(outs, axis=-1) * pool_scale


def trunk(x, mix_norm, w_in, hy_conv_w, hy_conv_b, hf_w1, hf_b1, hf_freq, hf_w2, hf_b2, hf_w3, hy_bias,
          q_norm, w_uq, kv_norm, w_ukv, w_out, pool_w, pool_scale, mlp_norm, mlp_w1, mlp_w2, final_norm):
    for i in range(DEPTH):
        h = rmsnorm(x, mix_norm[i])
        if i % 2 == 0:
            e = i // 2
            p = h @ w_in[e]
            u_hy = p[..., :HY_COLS]
            c_q = p[..., HY_COLS:HY_COLS + Q_LORA]
            c_kv = p[..., HY_COLS + Q_LORA:HY_COLS + Q_LORA + KV_LORA]
            k_r = p[..., IN_COLS - ROPE_DIM:]
            y_hy = hyena(u_hy, hy_conv_w[e], hy_conv_b[e], hf_w1[e], hf_b1[e], hf_freq[e],
                         hf_w2[e], hf_b2[e], hf_w3[e], hy_bias[e])
            y_att = mla(c_q, c_kv, k_r, q_norm[e], w_uq[e], kv_norm[e], w_ukv[e])
            x = x + jnp.concatenate([y_hy, y_att], axis=-1) @ w_out[e]
        else:
            o = i // 2
            x = x + pool_mixer(h, pool_w[o], pool_scale[o])
        h = rmsnorm(x, mlp_norm[i])
        x = x + jnp.square(jax.nn.relu(h @ mlp_w1[i])) @ mlp_w2[i]
    return rmsnorm(x, final_norm)


def setup_inputs(seed: int = 0) -> dict:
    key = jax.random.key(seed)
    ks = jax.random.split(key, 24)

    def nrm(k, shape, s):
        return jax.random.normal(k, shape, jnp.float32) * s

    def gain(k, shape):
        return 1.0 + 0.1 * jax.random.normal(k, shape, jnp.float32)

    ne, no = N_EVEN, N_ODD
    return {
        'x_prompt': nrm(ks[0], (BATCH, SEQ, D_MODEL), 1.0),
        'x_sample': nrm(ks[1], (DEC_BATCH, DEC_SEQ, D_MODEL), 1.0),
        'mix_norm': gain(ks[2], (DEPTH, D_MODEL)),
        'w_in': nrm(ks[3], (ne, D_MODEL, IN_COLS), D_MODEL ** -0.5),
        'hy_conv_w': nrm(ks[4], (ne, SHORT_K, HY_COLS), SHORT_K ** -0.5),
        'hy_conv_b': nrm(ks[5], (ne, HY_COLS), 0.02),
        'hf_w1': nrm(ks[6], (ne, EMB_DIM, FILT_HID), EMB_DIM ** -0.5),
        'hf_b1': nrm(ks[7], (ne, FILT_HID), 0.1),
        'hf_freq': gain(ks[8], (ne, FILT_HID)),
        'hf_w2': nrm(ks[9], (ne, FILT_HID, FILT_HID), FILT_HID ** -0.5),
        'hf_b2': nrm(ks[10], (ne, FILT_HID), 0.1),
        'hf_w3': nrm(ks[11], (ne, FILT_HID, 2 * N_FILT * D_HY), FILT_TAP_SCALE * FILT_HID ** -0.5),
        'hy_bias': nrm(ks[12], (ne, N_FILT, D_HY), 0.3),
        'q_norm': gain(ks[13], (ne, Q_LORA)),
        'w_uq': nrm(ks[14], (ne, Q_LORA, N_HEADS * (NOPE_DIM + ROPE_DIM)), Q_LORA ** -0.5),
        'kv_norm': gain(ks[15], (ne, KV_LORA)),
        'w_ukv': nrm(ks[16], (ne, KV_LORA, N_HEADS * (NOPE_DIM + V_DIM)), KV_LORA ** -0.5),
        'w_out': nrm(ks[17], (ne, MIX_WIDTH, D_MODEL), MIX_WIDTH ** -0.5),
        'pool_w': nrm(ks[18], (no, N_POOL_GROUPS, POOL_GRP, POOL_GRP), POOL_GRP ** -0.5),
        'pool_scale': gain(ks[19], (no, D_MODEL)),
        'mlp_norm': gain(ks[20], (DEPTH, D_MODEL)),
        'mlp_w1': nrm(ks[21], (DEPTH, D_MODEL, D_FF), D_MODEL ** -0.5),
        'mlp_w2': nrm(ks[22], (DEPTH, D_FF, D_MODEL), D_FF ** -0.5),
        'final_norm': gain(ks[23], (D_MODEL,)),
    }


def reference(x_prompt, x_sample, mix_norm, w_in, hy_conv_w, hy_conv_b, hf_w1, hf_b1, hf_freq, hf_w2, hf_b2,
              hf_w3, hy_bias, q_norm, w_uq, kv_norm, w_ukv, w_out, pool_w, pool_scale, mlp_norm, mlp_w1,
              mlp_w2, final_norm):
    weights = (mix_norm, w_in, hy_conv_w, hy_conv_b, hf_w1, hf_b1, hf_freq, hf_w2, hf_b2, hf_w3, hy_bias,
               q_norm, w_uq, kv_norm, w_ukv, w_out, pool_w, pool_scale, mlp_norm, mlp_w1, mlp_w2, final_norm)
    y_prompt = trunk(x_prompt, *weights)
    y_sample = trunk(x_sample, *weights)
    return (y_prompt, y_sample)
```

```python
import functools
import math

import jax
import jax.numpy as jnp
from jax import lax
from jax.experimental import pallas as pl
from jax.experimental.pallas import tpu as pltpu

F32 = jnp.float32
BF16 = jnp.bfloat16

D_MODEL = 1024
D_HY = 512
N_FILT = 2
HY_COLS = (N_FILT + 1) * D_HY
EMB_DIM = 33
N_BANDS = (EMB_DIM - 1) // 2
FILT_HID = 64
DECAY_FAST = 0.3
DECAY_SLOW = 1.5
DECAY_TARGET = 1e-2
N_HEADS = 8
NOPE_DIM = 64
ROPE_DIM = 32
V_DIM = 64
Q_LORA = 384
KV_LORA = 256
ROPE_THETA = 10000.0
POOL_WINDOWS = (2, 4, 8, 16)
POOL_GRP = D_MODEL // len(POOL_WINDOWS)
D_FF = 4 * D_MODEL
EPS = 1e-6
ATT_SCALE = (NOPE_DIM + ROPE_DIM) ** -0.5

LANES = 128
SUBLANES = 8
HALO = 16
FFT_N2 = 128
VMEM_LIMIT = 56 * 1024 * 1024


def _cparams(sem):
    return pltpu.CompilerParams(dimension_semantics=sem, vmem_limit_bytes=VMEM_LIMIT)


def _const_spec(shape):
    nd = len(shape)
    return pl.BlockSpec(shape, lambda *_: (0,) * nd, pipeline_mode=pl.Buffered(1))


def _rms(xf, g):
    ms = jnp.mean(xf * xf, axis=-1, keepdims=True)
    return xf * lax.rsqrt(ms + EPS) * g


def _bdot(a, b):
    return jnp.dot(a.astype(BF16), b.astype(BF16), preferred_element_type=F32)


def _in_kernel(xp_ref, xc_ref, xn_ref, g_ref, why_ref, wlat_ref, cw_ref, cb_ref,
               qn_ref, wq_ref, kvn_ref, wkv_ref, vone_ref, cos_ref, sin_ref,
               x1_ref, x2_ref, v_ref, q_ref, k_ref, vv_ref, *, tm, n_tiles):
    i = pl.program_id(1)
    g = g_ref[...]
    hc = _rms(xc_ref[0], g)
    hp = _rms(xp_ref[0], g) * jnp.where(i > 0, 1.0, 0.0)
    hn = _rms(xn_ref[0], g) * jnp.where(i < n_tiles - 1, 1.0, 0.0)
    hcb = hc.astype(BF16)
    h_ext = jnp.concatenate([hp.astype(BF16), hcb, hn.astype(BF16)], axis=0)
    p = jnp.dot(h_ext, why_ref[...], preferred_element_type=F32)
    n = tm + 2 * HALO
    cw = cw_ref[...]
    u = (pltpu.roll(p, 1, 0)[HALO:HALO + tm] * cw[0:1]
         + p[HALO:HALO + tm] * cw[1:2]
         + pltpu.roll(p, n - 1, 0)[HALO:HALO + tm] * cw[2:3]
         + cb_ref[...])
    x1_ref[0] = u[:, :D_HY]
    x2_ref[0] = u[:, D_HY:2 * D_HY]
    v_ref[0] = u[:, 2 * D_HY:]

    lat = jnp.dot(hcb, wlat_ref[...], preferred_element_type=F32)
    cq = lat[:, :Q_LORA]
    ckv = lat[:, Q_LORA:Q_LORA + KV_LORA]
    kr = lat[:, Q_LORA + KV_LORA:Q_LORA + KV_LORA + LANES]
    krs = lat[:, Q_LORA + KV_LORA + LANES:]
    q2 = _bdot(_rms(cq, qn_ref[...]), wq_ref[...])
    kv2 = _bdot(_rms(ckv, kvn_ref[...]), wkv_ref[...])
    hw = N_HEADS * LANES
    cos_t = cos_ref[...]
    sin_t = sin_ref[...]
    kr_r = kr * cos_t + krs * sin_t
    v_all = kv2[:, hw:] + vone_ref[...]
    for h in range(N_HEADS):
        sl = slice(h * LANES, (h + 1) * LANES)
        qh = (q2[:, sl] * cos_t + q2[:, hw + h * LANES:hw + (h + 1) * LANES] * sin_t) * ATT_SCALE
        q_ref[0, h] = qh.astype(BF16)
        k_ref[0, h] = (kv2[:, sl] + kr_r).astype(BF16)
        vv_ref[0, h] = v_all[:, sl].astype(BF16)


def _in_call(x, g, why, wlat, cw, cb, qn, wq, kvn, wkv, vone, cos_t, sin_t, tm):
    B, L, _ = x.shape
    n_tiles = L // tm
    hb = tm // HALO
    n_hb = L // HALO
    tok = lambda b, i: (b, i, 0)
    in_specs = [
        pl.BlockSpec((1, HALO, D_MODEL), lambda b, i: (b, jnp.maximum(i * hb - 1, 0), 0)),
        pl.BlockSpec((1, tm, D_MODEL), tok),
        pl.BlockSpec((1, HALO, D_MODEL), lambda b, i: (b, jnp.minimum((i + 1) * hb, n_hb - 1), 0)),
        _const_spec(g.shape), _const_spec(why.shape), _const_spec(wlat.shape),
        _const_spec(cw.shape), _const_spec(cb.shape), _const_spec(qn.shape),
        _const_spec(wq.shape), _const_spec(kvn.shape), _const_spec(wkv.shape),
        _const_spec(vone.shape),
        pl.BlockSpec((tm, LANES), lambda b, i: (i, 0)),
        pl.BlockSpec((tm, LANES), lambda b, i: (i, 0)),
    ]
    hy = jax.ShapeDtypeStruct((B, L, D_HY), F32)
    hd = jax.ShapeDtypeStruct((B, N_HEADS, L, LANES), BF16)
    hy_spec = pl.BlockSpec((1, tm, D_HY), tok)
    hd_spec = pl.BlockSpec((1, N_HEADS, tm, LANES), lambda b, i: (b, 0, i, 0))
    return pl.pallas_call(
        functools.partial(_in_kernel, tm=tm, n_tiles=n_tiles),
        grid=(B, n_tiles),
        in_specs=in_specs,
        out_specs=[hy_spec, hy_spec, hy_spec, hd_spec, hd_spec, hd_spec],
        out_shape=[hy, hy, hy, hd, hd, hd],
        compiler_params=_cparams(("parallel", "arbitrary")),
        name="even_in",
    )(x, x, x, g, why, wlat, cw, cb, qn, wq, kvn, wkv, vone, cos_t, sin_t)


def _attn_kernel(q_ref, k_ref, v_ref, o_ref, *, tq, tk, seq):
    outs = []
    for hh in range(2):
        q = q_ref[0, hh]

        def body(j, carry, hh=hh, q=q):
            m, acc = carry
            start = pl.multiple_of(j * tk, tk)
            ks = k_ref[0, hh, pl.ds(start, tk), :]
            vs = v_ref[0, hh, pl.ds(start, tk), :]
            s = lax.dot_general(q, ks, (((1,), (1,)), ((), ())), preferred_element_type=F32)
            m_new = jnp.maximum(m, jnp.max(s, axis=1, keepdims=True))
            alpha = jnp.exp(m - m_new)
            p = jnp.exp(s - m_new)
            acc = acc * alpha + jnp.dot(p.astype(BF16), vs, preferred_element_type=F32)
            return m_new, acc

        m0 = jnp.full((tq, 1), -1e30, F32)
        acc0 = jnp.zeros((tq, LANES), F32)
        _, acc = lax.fori_loop(0, seq // tk, body, (m0, acc0))
        den = acc[:, V_DIM:V_DIM + 1] if hh == 0 else acc[:, 0:1]
        outs.append(acc / den)
    lane = lax.broadcasted_iota(jnp.int32, (tq, LANES), 1)
    o_ref[0] = jnp.where(lane < V_DIM, outs[0], outs[1])


def _attn_call(q, k, v, tq, tk):
    B, _, L, _ = q.shape
    kv_spec = pl.BlockSpec((1, 2, L, LANES), lambda b, hp, i: (b, hp, 0, 0))
    return pl.pallas_call(
        functools.partial(_attn_kernel, tq=tq, tk=tk, seq=L),
        grid=(B, N_HEADS // 2, L // tq),
        in_specs=[pl.BlockSpec((1, 2, tq, LANES), lambda b, hp, i: (b, hp, i, 0)), kv_spec, kv_spec],
        out_specs=pl.BlockSpec((1, tq, LANES), lambda b, hp, i: (b, i, hp)),
        out_shape=jax.ShapeDtypeStruct((B, L, N_HEADS * V_DIM), F32),
        compiler_params=_cparams(("parallel", "parallel", "arbitrary")),
        name="attention",
    )(q, k, v)


def _mlp(x, g_ref, w1_ref, w2_ref):
    h = _rms(x, g_ref[...]).astype(BF16)
    acc = x
    chunk = D_MODEL
    for c in range(D_FF // chunk):
        hid = jnp.dot(h, w1_ref[:, c * chunk:(c + 1) * chunk], preferred_element_type=F32)
        hid = jnp.square(jnp.maximum(hid, 0.0)).astype(BF16)
        acc = acc + jnp.dot(hid, w2_ref[c * chunk:(c + 1) * chunk, :], preferred_element_type=F32)
    return acc


def _out_kernel(x_ref, yh_ref, ya_ref, wo_ref, g_ref, w1_ref, w2_ref, o_ref):
    y = jnp.concatenate([yh_ref[0], ya_ref[0]], axis=1).astype(BF16)
    x = x_ref[0] + jnp.dot(y, wo_ref[...], preferred_element_type=F32)
    o_ref[0] = _mlp(x, g_ref, w1_ref, w2_ref)


def _out_call(x, yh, ya, wo, g, w1, w2, tm):
    B, L, _ = x.shape
    tok = lambda b, i: (b, i, 0)
    return pl.pallas_call(
        _out_kernel,
        grid=(B, L // tm),
        in_specs=[pl.BlockSpec((1, tm, D_MODEL), tok), pl.BlockSpec((1, tm, D_HY), tok),
                  pl.BlockSpec((1, tm, D_HY), tok), _const_spec(wo.shape), _const_spec(g.shape),
                  _const_spec(w1.shape), _const_spec(w2.shape)],
        out_specs=pl.BlockSpec((1, tm, D_MODEL), tok),
        out_shape=jax.ShapeDtypeStruct(x.shape, F32),
        compiler_params=_cparams(("parallel", "arbitrary")),
        name="even_out_mlp",
    )(x, yh, ya, wo, g, w1, w2)


def _odd_kernel(xp_ref, xc_ref, xn_ref, gm_ref, pw_ref, ps_ref, g_ref, w1_ref, w2_ref, fn_ref,
                o_ref, *, tm, n_tiles, seq, final):
    i = pl.program_id(1)
    gm = gm_ref[...]
    x = xc_ref[0]
    hp = _rms(xp_ref[0], gm) * jnp.where(i > 0, 1.0, 0.0)
    hn = _rms(xn_ref[0], gm) * jnp.where(i < n_tiles - 1, 1.0, 0.0)
    hc = _rms(x, gm)
    e = jnp.concatenate([hp, hc, hn], axis=0)
    n = tm + 2 * HALO
    t = i * tm + lax.broadcasted_iota(jnp.int32, (tm, 1), 0)
    ps = ps_ref[...]
    mixed = []
    for gi, w in enumerate(POOL_WINDOWS):
        sl = slice(gi * POOL_GRP, (gi + 1) * POOL_GRP)
        eg = e[:, sl]
        s = pltpu.roll(eg, 1, 0) + eg
        half = 1
        while 2 * half < w:
            s = pltpu.roll(s, half, 0) + pltpu.roll(s, n - half, 0)
            half *= 2
        cnt = (jnp.minimum(t + w // 2, seq) - jnp.maximum(t - w // 2, 0)).astype(F32)
        d = s[HALO:HALO + tm] / cnt - hc[:, sl]
        mixed.append(_bdot(d, pw_ref[gi]) * ps[:, sl])
    x = x + jnp.concatenate(mixed, axis=1)
    y = _mlp(x, g_ref, w1_ref, w2_ref)
    if final:
        y = _rms(y, fn_ref[...])
    o_ref[0] = y


def _odd_call(x, gm, pw, ps, g, w1, w2, fn, tm, final):
    B, L, _ = x.shape
    n_tiles = L // tm
    hb = tm // HALO
    n_hb = L // HALO
    tok = lambda b, i: (b, i, 0)
    return pl.pallas_call(
        functools.partial(_odd_kernel, tm=tm, n_tiles=n_tiles, seq=L, final=final),
        grid=(B, n_tiles),
        in_specs=[
            pl.BlockSpec((1, HALO, D_MODEL), lambda b, i: (b, jnp.maximum(i * hb - 1, 0), 0)),
            pl.BlockSpec((1, tm, D_MODEL), tok),
            pl.BlockSpec((1, HALO, D_MODEL), lambda b, i: (b, jnp.minimum((i + 1) * hb, n_hb - 1), 0)),
            _const_spec(gm.shape), _const_spec(pw.shape), _const_spec(ps.shape), _const_spec(g.shape),
            _const_spec(w1.shape), _const_spec(w2.shape), _const_spec(fn.shape)],
        out_specs=pl.BlockSpec((1, tm, D_MODEL), tok),
        out_shape=jax.ShapeDtypeStruct(x.shape, F32),
        compiler_params=_cparams(("parallel", "arbitrary")),
        name="odd_pool_mlp",
    )(x, x, x, gm, pw, ps, g, w1, w2, fn)


def _coarse_fwd_kernel(x_ref, g_ref, o_ref):
    x = x_ref[0, :, 0]
    a, _, cb = x.shape
    xb = x.reshape(a * SUBLANES, cb).astype(BF16)
    y = jnp.dot(g_ref[...], xb, preferred_element_type=F32)
    o_ref[0, :, 0] = y.reshape(y.shape[0] // SUBLANES, SUBLANES, cb)


def _coarse_fwd_call(x, g_a, cb):
    B, L, C = x.shape
    a = L // FFT_N2
    n1 = 2 * a
    nb = FFT_N2 // SUBLANES
    x5 = x.reshape(B, a, nb, SUBLANES, C)
    out = pl.pallas_call(
        _coarse_fwd_kernel,
        grid=(B, nb, C // cb),
        in_specs=[pl.BlockSpec((1, a, 1, SUBLANES, cb), lambda b, r, c: (b, 0, r, 0, c)),
                  _const_spec(g_a.shape)],
        out_specs=pl.BlockSpec((1, n1, 1, SUBLANES, cb), lambda b, r, c: (b, 0, r, 0, c)),
        out_shape=jax.ShapeDtypeStruct((B, n1, nb, SUBLANES, C), F32),
        compiler_params=_cparams(("parallel", "parallel", "arbitrary")),
        name="hyena_coarse_fwd",
    )(x5, g_a)
    return out.reshape(B, n1, FFT_N2, C)


def _cmul(x, kr, ki):
    xr, xi = x[:FFT_N2], x[FFT_N2:]
    return jnp.concatenate([xr * kr - xi * ki, xr * ki + xi * kr], axis=0)


def _fine_kernel(s_ref, t_ref, ta_ref, kf_ref, kfn_ref, o_ref, *, jb):
    step = pl.program_id(0)
    cb = s_ref.shape[-1]

    def generic(jj):
        s = s_ref[0, 2 * jj:2 * jj + 2].reshape(2 * FFT_N2, cb)
        x = _bdot(t_ref[jj], s)
        y = _cmul(x, kf_ref[0, jj], kf_ref[1, jj])
        z = _bdot(ta_ref[jj], y)
        o_ref[0, 2 * jj:2 * jj + 2] = z.reshape(2, FFT_N2, cb)

    def special():
        s = s_ref[0, 0:2].reshape(2 * FFT_N2, cb)
        row = lax.broadcasted_iota(jnp.int32, (2 * FFT_N2, 1), 0)
        lo = row < FFT_N2
        xa = _bdot(t_ref[0], jnp.where(lo, s, 0.0))
        xb = _bdot(t_ref[0], jnp.where(lo, 0.0, s))
        ya = _cmul(xa, kf_ref[0, 0], kf_ref[1, 0])
        yb = _cmul(xb, kfn_ref[0, 0], kfn_ref[1, 0])
        z = jnp.where(lo, _bdot(ta_ref[0], ya), _bdot(ta_ref[0], yb))
        o_ref[0, 0:2] = z.reshape(2, FFT_N2, cb)

    pl.when(step == 0)(special)
    pl.when(step != 0)(lambda: generic(0))
    for jj in range(1, jb):
        generic(jj)


def _fine_call(s, t_f, t_i, kf, kfn, filt, cb, jb):
    B, n1, _, C = s.shape
    npair = n1 // 2
    koff = filt * (C // cb)
    sspec = pl.BlockSpec((1, 2 * jb, FFT_N2, cb), lambda j, c, b: (b, j, 0, c))
    tspec = pl.BlockSpec((jb, 2 * FFT_N2, 2 * FFT_N2), lambda j, c, b: (j, 0, 0))
    return pl.pallas_call(
        functools.partial(_fine_kernel, jb=jb),
        grid=(npair // jb, C // cb, B),
        in_specs=[sspec, tspec, tspec,
                  pl.BlockSpec((2, jb, FFT_N2, cb), lambda j, c, b: (0, j, 0, koff + c)),
                  pl.BlockSpec((2, 1, FFT_N2, cb), lambda j, c, b: (0, 0, 0, koff + c))],
        out_specs=sspec,
        out_shape=jax.ShapeDtypeStruct(s.shape, F32),
        compiler_params=_cparams(("arbitrary", "arbitrary", "arbitrary")),
        name="hyena_fine",
    )(s, t_f, t_i, kf, kfn)


def _coarse_inv_kernel(z_ref, g_ref, u_ref, gate_ref, bias_ref, o_ref):
    z = z_ref[0, :, 0]
    n1, _, cb = z.shape
    zb = z.reshape(n1 * SUBLANES, cb).astype(BF16)
    y = jnp.dot(g_ref[...], zb, preferred_element_type=F32)
    y = y.reshape(y.shape[0] // SUBLANES, SUBLANES, cb)
    o_ref[0, :, 0] = (y + u_ref[0, :, 0] * bias_ref[...]) * gate_ref[0, :, 0]


def _coarse_inv_call(z, g_b, u, gate, bias, cb):
    B, n1, _, C = z.shape
    a = n1 // 2
    L = a * FFT_N2
    nb = FFT_N2 // SUBLANES
    z5 = z.reshape(B, n1, nb, SUBLANES, C)
    tspec = pl.BlockSpec((1, a, 1, SUBLANES, cb), lambda b, r, c: (b, 0, r, 0, c))
    out = pl.pallas_call(
        _coarse_inv_kernel,
        grid=(B, nb, C // cb),
        in_specs=[pl.BlockSpec((1, n1, 1, SUBLANES, cb), lambda b, r, c: (b, 0, r, 0, c)),
                  _const_spec(g_b.shape), tspec, tspec,
                  pl.BlockSpec((1, cb), lambda b, r, c: (0, c))],
        out_specs=tspec,
        out_shape=jax.ShapeDtypeStruct((B, a, nb, SUBLANES, C), F32),
        compiler_params=_cparams(("parallel", "parallel", "arbitrary")),
        name="hyena_coarse_inv",
    )(z5, g_b, u.reshape(B, a, nb, SUBLANES, C), gate.reshape(B, a, nb, SUBLANES, C), bias)
    return out.reshape(B, L, C)


def _filter_kernel(z_ref, w1_ref, b1_ref, fr_ref, w2_ref, b2_ref, w3_ref, dec_ref, o_ref, *, tl):
    hi = lax.Precision.HIGHEST
    fr = fr_ref[...]
    h = jnp.sin(fr * (jnp.dot(z_ref[...], w1_ref[...], precision=hi, preferred_element_type=F32) + b1_ref[...]))
    h = jnp.sin(fr * (jnp.dot(h, w2_ref[...], precision=hi, preferred_element_type=F32) + b2_ref[...]))
    h = jnp.dot(h, w3_ref[...], precision=hi, preferred_element_type=F32)
    dec = dec_ref[...]
    h = h * jnp.concatenate([dec] * (2 * N_FILT), axis=1)
    row = pl.program_id(0) * tl + lax.broadcasted_iota(jnp.int32, h.shape, 0)
    lane = lax.broadcasted_iota(jnp.int32, h.shape, 1)
    o_ref[...] = jnp.where((row == 0) & (lane >= N_FILT * D_HY), 0.0, h)


def _filter_call(z, w1, b1, fr, w2, b2, w3, dec, tl):
    L = z.shape[0]
    return pl.pallas_call(
        functools.partial(_filter_kernel, tl=tl),
        grid=(L // tl,),
        in_specs=[pl.BlockSpec((tl, LANES), lambda i: (i, 0)),
                  _const_spec(w1.shape), _const_spec(b1.shape), _const_spec(fr.shape),
                  _const_spec(w2.shape), _const_spec(b2.shape), _const_spec(w3.shape),
                  pl.BlockSpec((tl, D_HY), lambda i: (i, 0))],
        out_specs=pl.BlockSpec((tl, 2 * N_FILT * D_HY), lambda i: (i, 0)),
        out_shape=jax.ShapeDtypeStruct((L, 2 * N_FILT * D_HY), F32),
        compiler_params=_cparams(("arbitrary",)),
        name="hyena_filter_mlp",
    )(z, w1, b1, fr, w2, b2, w3, dec)


def _spec_kernel(sf_ref, sb_ref, t_ref, kf_ref, kfn_ref, *, jb):
    step = pl.program_id(1)
    cb = sf_ref.shape[-1]

    def combine(xf, xb):
        return jnp.concatenate([xf[:FFT_N2] + xb[:FFT_N2], xf[FFT_N2:] - xb[FFT_N2:]], axis=0)

    def generic(jj):
        xf = _bdot(t_ref[jj], sf_ref[0, 2 * jj:2 * jj + 2].reshape(2 * FFT_N2, cb))
        xb = _bdot(t_ref[jj], sb_ref[0, 2 * jj:2 * jj + 2].reshape(2 * FFT_N2, cb))
        kf_ref[:, jj] = combine(xf, xb).reshape(2, FFT_N2, cb)

    def special():
        row = lax.broadcasted_iota(jnp.int32, (2 * FFT_N2, 1), 0)
        lo = row < FFT_N2
        sf = sf_ref[0, 0:2].reshape(2 * FFT_N2, cb)
        sb = sb_ref[0, 0:2].reshape(2 * FFT_N2, cb)
        k0 = combine(_bdot(t_ref[0], jnp.where(lo, sf, 0.0)), _bdot(t_ref[0], jnp.where(lo, sb, 0.0)))
        kn = combine(_bdot(t_ref[0], jnp.where(lo, 0.0, sf)), _bdot(t_ref[0], jnp.where(lo, 0.0, sb)))
        kf_ref[:, 0] = k0.reshape(2, FFT_N2, cb)
        kfn_ref[:, 0] = kn.reshape(2, FFT_N2, cb)

    pl.when(step == 0)(special)
    pl.when(step != 0)(lambda: generic(0))
    for jj in range(1, jb):
        generic(jj)


def _spec_call(s, t_f, cb, jb):
    _, n1, _, c2 = s.shape
    npair = n1 // 2
    C = c2 // 2
    nc = C // cb
    return pl.pallas_call(
        functools.partial(_spec_kernel, jb=jb),
        grid=(nc, npair // jb),
        in_specs=[pl.BlockSpec((1, 2 * jb, FFT_N2, cb), lambda c, j: (0, j, 0, c)),
                  pl.BlockSpec((1, 2 * jb, FFT_N2, cb), lambda c, j: (0, j, 0, nc + c)),
                  pl.BlockSpec((jb, 2 * FFT_N2, 2 * FFT_N2), lambda c, j: (j, 0, 0))],
        out_specs=[pl.BlockSpec((2, jb, FFT_N2, cb), lambda c, j: (0, j, 0, c)),
                   pl.BlockSpec((2, 1, FFT_N2, cb), lambda c, j: (0, 0, 0, c))],
        out_shape=[jax.ShapeDtypeStruct((2, npair, FFT_N2, C), F32),
                   jax.ShapeDtypeStruct((2, 1, FFT_N2, C), F32)],
        compiler_params=_cparams(("arbitrary", "arbitrary")),
        name="hyena_filter_spectrum",
    )(s, s, t_f)


def _dft_tables(L):
    a_n = L // FFT_N2
    n1 = 2 * a_n
    n = 2 * L
    npair = n1 // 2
    two_pi = 2.0 * math.pi

    a = jnp.arange(a_n, dtype=jnp.int32)[None, :]
    slot = jnp.arange(n1, dtype=jnp.int32)[:, None]
    j = slot // 2
    ang = two_pi * ((a * j) % n1).astype(F32) / n1
    sign = jnp.where(a % 2 == 0, 1.0, -1.0)
    is_im = (slot % 2) == 1
    f1 = jnp.where(is_im, -jnp.sin(ang), jnp.cos(ang))
    f1 = jnp.where(slot == 0, 1.0, jnp.where(slot == 1, sign, f1))
    fb = jnp.where(is_im, -2.0 * jnp.sin(ang), 2.0 * jnp.cos(ang))
    fb = jnp.where(slot == 0, 1.0, jnp.where(slot == 1, sign, fb)) / n
    eye = jnp.eye(SUBLANES, dtype=F32)
    g_a = jnp.kron(f1, eye).astype(BF16)
    g_b = jnp.kron(fb.T, eye).astype(BF16)

    b = jnp.arange(FFT_N2, dtype=jnp.int32)[None, None, :]
    d = jnp.arange(FFT_N2, dtype=jnp.int32)[None, :, None]
    c = jnp.arange(npair + 1, dtype=jnp.int32)[:, None, None]
    phi = two_pi * ((b * (d * n1 + c)) % n).astype(F32) / n
    co, si = jnp.cos(phi), jnp.sin(phi)
    t_f = jnp.concatenate([jnp.concatenate([co, si], axis=2),
                           jnp.concatenate([-si, co], axis=2)], axis=1)[:npair]
    t0 = jnp.concatenate([jnp.concatenate([co[0], co[npair]], axis=1),
                          jnp.concatenate([-si[0], -si[npair]], axis=1)], axis=0)
    t_f = t_f.at[0].set(t0)
    t_i = jnp.swapaxes(t_f, 1, 2)
    return g_a, g_b, t_f.astype(BF16), t_i.astype(BF16)


def _filter_tables(L):
    t = jnp.linspace(0.0, 1.0, L, dtype=F32)[:, None]
    wpos = (2.0 * math.pi / L) * jnp.arange(L, dtype=F32)[:, None]
    bands = jnp.linspace(1e-4, N_BANDS - 1, N_BANDS, dtype=F32)[None, :]
    z = jnp.concatenate([t, jnp.cos(bands * wpos), -jnp.sin(bands * wpos)], axis=-1)
    z = jnp.pad(z, ((0, 0), (0, LANES - EMB_DIM)))
    max_decay = math.log(DECAY_TARGET) / DECAY_FAST
    min_decay = math.log(DECAY_TARGET) / DECAY_SLOW
    deltas = jnp.linspace(min_decay, max_decay, D_HY, dtype=F32)
    decay = jnp.exp(-t * jnp.abs(deltas))
    return z, decay


def _rope_tables(L):
    inv_freq = 1.0 / (ROPE_THETA ** (jnp.arange(0, ROPE_DIM, 2, dtype=F32) / ROPE_DIM))
    ang = jnp.arange(L, dtype=F32)[:, None] * inv_freq[None, :]
    cos, sin = jnp.cos(ang), jnp.sin(ang)
    one = jnp.ones((L, NOPE_DIM), F32)
    zero_n = jnp.zeros((L, NOPE_DIM), F32)
    zero_t = jnp.zeros((L, LANES - NOPE_DIM - ROPE_DIM), F32)
    cos_t = jnp.concatenate([one, cos, cos, zero_t], axis=1)
    sin_t = jnp.concatenate([zero_n, -sin, sin, zero_t], axis=1)
    return cos_t, sin_t


def _swap_halves(w):
    half = w.shape[-1] // 2
    return jnp.concatenate([w[..., half:], w[..., :half]], axis=-1)


def _place(w, width, offset):
    return jnp.pad(w, ((0, 0), (offset, width - offset - w.shape[1])))


def _even_weights(w_in, w_uq, w_ukv):
    why = w_in[:, :HY_COLS].astype(BF16)
    w_kr = w_in[:, HY_COLS + Q_LORA + KV_LORA:]
    wlat = jnp.concatenate([
        w_in[:, HY_COLS:HY_COLS + Q_LORA + KV_LORA],
        _place(w_kr, LANES, NOPE_DIM),
        _place(_swap_halves(w_kr), LANES, NOPE_DIM)], axis=1).astype(BF16)
    qh = w_uq.reshape(Q_LORA, N_HEADS, NOPE_DIM + ROPE_DIM)
    q_main = jnp.pad(qh, ((0, 0), (0, 0), (0, LANES - NOPE_DIM - ROPE_DIM)))
    q_sw = jnp.pad(_swap_halves(qh[..., NOPE_DIM:]),
                   ((0, 0), (0, 0), (NOPE_DIM, LANES - NOPE_DIM - ROPE_DIM)))
    wq = jnp.concatenate([q_main.reshape(Q_LORA, -1), q_sw.reshape(Q_LORA, -1)], axis=1).astype(BF16)
    kvh = w_ukv.reshape(KV_LORA, N_HEADS, NOPE_DIM + V_DIM)
    k_main = jnp.pad(kvh[..., :NOPE_DIM], ((0, 0), (0, 0), (0, LANES - NOPE_DIM)))
    vh = kvh[..., NOPE_DIM:]
    v_even = jnp.pad(vh, ((0, 0), (0, 0), (0, LANES - V_DIM)))
    v_odd = jnp.pad(vh, ((0, 0), (0, 0), (LANES - V_DIM, 0)))
    odd = (jnp.arange(N_HEADS) % 2 == 1)[None, :, None]
    v_main = jnp.where(odd, v_odd, v_even)
    wkv = jnp.concatenate([k_main.reshape(KV_LORA, -1), v_main.reshape(KV_LORA, -1)], axis=1).astype(BF16)
    lane = jnp.arange(N_HEADS * LANES)
    head_odd = (lane // LANES) % 2 == 1
    vone = jnp.where(head_odd, lane % LANES == 0, lane % LANES == V_DIM).astype(F32)[None, :]
    return why, wlat, wq, wkv, vone


def _tile(L, want):
    return min(want, L)


def _hyena(x1, x2, v, kf, kfn, hy_bias, tabs, cb, jb):
    g_a, g_b, t_f, t_i = tabs
    z = v
    for f, gate in enumerate((x1, x2)):
        s = _coarse_fwd_call(z, g_a, cb)
        s = _fine_call(s, t_f, t_i, kf, kfn, f, cb, jb)
        z = _coarse_inv_call(s, g_b, z, gate, hy_bias[f][None, :], cb)
    return z


def _filter_spectrum(L, w1, b1, fr, w2, b2, w3, tabs, ftabs, cb, jb):
    g_a, _, t_f, _ = tabs
    z, decay = ftabs
    pad_h = LANES - FILT_HID
    w1p = jnp.pad(w1, ((0, LANES - EMB_DIM), (0, pad_h)))
    w2p = jnp.pad(w2, ((0, pad_h), (0, pad_h)))
    w3p = jnp.pad(w3, ((0, pad_h), (0, 0)))
    pad1 = lambda a: jnp.pad(a, (0, pad_h))[None, :]
    hf = _filter_call(z, w1p, pad1(b1), pad1(fr), w2p, pad1(b2), w3p, decay, _tile(L, 512))
    s = _coarse_fwd_call(hf[None], g_a, cb)
    return _spec_call(s, t_f, cb, jb)


def _trunk(x, mix_norm, w_in, hy_conv_w, hy_conv_b, hf_w1, hf_b1, hf_freq, hf_w2, hf_b2, hf_w3, hy_bias,
           q_norm, w_uq, kv_norm, w_ukv, w_out, pool_w, pool_scale, mlp_norm, mlp_w1, mlp_w2, final_norm):
    B, L, _ = x.shape
    depth = mix_norm.shape[0]
    tm = _tile(L, 512)
    cb = 256
    jb = min(4, L // FFT_N2)
    tabs = _dft_tables(L)
    ftabs = _filter_tables(L)
    cos_t, sin_t = _rope_tables(L)
    row = lambda a: a[None, :]
    for i in range(depth):
        w1 = mlp_w1[i].astype(BF16)
        w2 = mlp_w2[i].astype(BF16)
        g = row(mlp_norm[i])
        if i % 2 == 0:
            e = i // 2
            why, wlat, wq, wkv, vone = _even_weights(w_in[e], w_uq[e], w_ukv[e])
            x1, x2, v, q, k, vv = _in_call(
                x, row(mix_norm[i]), why, wlat, hy_conv_w[e], row(hy_conv_b[e]),
                row(q_norm[e]), wq, row(kv_norm[e]), wkv, vone, cos_t, sin_t, tm)
            kf, kfn = _filter_spectrum(L, hf_w1[e], hf_b1[e], hf_freq[e], hf_w2[e], hf_b2[e], hf_w3[e],
                                       tabs, ftabs, cb, jb)
            y_hy = _hyena(x1, x2, v, kf, kfn, hy_bias[e], tabs, cb, jb)
            y_att = _attn_call(q, k, vv, _tile(L, 512), _tile(L, 512))
            x = _out_call(x, y_hy, y_att, w_out[e].astype(BF16), g, w1, w2, tm)
        else:
            o = i // 2
            x = _odd_call(x, row(mix_norm[i]), pool_w[o].astype(BF16), row(pool_scale[o]), g, w1, w2,
                          row(final_norm), tm, final=(i == depth - 1))
    if depth % 2 == 1:
        raise NotImplementedError("final norm is fused into the last odd layer")
    return x


def kernel(x_prompt, x_sample, mix_norm, w_in, hy_conv_w, hy_conv_b, hf_w1, hf_b1, hf_freq, hf_w2, hf_b2,
           hf_w3, hy_bias, q_norm, w_uq, kv_norm, w_ukv, w_out, pool_w, pool_scale, mlp_norm, mlp_w1,
           mlp_w2, final_norm):
    weights = (mix_norm, w_in, hy_conv_w, hy_conv_b, hf_w1, hf_b1, hf_freq, hf_w2, hf_b2, hf_w3, hy_bias,
               q_norm, w_uq, kv_norm, w_ukv, w_out, pool_w, pool_scale, mlp_norm, mlp_w1, mlp_w2, final_norm)
    return (_trunk(x_prompt, *weights), _trunk(x_sample, *weights))
```

```python
import functools
import math

import jax
import jax.numpy as jnp
from jax import lax
from jax.experimental import pallas as pl
from jax.experimental.pallas import tpu as pltpu

F32 = jnp.float32
BF16 = jnp.bfloat16

D_MODEL = 1024
D_HY = 512
N_FILT = 2
HY_COLS = (N_FILT + 1) * D_HY
EMB_DIM = 33
N_BANDS = (EMB_DIM - 1) // 2
FILT_HID = 64
DECAY_FAST = 0.3
DECAY_SLOW = 1.5
DECAY_TARGET = 1e-2
N_HEADS = 8
NOPE_DIM = 64
ROPE_DIM = 32
V_DIM = 64
Q_LORA = 384
KV_LORA = 256
ROPE_THETA = 10000.0
POOL_WINDOWS = (2, 4, 8, 16)
POOL_GRP = D_MODEL // len(POOL_WINDOWS)
D_FF = 4 * D_MODEL
EPS = 1e-6
ATT_SCALE = (NOPE_DIM + ROPE_DIM) ** -0.5
Q_SCALE = ATT_SCALE * math.log2(math.e)
VT_ROWS = 80

LANES = 128
SUBLANES = 8
HALO = 16
FFT_N2 = 128
VMEM_LIMIT = 56 * 1024 * 1024


def _cparams(sem):
    return pltpu.CompilerParams(dimension_semantics=sem, vmem_limit_bytes=VMEM_LIMIT)


def _const_spec(shape):
    nd = len(shape)
    return pl.BlockSpec(shape, lambda *_: (0,) * nd, pipeline_mode=pl.Buffered(1))


def _rms(xf, g):
    ms = jnp.mean(xf * xf, axis=-1, keepdims=True)
    return xf * lax.rsqrt(ms + EPS) * g


def _bdot(a, b):
    return jnp.dot(a.astype(BF16), b.astype(BF16), preferred_element_type=F32)


def _in_kernel(xp_ref, xc_ref, xn_ref, g_ref, why_ref, wlat_ref, cw_ref, cb_ref,
               qn_ref, wq_ref, kvn_ref, wk_ref, wvt_ref, vone_ref, cos_ref, sin_ref,
               x1_ref, x2_ref, v_ref, q_ref, k_ref, vt_ref, *, tm, n_tiles):
    i = pl.program_id(1)
    g = g_ref[...]
    hc = _rms(xc_ref[0], g)
    hp = _rms(xp_ref[0], g) * jnp.where(i > 0, 1.0, 0.0)
    hn = _rms(xn_ref[0], g) * jnp.where(i < n_tiles - 1, 1.0, 0.0)
    hcb = hc.astype(BF16)
    h_ext = jnp.concatenate([hp.astype(BF16), hcb, hn.astype(BF16)], axis=0)
    p = jnp.dot(h_ext, why_ref[...], preferred_element_type=F32)
    n = tm + 2 * HALO
    cw = cw_ref[...]
    u = (pltpu.roll(p, 1, 0)[HALO:HALO + tm] * cw[0:1]
         + p[HALO:HALO + tm] * cw[1:2]
         + pltpu.roll(p, n - 1, 0)[HALO:HALO + tm] * cw[2:3]
         + cb_ref[...])
    x1_ref[0] = u[:, :D_HY]
    x2_ref[0] = u[:, D_HY:2 * D_HY]
    v_ref[0] = u[:, 2 * D_HY:]

    lat = jnp.dot(hcb, wlat_ref[...], preferred_element_type=F32)
    cq = lat[:, :Q_LORA]
    ckv = lat[:, Q_LORA:Q_LORA + KV_LORA]
    kr = lat[:, Q_LORA + KV_LORA:Q_LORA + KV_LORA + LANES]
    krs = lat[:, Q_LORA + KV_LORA + LANES:]
    q2 = _bdot(_rms(cq, qn_ref[...]), wq_ref[...])
    ckvn = _rms(ckv, kvn_ref[...]).astype(BF16)
    k2 = jnp.dot(ckvn, wk_ref[...], preferred_element_type=F32)
    v_t = lax.dot_general(wvt_ref[...], ckvn, (((1,), (1,)), ((), ())),
                          preferred_element_type=F32) + vone_ref[...]
    hw = N_HEADS * LANES
    cos_t = cos_ref[...]
    sin_t = sin_ref[...]
    kr_r = kr * cos_t + krs * sin_t
    for h in range(N_HEADS):
        sl = slice(h * LANES, (h + 1) * LANES)
        qh = (q2[:, sl] * cos_t + q2[:, hw + h * LANES:hw + (h + 1) * LANES] * sin_t) * Q_SCALE
        q_ref[0, h] = qh.astype(BF16)
        k_ref[0, h] = (k2[:, sl] + kr_r).astype(BF16)
        vt_ref[0, h, 0] = v_t[h * VT_ROWS:(h + 1) * VT_ROWS, :].astype(BF16)


def _in_call(x, g, why, wlat, cw, cb, qn, wq, kvn, wk, wvt, vone, cos_t, sin_t, tm):
    B, L, _ = x.shape
    n_tiles = L // tm
    hb = tm // HALO
    n_hb = L // HALO
    tok = lambda b, i: (b, i, 0)
    in_specs = [
        pl.BlockSpec((1, HALO, D_MODEL), lambda b, i: (b, jnp.maximum(i * hb - 1, 0), 0)),
        pl.BlockSpec((1, tm, D_MODEL), tok),
        pl.BlockSpec((1, HALO, D_MODEL), lambda b, i: (b, jnp.minimum((i + 1) * hb, n_hb - 1), 0)),
        _const_spec(g.shape), _const_spec(why.shape), _const_spec(wlat.shape),
        _const_spec(cw.shape), _const_spec(cb.shape), _const_spec(qn.shape),
        _const_spec(wq.shape), _const_spec(kvn.shape), _const_spec(wk.shape),
        _const_spec(wvt.shape), _const_spec(vone.shape),
        pl.BlockSpec((tm, LANES), lambda b, i: (i, 0)),
        pl.BlockSpec((tm, LANES), lambda b, i: (i, 0)),
    ]
    hy = jax.ShapeDtypeStruct((B, L, D_HY), F32)
    hd = jax.ShapeDtypeStruct((B, N_HEADS, L, LANES), BF16)
    hy_spec = pl.BlockSpec((1, tm, D_HY), tok)
    hd_spec = pl.BlockSpec((1, N_HEADS, tm, LANES), lambda b, i: (b, 0, i, 0))
    vt = jax.ShapeDtypeStruct((B, N_HEADS, n_tiles, VT_ROWS, tm), BF16)
    vt_spec = pl.BlockSpec((1, N_HEADS, 1, VT_ROWS, tm), lambda b, i: (b, 0, i, 0, 0))
    return pl.pallas_call(
        functools.partial(_in_kernel, tm=tm, n_tiles=n_tiles),
        grid=(B, n_tiles),
        in_specs=in_specs,
        out_specs=[hy_spec, hy_spec, hy_spec, hd_spec, hd_spec, vt_spec],
        out_shape=[hy, hy, hy, hd, hd, vt],
        compiler_params=_cparams(("parallel", "arbitrary")),
        name="even_in",
    )(x, x, x, g, why, wlat, cw, cb, qn, wq, kvn, wk, wvt, vone, cos_t, sin_t)


def _attn_kernel(q_ref, k_ref, vt_ref, o_ref, sa_ref, sb_ref, *, tq, tk, n_chunks):
    qs = [q_ref[0, hh] for hh in range(2)]

    def scores(j, s_ref):
        start = pl.multiple_of(j * tk, tk)
        cms = []
        for hh in range(2):
            kc = k_ref[0, hh, pl.ds(start, tk), :]
            s = lax.dot_general(kc, qs[hh], (((1,), (1,)), ((), ())), preferred_element_type=F32)
            s_ref[hh] = s
            cms.append(jnp.max(s, axis=0, keepdims=True))
        return cms

    def consume(j, s_ref, cms, state):
        new = []
        for hh in range(2):
            m, acc = state[2 * hh], state[2 * hh + 1]
            m_new = jnp.maximum(m, cms[hh])
            alpha = jnp.exp2(m - m_new)
            p = jnp.exp2(s_ref[hh] - m_new).astype(BF16)
            acc = acc * alpha + jnp.dot(vt_ref[0, hh, j], p, preferred_element_type=F32)
            new += [m_new, acc]
        return new

    def two_chunks(j, state, cms_a, last):
        cms_b = scores(j + 1, sb_ref)
        state = consume(j, sa_ref, cms_a, state)
        cms_a = None if last else scores(j + 2, sa_ref)
        state = consume(j + 1, sb_ref, cms_b, state)
        return state, cms_a

    m0 = jnp.full((1, tq), -1e30, F32)
    acc0 = jnp.zeros((VT_ROWS, tq), F32)
    state = [m0, acc0, m0, acc0]
    cms = scores(0, sa_ref)
    if n_chunks == 1:
        res = consume(0, sa_ref, cms, state)
    else:
        def body(i, carry):
            st, cm = two_chunks(2 * i, list(carry[:4]), list(carry[4:]), last=False)
            return tuple(st) + tuple(cm)

        carry = lax.fori_loop(0, n_chunks // 2 - 1, body, tuple(state) + tuple(cms))
        res, _ = two_chunks(n_chunks - 2, list(carry[:4]), list(carry[4:]), last=True)
    o_t = jnp.concatenate([res[1][:V_DIM] / res[1][V_DIM:V_DIM + 1],
                           res[3][:V_DIM] / res[3][V_DIM:V_DIM + 1]], axis=0)
    o_ref[0] = o_t.T


def _attn_call(q, k, vt, tq):
    B, _, L, _ = q.shape
    _, _, n_chunks, _, tk = vt.shape
    assert n_chunks == 1 or n_chunks % 2 == 0, n_chunks
    return pl.pallas_call(
        functools.partial(_attn_kernel, tq=tq, tk=tk, n_chunks=n_chunks),
        grid=(B, N_HEADS // 2, L // tq),
        in_specs=[pl.BlockSpec((1, 2, tq, LANES), lambda b, hp, i: (b, hp, i, 0)),
                  pl.BlockSpec((1, 2, L, LANES), lambda b, hp, i: (b, hp, 0, 0)),
                  pl.BlockSpec((1, 2, n_chunks, VT_ROWS, tk), lambda b, hp, i: (b, hp, 0, 0, 0))],
        out_specs=pl.BlockSpec((1, tq, LANES), lambda b, hp, i: (b, i, hp)),
        out_shape=jax.ShapeDtypeStruct((B, L, N_HEADS * V_DIM), F32),
        scratch_shapes=[pltpu.VMEM((2, tk, tq), F32), pltpu.VMEM((2, tk, tq), F32)],
        compiler_params=_cparams(("parallel", "parallel", "arbitrary")),
        name="attention",
    )(q, k, vt)


def _mlp(x, g_ref, w1_ref, w2_ref):
    h = _rms(x, g_ref[...]).astype(BF16)
    acc = x
    chunk = D_MODEL
    for c in range(D_FF // chunk):
        hid = jnp.dot(h, w1_ref[:, c * chunk:(c + 1) * chunk], preferred_element_type=F32)
        hid = jnp.square(jnp.maximum(hid, 0.0)).astype(BF16)
        acc = acc + jnp.dot(hid, w2_ref[c * chunk:(c + 1) * chunk, :], preferred_element_type=F32)
    return acc


def _out_kernel(x_ref, yh_ref, ya_ref, wo_ref, g_ref, w1_ref, w2_ref, o_ref):
    y = jnp.concatenate([yh_ref[0], ya_ref[0]], axis=1).astype(BF16)
    x = x_ref[0] + jnp.dot(y, wo_ref[...], preferred_element_type=F32)
    o_ref[0] = _mlp(x, g_ref, w1_ref, w2_ref)


def _out_call(x, yh, ya, wo, g, w1, w2, tm):
    B, L, _ = x.shape
    tok = lambda b, i: (b, i, 0)
    return pl.pallas_call(
        _out_kernel,
        grid=(B, L // tm),
        in_specs=[pl.BlockSpec((1, tm, D_MODEL), tok), pl.BlockSpec((1, tm, D_HY), tok),
                  pl.BlockSpec((1, tm, D_HY), tok), _const_spec(wo.shape), _const_spec(g.shape),
                  _const_spec(w1.shape), _const_spec(w2.shape)],
        out_specs=pl.BlockSpec((1, tm, D_MODEL), tok),
        out_shape=jax.ShapeDtypeStruct(x.shape, F32),
        compiler_params=_cparams(("parallel", "arbitrary")),
        name="even_out_mlp",
    )(x, yh, ya, wo, g, w1, w2)


def _odd_kernel(xp_ref, xc_ref, xn_ref, gm_ref, pw_ref, ps_ref, g_ref, w1_ref, w2_ref, fn_ref,
                o_ref, *, tm, n_tiles, seq, final):
    i = pl.program_id(1)
    gm = gm_ref[...]
    x = xc_ref[0]
    hp = _rms(xp_ref[0], gm) * jnp.where(i > 0, 1.0, 0.0)
    hn = _rms(xn_ref[0], gm) * jnp.where(i < n_tiles - 1, 1.0, 0.0)
    hc = _rms(x, gm)
    e = jnp.concatenate([hp, hc, hn], axis=0)
    n = tm + 2 * HALO
    t = i * tm + lax.broadcasted_iota(jnp.int32, (tm, 1), 0)
    ps = ps_ref[...]
    mixed = []
    for gi, w in enumerate(POOL_WINDOWS):
        sl = slice(gi * POOL_GRP, (gi + 1) * POOL_GRP)
        eg = e[:, sl]
        s = pltpu.roll(eg, 1, 0) + eg
        half = 1
        while 2 * half < w:
            s = pltpu.roll(s, half, 0) + pltpu.roll(s, n - half, 0)
            half *= 2
        cnt = (jnp.minimum(t + w // 2, seq) - jnp.maximum(t - w // 2, 0)).astype(F32)
        d = s[HALO:HALO + tm] / cnt - hc[:, sl]
        mixed.append(_bdot(d, pw_ref[gi]) * ps[:, sl])
    x = x + jnp.concatenate(mixed, axis=1)
    y = _mlp(x, g_ref, w1_ref, w2_ref)
    if final:
        y = _rms(y, fn_ref[...])
    o_ref[0] = y


def _odd_call(x, gm, pw, ps, g, w1, w2, fn, tm, final):
    B, L, _ = x.shape
    n_tiles = L // tm
    hb = tm // HALO
    n_hb = L // HALO
    tok = lambda b, i: (b, i, 0)
    return pl.pallas_call(
        functools.partial(_odd_kernel, tm=tm, n_tiles=n_tiles, seq=L, final=final),
        grid=(B, n_tiles),
        in_specs=[
            pl.BlockSpec((1, HALO, D_MODEL), lambda b, i: (b, jnp.maximum(i * hb - 1, 0), 0)),
            pl.BlockSpec((1, tm, D_MODEL), tok),
            pl.BlockSpec((1, HALO, D_MODEL), lambda b, i: (b, jnp.minimum((i + 1) * hb, n_hb - 1), 0)),
            _const_spec(gm.shape), _const_spec(pw.shape), _const_spec(ps.shape), _const_spec(g.shape),
            _const_spec(w1.shape), _const_spec(w2.shape), _const_spec(fn.shape)],
        out_specs=pl.BlockSpec((1, tm, D_MODEL), tok),
        out_shape=jax.ShapeDtypeStruct(x.shape, F32),
        compiler_params=_cparams(("parallel", "arbitrary")),
        name="odd_pool_mlp",
    )(x, x, x, gm, pw, ps, g, w1, w2, fn)


def _coarse_fwd_kernel(x_ref, g_ref, o_ref):
    x = x_ref[0, :, 0]
    a, _, cb = x.shape
    xb = x.reshape(a * SUBLANES, cb).astype(BF16)
    y = jnp.dot(g_ref[...], xb, preferred_element_type=F32)
    o_ref[0, :, 0] = y.reshape(y.shape[0] // SUBLANES, SUBLANES, cb)


def _coarse_fwd_call(x, g_a, cb):
    B, L, C = x.shape
    a = L // FFT_N2
    n1 = 2 * a
    nb = FFT_N2 // SUBLANES
    x5 = x.reshape(B, a, nb, SUBLANES, C)
    out = pl.pallas_call(
        _coarse_fwd_kernel,
        grid=(B, nb, C // cb),
        in_specs=[pl.BlockSpec((1, a, 1, SUBLANES, cb), lambda b, r, c: (b, 0, r, 0, c)),
                  _const_spec(g_a.shape)],
        out_specs=pl.BlockSpec((1, n1, 1, SUBLANES, cb), lambda b, r, c: (b, 0, r, 0, c)),
        out_shape=jax.ShapeDtypeStruct((B, n1, nb, SUBLANES, C), F32),
        compiler_params=_cparams(("parallel", "parallel", "arbitrary")),
        name="hyena_coarse_fwd",
    )(x5, g_a)
    return out.reshape(B, n1, FFT_N2, C)


def _cmul(x, kr, ki):
    xr, xi = x[:FFT_N2], x[FFT_N2:]
    return jnp.concatenate([xr * kr - xi * ki, xr * ki + xi * kr], axis=0)


def _fine_kernel(s_ref, t_ref, ta_ref, kf_ref, kfn_ref, o_ref, *, jb):
    step = pl.program_id(0)
    cb = s_ref.shape[-1]

    def generic(jj):
        s = s_ref[0, 2 * jj:2 * jj + 2].reshape(2 * FFT_N2, cb)
        x = _bdot(t_ref[jj], s)
        y = _cmul(x, kf_ref[0, jj], kf_ref[1, jj])
        z = _bdot(ta_ref[jj], y)
        o_ref[0, 2 * jj:2 * jj + 2] = z.reshape(2, FFT_N2, cb)

    def special():
        s = s_ref[0, 0:2].reshape(2 * FFT_N2, cb)
        row = lax.broadcasted_iota(jnp.int32, (2 * FFT_N2, 1), 0)
        lo = row < FFT_N2
        xa = _bdot(t_ref[0], jnp.where(lo, s, 0.0))
        xb = _bdot(t_ref[0], jnp.where(lo, 0.0, s))
        ya = _cmul(xa, kf_ref[0, 0], kf_ref[1, 0])
        yb = _cmul(xb, kfn_ref[0, 0], kfn_ref[1, 0])
        z = jnp.where(lo, _bdot(ta_ref[0], ya), _bdot(ta_ref[0], yb))
        o_ref[0, 0:2] = z.reshape(2, FFT_N2, cb)

    pl.when(step == 0)(special)
    pl.when(step != 0)(lambda: generic(0))
    for jj in range(1, jb):
        generic(jj)


def _fine_call(s, t_f, t_i, kf, kfn, filt, cb, jb):
    B, n1, _, C = s.shape
    npair = n1 // 2
    koff = filt * (C // cb)
    sspec = pl.BlockSpec((1, 2 * jb, FFT_N2, cb), lambda j, c, b: (b, j, 0, c))
    tspec = pl.BlockSpec((jb, 2 * FFT_N2, 2 * FFT_N2), lambda j, c, b: (j, 0, 0))
    return pl.pallas_call(
        functools.partial(_fine_kernel, jb=jb),
        grid=(npair // jb, C // cb, B),
        in_specs=[sspec, tspec, tspec,
                  pl.BlockSpec((2, jb, FFT_N2, cb), lambda j, c, b: (0, j, 0, koff + c)),
                  pl.BlockSpec((2, 1, FFT_N2, cb), lambda j, c, b: (0, 0, 0, koff + c))],
        out_specs=sspec,
        out_shape=jax.ShapeDtypeStruct(s.shape, F32),
        compiler_params=_cparams(("arbitrary", "arbitrary", "arbitrary")),
        name="hyena_fine",
    )(s, t_f, t_i, kf, kfn)


def _coarse_inv_kernel(z_ref, g_ref, u_ref, gate_ref, bias_ref, o_ref):
    z = z_ref[0, :, 0]
    n1, _, cb = z.shape
    zb = z.reshape(n1 * SUBLANES, cb).astype(BF16)
    y = jnp.dot(g_ref[...], zb, preferred_element_type=F32)
    y = y.reshape(y.shape[0] // SUBLANES, SUBLANES, cb)
    o_ref[0, :, 0] = (y + u_ref[0, :, 0] * bias_ref[...]) * gate_ref[0, :, 0]


def _coarse_inv_call(z, g_b, u, gate, bias, cb):
    B, n1, _, C = z.shape
    a = n1 // 2
    L = a * FFT_N2
    nb = FFT_N2 // SUBLANES
    z5 = z.reshape(B, n1, nb, SUBLANES, C)
    tspec = pl.BlockSpec((1, a, 1, SUBLANES, cb), lambda b, r, c: (b, 0, r, 0, c))
    out = pl.pallas_call(
        _coarse_inv_kernel,
        grid=(B, nb, C // cb),
        in_specs=[pl.BlockSpec((1, n1, 1, SUBLANES, cb), lambda b, r, c: (b, 0, r, 0, c)),
                  _const_spec(g_b.shape), tspec, tspec,
                  pl.BlockSpec((1, cb), lambda b, r, c: (0, c))],
        out_specs=tspec,
        out_shape=jax.ShapeDtypeStruct((B, a, nb, SUBLANES, C), F32),
        compiler_params=_cparams(("parallel", "parallel", "arbitrary")),
        name="hyena_coarse_inv",
    )(z5, g_b, u.reshape(B, a, nb, SUBLANES, C), gate.reshape(B, a, nb, SUBLANES, C), bias)
    return out.reshape(B, L, C)


def _filter_kernel(z_ref, w1_ref, b1_ref, fr_ref, w2_ref, b2_ref, w3_ref, dec_ref, o_ref, *, tl):
    hi = lax.Precision.HIGHEST
    fr = fr_ref[...]
    h = jnp.sin(fr * (jnp.dot(z_ref[...], w1_ref[...], precision=hi, preferred_element_type=F32) + b1_ref[...]))
    h = jnp.sin(fr * (jnp.dot(h, w2_ref[...], precision=hi, preferred_element_type=F32) + b2_ref[...]))
    h = jnp.dot(h, w3_ref[...], precision=hi, preferred_element_type=F32)
    dec = dec_ref[...]
    h = h * jnp.concatenate([dec] * (2 * N_FILT), axis=1)
    row = pl.program_id(0) * tl + lax.broadcasted_iota(jnp.int32, h.shape, 0)
    lane = lax.broadcasted_iota(jnp.int32, h.shape, 1)
    o_ref[...] = jnp.where((row == 0) & (lane >= N_FILT * D_HY), 0.0, h)


def _filter_call(z, w1, b1, fr, w2, b2, w3, dec, tl):
    L = z.shape[0]
    return pl.pallas_call(
        functools.partial(_filter_kernel, tl=tl),
        grid=(L // tl,),
        in_specs=[pl.BlockSpec((tl, LANES), lambda i: (i, 0)),
                  _const_spec(w1.shape), _const_spec(b1.shape), _const_spec(fr.shape),
                  _const_spec(w2.shape), _const_spec(b2.shape), _const_spec(w3.shape),
                  pl.BlockSpec((tl, D_HY), lambda i: (i, 0))],
        out_specs=pl.BlockSpec((tl, 2 * N_FILT * D_HY), lambda i: (i, 0)),
        out_shape=jax.ShapeDtypeStruct((L, 2 * N_FILT * D_HY), F32),
        compiler_params=_cparams(("arbitrary",)),
        name="hyena_filter_mlp",
    )(z, w1, b1, fr, w2, b2, w3, dec)


def _spec_kernel(sf_ref, sb_ref, t_ref, kf_ref, kfn_ref, *, jb):
    step = pl.program_id(1)
    cb = sf_ref.shape[-1]

    def combine(xf, xb):
        return jnp.concatenate([xf[:FFT_N2] + xb[:FFT_N2], xf[FFT_N2:] - xb[FFT_N2:]], axis=0)

    def generic(jj):
        xf = _bdot(t_ref[jj], sf_ref[0, 2 * jj:2 * jj + 2].reshape(2 * FFT_N2, cb))
        xb = _bdot(t_ref[jj], sb_ref[0, 2 * jj:2 * jj + 2].reshape(2 * FFT_N2, cb))
        kf_ref[:, jj] = combine(xf, xb).reshape(2, FFT_N2, cb)

    def special():
        row = lax.broadcasted_iota(jnp.int32, (2 * FFT_N2, 1), 0)
        lo = row < FFT_N2
        sf = sf_ref[0, 0:2].reshape(2 * FFT_N2, cb)
        sb = sb_ref[0, 0:2].reshape(2 * FFT_N2, cb)
        k0 = combine(_bdot(t_ref[0], jnp.where(lo, sf, 0.0)), _bdot(t_ref[0], jnp.where(lo, sb, 0.0)))
        kn = combine(_bdot(t_ref[0], jnp.where(lo, 0.0, sf)), _bdot(t_ref[0], jnp.where(lo, 0.0, sb)))
        kf_ref[:, 0] = k0.reshape(2, FFT_N2, cb)
        kfn_ref[:, 0] = kn.reshape(2, FFT_N2, cb)

    pl.when(step == 0)(special)
    pl.when(step != 0)(lambda: generic(0))
    for jj in range(1, jb):
        generic(jj)


def _spec_call(s, t_f, cb, jb):
    _, n1, _, c2 = s.shape
    npair = n1 // 2
    C = c2 // 2
    nc = C // cb
    return pl.pallas_call(
        functools.partial(_spec_kernel, jb=jb),
        grid=(nc, npair // jb),
        in_specs=[pl.BlockSpec((1, 2 * jb, FFT_N2, cb), lambda c, j: (0, j, 0, c)),
                  pl.BlockSpec((1, 2 * jb, FFT_N2, cb), lambda c, j: (0, j, 0, nc + c)),
                  pl.BlockSpec((jb, 2 * FFT_N2, 2 * FFT_N2), lambda c, j: (j, 0, 0))],
        out_specs=[pl.BlockSpec((2, jb, FFT_N2, cb), lambda c, j: (0, j, 0, c)),
                   pl.BlockSpec((2, 1, FFT_N2, cb), lambda c, j: (0, 0, 0, c))],
        out_shape=[jax.ShapeDtypeStruct((2, npair, FFT_N2, C), F32),
                   jax.ShapeDtypeStruct((2, 1, FFT_N2, C), F32)],
        compiler_params=_cparams(("arbitrary", "arbitrary")),
        name="hyena_filter_spectrum",
    )(s, s, t_f)


def _dft_tables(L):
    a_n = L // FFT_N2
    n1 = 2 * a_n
    n = 2 * L
    npair = n1 // 2
    two_pi = 2.0 * math.pi

    a = jnp.arange(a_n, dtype=jnp.int32)[None, :]
    slot = jnp.arange(n1, dtype=jnp.int32)[:, None]
    j = slot // 2
    ang = two_pi * ((a * j) % n1).astype(F32) / n1
    sign = jnp.where(a % 2 == 0, 1.0, -1.0)
    is_im = (slot % 2) == 1
    f1 = jnp.where(is_im, -jnp.sin(ang), jnp.cos(ang))
    f1 = jnp.where(slot == 0, 1.0, jnp.where(slot == 1, sign, f1))
    fb = jnp.where(is_im, -2.0 * jnp.sin(ang), 2.0 * jnp.cos(ang))
    fb = jnp.where(slot == 0, 1.0, jnp.where(slot == 1, sign, fb)) / n
    eye = jnp.eye(SUBLANES, dtype=F32)
    g_a = jnp.kron(f1, eye).astype(BF16)
    g_b = jnp.kron(fb.T, eye).astype(BF16)

    b = jnp.arange(FFT_N2, dtype=jnp.int32)[None, None, :]
    d = jnp.arange(FFT_N2, dtype=jnp.int32)[None, :, None]
    c = jnp.arange(npair + 1, dtype=jnp.int32)[:, None, None]
    phi = two_pi * ((b * (d * n1 + c)) % n).astype(F32) / n
    co, si = jnp.cos(phi), jnp.sin(phi)
    t_f = jnp.concatenate([jnp.concatenate([co, si], axis=2),
                           jnp.concatenate([-si, co], axis=2)], axis=1)[:npair]
    t0 = jnp.concatenate([jnp.concatenate([co[0], co[npair]], axis=1),
                          jnp.concatenate([-si[0], -si[npair]], axis=1)], axis=0)
    t_f = t_f.at[0].set(t0)
    t_i = jnp.swapaxes(t_f, 1, 2)
    return g_a, g_b, t_f.astype(BF16), t_i.astype(BF16)


def _filter_tables(L):
    t = jnp.linspace(0.0, 1.0, L, dtype=F32)[:, None]
    wpos = (2.0 * math.pi / L) * jnp.arange(L, dtype=F32)[:, None]
    bands = jnp.linspace(1e-4, N_BANDS - 1, N_BANDS, dtype=F32)[None, :]
    z = jnp.concatenate([t, jnp.cos(bands * wpos), -jnp.sin(bands * wpos)], axis=-1)
    z = jnp.pad(z, ((0, 0), (0, LANES - EMB_DIM)))
    max_decay = math.log(DECAY_TARGET) / DECAY_FAST
    min_decay = math.log(DECAY_TARGET) / DECAY_SLOW
    deltas = jnp.linspace(min_decay, max_decay, D_HY, dtype=F32)
    decay = jnp.exp(-t * jnp.abs(deltas))
    return z, decay


def _rope_tables(L):
    inv_freq = 1.0 / (ROPE_THETA ** (jnp.arange(0, ROPE_DIM, 2, dtype=F32) / ROPE_DIM))
    ang = jnp.arange(L, dtype=F32)[:, None] * inv_freq[None, :]
    cos, sin = jnp.cos(ang), jnp.sin(ang)
    one = jnp.ones((L, NOPE_DIM), F32)
    zero_n = jnp.zeros((L, NOPE_DIM), F32)
    zero_t = jnp.zeros((L, LANES - NOPE_DIM - ROPE_DIM), F32)
    cos_t = jnp.concatenate([one, cos, cos, zero_t], axis=1)
    sin_t = jnp.concatenate([zero_n, -sin, sin, zero_t], axis=1)
    return cos_t, sin_t


def _swap_halves(w):
    half = w.shape[-1] // 2
    return jnp.concatenate([w[..., half:], w[..., :half]], axis=-1)


def _place(w, width, offset):
    return jnp.pad(w, ((0, 0), (offset, width - offset - w.shape[1])))


def _even_weights(w_in, w_uq, w_ukv):
    why = w_in[:, :HY_COLS].astype(BF16)
    w_kr = w_in[:, HY_COLS + Q_LORA + KV_LORA:]
    wlat = jnp.concatenate([
        w_in[:, HY_COLS:HY_COLS + Q_LORA + KV_LORA],
        _place(w_kr, LANES, NOPE_DIM),
        _place(_swap_halves(w_kr), LANES, NOPE_DIM)], axis=1).astype(BF16)
    qh = w_uq.reshape(Q_LORA, N_HEADS, NOPE_DIM + ROPE_DIM)
    q_main = jnp.pad(qh, ((0, 0), (0, 0), (0, LANES - NOPE_DIM - ROPE_DIM)))
    q_sw = jnp.pad(_swap_halves(qh[..., NOPE_DIM:]),
                   ((0, 0), (0, 0), (NOPE_DIM, LANES - NOPE_DIM - ROPE_DIM)))
    wq = jnp.concatenate([q_main.reshape(Q_LORA, -1), q_sw.reshape(Q_LORA, -1)], axis=1).astype(BF16)
    kvh = w_ukv.reshape(KV_LORA, N_HEADS, NOPE_DIM + V_DIM)
    k_main = jnp.pad(kvh[..., :NOPE_DIM], ((0, 0), (0, 0), (0, LANES - NOPE_DIM)))
    wk = k_main.reshape(KV_LORA, -1).astype(BF16)
    v_rows = jnp.pad(kvh[..., NOPE_DIM:], ((0, 0), (0, 0), (0, VT_ROWS - V_DIM)))
    wvt = v_rows.reshape(KV_LORA, -1).T.astype(BF16)
    vone = (jnp.arange(N_HEADS * VT_ROWS) % VT_ROWS == V_DIM).astype(F32)[:, None]
    return why, wlat, wq, wk, wvt, vone


def _tile(L, want):
    return min(want, L)


def _hyena(x1, x2, v, kf, kfn, hy_bias, tabs, cb, jb):
    g_a, g_b, t_f, t_i = tabs
    z = v
    for f, gate in enumerate((x1, x2)):
        s = _coarse_fwd_call(z, g_a, cb)
        s = _fine_call(s, t_f, t_i, kf, kfn, f, cb, jb)
        z = _coarse_inv_call(s, g_b, z, gate, hy_bias[f][None, :], cb)
    return z


def _filter_spectrum(L, w1, b1, fr, w2, b2, w3, tabs, ftabs, cb, jb):
    g_a, _, t_f, _ = tabs
    z, decay = ftabs
    pad_h = LANES - FILT_HID
    w1p = jnp.pad(w1, ((0, LANES - EMB_DIM), (0, pad_h)))
    w2p = jnp.pad(w2, ((0, pad_h), (0, pad_h)))
    w3p = jnp.pad(w3, ((0, pad_h), (0, 0)))
    pad1 = lambda a: jnp.pad(a, (0, pad_h))[None, :]
    hf = _filter_call(z, w1p, pad1(b1), pad1(fr), w2p, pad1(b2), w3p, decay, _tile(L, 512))
    s = _coarse_fwd_call(hf[None], g_a, cb)
    return _spec_call(s, t_f, cb, jb)


def _trunk(x, mix_norm, w_in, hy_conv_w, hy_conv_b, hf_w1, hf_b1, hf_freq, hf_w2, hf_b2, hf_w3, hy_bias,
           q_norm, w_uq, kv_norm, w_ukv, w_out, pool_w, pool_scale, mlp_norm, mlp_w1, mlp_w2, final_norm):
    B, L, _ = x.shape
    depth = mix_norm.shape[0]
    tm = _tile(L, 512)
    cb = 256
    jb = min(4, L // FFT_N2)
    tabs = _dft_tables(L)
    ftabs = _filter_tables(L)
    cos_t, sin_t = _rope_tables(L)
    row = lambda a: a[None, :]
    for i in range(depth):
        w1 = mlp_w1[i].astype(BF16)
        w2 = mlp_w2[i].astype(BF16)
        g = row(mlp_norm[i])
        if i % 2 == 0:
            e = i // 2
            why, wlat, wq, wk, wvt, vone = _even_weights(w_in[e], w_uq[e], w_ukv[e])
            x1, x2, v, q, k, vt = _in_call(
                x, row(mix_norm[i]), why, wlat, hy_conv_w[e], row(hy_conv_b[e]),
                row(q_norm[e]), wq, row(kv_norm[e]), wk, wvt, vone, cos_t, sin_t, tm)
            kf, kfn = _filter_spectrum(L, hf_w1[e], hf_b1[e], hf_freq[e], hf_w2[e], hf_b2[e], hf_w3[e],
                                       tabs, ftabs, cb, jb)
            y_hy = _hyena(x1, x2, v, kf, kfn, hy_bias[e], tabs, cb, jb)
            y_att = _attn_call(q, k, vt, _tile(L, 512))
            x = _out_call(x, y_hy, y_att, w_out[e].astype(BF16), g, w1, w2, tm)
        else:
            o = i // 2
            x = _odd_call(x, row(mix_norm[i]), pool_w[o].astype(BF16), row(pool_scale[o]), g, w1, w2,
                          row(final_norm), tm, final=(i == depth - 1))
    if depth % 2 == 1:
        raise NotImplementedError("final norm is fused into the last odd layer")
    return x


def kernel(x_prompt, x_sample, mix_norm, w_in, hy_conv_w, hy_conv_b, hf_w1, hf_b1, hf_freq, hf_w2, hf_b2,
           hf_w3, hy_bias, q_norm, w_uq, kv_norm, w_ukv, w_out, pool_w, pool_scale, mlp_norm, mlp_w1,
           mlp_w2, final_norm):
    weights = (mix_norm, w_in, hy_conv_w, hy_conv_b, hf_w1, hf_b1, hf_freq, hf_w2, hf_b2, hf_w3, hy_bias,
               q_norm, w_uq, kv_norm, w_ukv, w_out, pool_w, pool_scale, mlp_norm, mlp_w1, mlp_w2, final_norm)
    return (_trunk(x_prompt, *weights), _trunk(x_sample, *weights))
```

```python
import functools
import math

import jax
import jax.numpy as jnp
from jax import lax
from jax.experimental import pallas as pl
from jax.experimental.pallas import tpu as pltpu

F32 = jnp.float32
BF16 = jnp.bfloat16

D_MODEL = 1024
D_HY = 512
N_FILT = 2
HY_COLS = (N_FILT + 1) * D_HY
EMB_DIM = 33
N_BANDS = (EMB_DIM - 1) // 2
FILT_HID = 64
DECAY_FAST = 0.3
DECAY_SLOW = 1.5
DECAY_TARGET = 1e-2
N_HEADS = 8
NOPE_DIM = 64
ROPE_DIM = 32
V_DIM = 64
Q_LORA = 384
KV_LORA = 256
ROPE_THETA = 10000.0
POOL_WINDOWS = (2, 4, 8, 16)
POOL_GRP = D_MODEL // len(POOL_WINDOWS)
D_FF = 4 * D_MODEL
EPS = 1e-6
ATT_SCALE = (NOPE_DIM + ROPE_DIM) ** -0.5
Q_SCALE = ATT_SCALE * math.log2(math.e)
VT_ROWS = 80

LANES = 128
SUBLANES = 8
HALO = 16
FFT_N2 = 128
VMEM_LIMIT = 56 * 1024 * 1024


def _cparams(sem):
    return pltpu.CompilerParams(dimension_semantics=sem, vmem_limit_bytes=VMEM_LIMIT)


def _const_spec(shape):
    nd = len(shape)
    return pl.BlockSpec(shape, lambda *_: (0,) * nd, pipeline_mode=pl.Buffered(1))


def _rms(xf, g):
    ms = jnp.mean(xf * xf, axis=-1, keepdims=True)
    return xf * lax.rsqrt(ms + EPS) * g


def _bdot(a, b):
    return jnp.dot(a.astype(BF16), b.astype(BF16), preferred_element_type=F32)


def _in_kernel(xp_ref, xc_ref, xn_ref, g_ref, why_ref, wlat_ref, cw_ref, cb_ref,
               qn_ref, wq_ref, kvn_ref, wk_ref, wvt_ref, vone_ref, cos_ref, sin_ref,
               x1_ref, x2_ref, v_ref, q_ref, k_ref, vt_ref, *, tm, n_tiles):
    i = pl.program_id(1)
    g = g_ref[...]
    hc = _rms(xc_ref[0], g)
    hp = _rms(xp_ref[0], g) * jnp.where(i > 0, 1.0, 0.0)
    hn = _rms(xn_ref[0], g) * jnp.where(i < n_tiles - 1, 1.0, 0.0)
    hcb = hc.astype(BF16)
    h_ext = jnp.concatenate([hp.astype(BF16), hcb, hn.astype(BF16)], axis=0)
    p = jnp.dot(h_ext, why_ref[...], preferred_element_type=F32)
    n = tm + 2 * HALO
    cw = cw_ref[...]
    u = (pltpu.roll(p, 1, 0)[HALO:HALO + tm] * cw[0:1]
         + p[HALO:HALO + tm] * cw[1:2]
         + pltpu.roll(p, n - 1, 0)[HALO:HALO + tm] * cw[2:3]
         + cb_ref[...])
    x1_ref[0] = u[:, :D_HY]
    x2_ref[0] = u[:, D_HY:2 * D_HY]
    v_ref[0] = u[:, 2 * D_HY:]

    lat = jnp.dot(hcb, wlat_ref[...], preferred_element_type=F32)
    cq = lat[:, :Q_LORA]
    ckv = lat[:, Q_LORA:Q_LORA + KV_LORA]
    kr = lat[:, Q_LORA + KV_LORA:Q_LORA + KV_LORA + LANES]
    krs = lat[:, Q_LORA + KV_LORA + LANES:]
    q2 = _bdot(_rms(cq, qn_ref[...]), wq_ref[...])
    ckvn = _rms(ckv, kvn_ref[...]).astype(BF16)
    k2 = jnp.dot(ckvn, wk_ref[...], preferred_element_type=F32)
    v_t = lax.dot_general(wvt_ref[...], ckvn, (((1,), (1,)), ((), ())),
                          preferred_element_type=F32) + vone_ref[...]
    hw = N_HEADS * LANES
    cos_t = cos_ref[...]
    sin_t = sin_ref[...]
    kr_r = kr * cos_t + krs * sin_t
    for h in range(N_HEADS):
        sl = slice(h * LANES, (h + 1) * LANES)
        qh = (q2[:, sl] * cos_t + q2[:, hw + h * LANES:hw + (h + 1) * LANES] * sin_t) * Q_SCALE
        q_ref[0, h] = qh.astype(BF16)
        k_ref[0, h] = (k2[:, sl] + kr_r).astype(BF16)
        vt_ref[0, h, 0] = v_t[h * VT_ROWS:(h + 1) * VT_ROWS, :].astype(BF16)


def _in_call(x, g, why, wlat, cw, cb, qn, wq, kvn, wk, wvt, vone, cos_t, sin_t, tm):
    B, L, _ = x.shape
    n_tiles = L // tm
    hb = tm // HALO
    n_hb = L // HALO
    tok = lambda b, i: (b, i, 0)
    in_specs = [
        pl.BlockSpec((1, HALO, D_MODEL), lambda b, i: (b, jnp.maximum(i * hb - 1, 0), 0)),
        pl.BlockSpec((1, tm, D_MODEL), tok),
        pl.BlockSpec((1, HALO, D_MODEL), lambda b, i: (b, jnp.minimum((i + 1) * hb, n_hb - 1), 0)),
        _const_spec(g.shape), _const_spec(why.shape), _const_spec(wlat.shape),
        _const_spec(cw.shape), _const_spec(cb.shape), _const_spec(qn.shape),
        _const_spec(wq.shape), _const_spec(kvn.shape), _const_spec(wk.shape),
        _const_spec(wvt.shape), _const_spec(vone.shape),
        pl.BlockSpec((tm, LANES), lambda b, i: (i, 0)),
        pl.BlockSpec((tm, LANES), lambda b, i: (i, 0)),
    ]
    hy = jax.ShapeDtypeStruct((B, L, D_HY), F32)
    hd = jax.ShapeDtypeStruct((B, N_HEADS, L, LANES), BF16)
    hy_spec = pl.BlockSpec((1, tm, D_HY), tok)
    hd_spec = pl.BlockSpec((1, N_HEADS, tm, LANES), lambda b, i: (b, 0, i, 0))
    vt = jax.ShapeDtypeStruct((B, N_HEADS, n_tiles, VT_ROWS, tm), BF16)
    vt_spec = pl.BlockSpec((1, N_HEADS, 1, VT_ROWS, tm), lambda b, i: (b, 0, i, 0, 0))
    return pl.pallas_call(
        functools.partial(_in_kernel, tm=tm, n_tiles=n_tiles),
        grid=(B, n_tiles),
        in_specs=in_specs,
        out_specs=[hy_spec, hy_spec, hy_spec, hd_spec, hd_spec, vt_spec],
        out_shape=[hy, hy, hy, hd, hd, vt],
        compiler_params=_cparams(("parallel", "arbitrary")),
        name="even_in",
    )(x, x, x, g, why, wlat, cw, cb, qn, wq, kvn, wk, wvt, vone, cos_t, sin_t)


def _attn_kernel(q_ref, k_ref, vt_ref, o_ref, sa_ref, sb_ref, *, tq, tk, n_chunks):
    qs = [q_ref[0, hh] for hh in range(2)]

    def scores(j, s_ref):
        start = pl.multiple_of(j * tk, tk)
        cms = []
        for hh in range(2):
            kc = k_ref[0, hh, pl.ds(start, tk), :]
            s = lax.dot_general(kc, qs[hh], (((1,), (1,)), ((), ())), preferred_element_type=F32)
            s_ref[hh] = s
            cms.append(jnp.max(s, axis=0, keepdims=True))
        return cms

    def consume(j, s_ref, cms, state):
        new = []
        for hh in range(2):
            m, acc = state[2 * hh], state[2 * hh + 1]
            m_new = jnp.maximum(m, cms[hh])
            alpha = jnp.exp2(m - m_new)
            p = jnp.exp2(s_ref[hh] - m_new).astype(BF16)
            acc = acc * alpha + jnp.dot(vt_ref[0, hh, j], p, preferred_element_type=F32)
            new += [m_new, acc]
        return new

    def two_chunks(j, state, cms_a, last):
        cms_b = scores(j + 1, sb_ref)
        state = consume(j, sa_ref, cms_a, state)
        cms_a = None if last else scores(j + 2, sa_ref)
        state = consume(j + 1, sb_ref, cms_b, state)
        return state, cms_a

    m0 = jnp.full((1, tq), -1e30, F32)
    acc0 = jnp.zeros((VT_ROWS, tq), F32)
    state = [m0, acc0, m0, acc0]
    cms = scores(0, sa_ref)
    if n_chunks == 1:
        res = consume(0, sa_ref, cms, state)
    else:
        def body(i, carry):
            st, cm = two_chunks(2 * i, list(carry[:4]), list(carry[4:]), last=False)
            return tuple(st) + tuple(cm)

        carry = lax.fori_loop(0, n_chunks // 2 - 1, body, tuple(state) + tuple(cms))
        res, _ = two_chunks(n_chunks - 2, list(carry[:4]), list(carry[4:]), last=True)
    o_t = jnp.concatenate([res[1][:V_DIM] / res[1][V_DIM:V_DIM + 1],
                           res[3][:V_DIM] / res[3][V_DIM:V_DIM + 1]], axis=0)
    o_ref[0] = o_t.T


def _attn_call(q, k, vt, tq):
    B, _, L, _ = q.shape
    _, _, n_chunks, _, tk = vt.shape
    assert n_chunks == 1 or n_chunks % 2 == 0, n_chunks
    return pl.pallas_call(
        functools.partial(_attn_kernel, tq=tq, tk=tk, n_chunks=n_chunks),
        grid=(B, N_HEADS // 2, L // tq),
        in_specs=[pl.BlockSpec((1, 2, tq, LANES), lambda b, hp, i: (b, hp, i, 0)),
                  pl.BlockSpec((1, 2, L, LANES), lambda b, hp, i: (b, hp, 0, 0)),
                  pl.BlockSpec((1, 2, n_chunks, VT_ROWS, tk), lambda b, hp, i: (b, hp, 0, 0, 0))],
        out_specs=pl.BlockSpec((1, tq, LANES), lambda b, hp, i: (b, i, hp)),
        out_shape=jax.ShapeDtypeStruct((B, L, N_HEADS * V_DIM), F32),
        scratch_shapes=[pltpu.VMEM((2, tk, tq), F32), pltpu.VMEM((2, tk, tq), F32)],
        compiler_params=_cparams(("parallel", "parallel", "arbitrary")),
        name="attention",
    )(q, k, vt)


def _mlp(x, g_ref, w1_ref, w2_ref):
    h = _rms(x, g_ref[...]).astype(BF16)
    acc = x
    chunk = D_MODEL
    for c in range(D_FF // chunk):
        hid = jnp.dot(h, w1_ref[:, c * chunk:(c + 1) * chunk], preferred_element_type=F32)
        hid = jnp.square(jnp.maximum(hid, 0.0)).astype(BF16)
        acc = acc + jnp.dot(hid, w2_ref[c * chunk:(c + 1) * chunk, :], preferred_element_type=F32)
    return acc


def _out_kernel(x_ref, yh_ref, ya_ref, wo_ref, g_ref, w1_ref, w2_ref, o_ref):
    y = jnp.concatenate([yh_ref[0], ya_ref[0]], axis=1).astype(BF16)
    x = x_ref[0] + jnp.dot(y, wo_ref[...], preferred_element_type=F32)
    o_ref[0] = _mlp(x, g_ref, w1_ref, w2_ref)


def _out_call(x, yh, ya, wo, g, w1, w2, tm):
    B, L, _ = x.shape
    tok = lambda b, i: (b, i, 0)
    return pl.pallas_call(
        _out_kernel,
        grid=(B, L // tm),
        in_specs=[pl.BlockSpec((1, tm, D_MODEL), tok), pl.BlockSpec((1, tm, D_HY), tok),
                  pl.BlockSpec((1, tm, D_HY), tok), _const_spec(wo.shape), _const_spec(g.shape),
                  _const_spec(w1.shape), _const_spec(w2.shape)],
        out_specs=pl.BlockSpec((1, tm, D_MODEL), tok),
        out_shape=jax.ShapeDtypeStruct(x.shape, F32),
        compiler_params=_cparams(("parallel", "arbitrary")),
        name="even_out_mlp",
    )(x, yh, ya, wo, g, w1, w2)


def _odd_kernel(xp_ref, xc_ref, xn_ref, gm_ref, pw_ref, ps_ref, g_ref, w1_ref, w2_ref, fn_ref,
                o_ref, *, tm, n_tiles, seq, final):
    i = pl.program_id(1)
    gm = gm_ref[...]
    x = xc_ref[0]
    hp = _rms(xp_ref[0], gm) * jnp.where(i > 0, 1.0, 0.0)
    hn = _rms(xn_ref[0], gm) * jnp.where(i < n_tiles - 1, 1.0, 0.0)
    hc = _rms(x, gm)
    e = jnp.concatenate([hp, hc, hn], axis=0)
    n = tm + 2 * HALO
    t = i * tm + lax.broadcasted_iota(jnp.int32, (tm, 1), 0)
    ps = ps_ref[...]
    mixed = []
    for gi, w in enumerate(POOL_WINDOWS):
        sl = slice(gi * POOL_GRP, (gi + 1) * POOL_GRP)
        eg = e[:, sl]
        s = pltpu.roll(eg, 1, 0) + eg
        half = 1
        while 2 * half < w:
            s = pltpu.roll(s, half, 0) + pltpu.roll(s, n - half, 0)
            half *= 2
        cnt = (jnp.minimum(t + w // 2, seq) - jnp.maximum(t - w // 2, 0)).astype(F32)
        d = s[HALO:HALO + tm] / cnt - hc[:, sl]
        mixed.append(_bdot(d, pw_ref[gi]) * ps[:, sl])
    x = x + jnp.concatenate(mixed, axis=1)
    y = _mlp(x, g_ref, w1_ref, w2_ref)
    if final:
        y = _rms(y, fn_ref[...])
    o_ref[0] = y


def _odd_call(x, gm, pw, ps, g, w1, w2, fn, tm, final):
    B, L, _ = x.shape
    n_tiles = L // tm
    hb = tm // HALO
    n_hb = L // HALO
    tok = lambda b, i: (b, i, 0)
    return pl.pallas_call(
        functools.partial(_odd_kernel, tm=tm, n_tiles=n_tiles, seq=L, final=final),
        grid=(B, n_tiles),
        in_specs=[
            pl.BlockSpec((1, HALO, D_MODEL), lambda b, i: (b, jnp.maximum(i * hb - 1, 0), 0)),
            pl.BlockSpec((1, tm, D_MODEL), tok),
            pl.BlockSpec((1, HALO, D_MODEL), lambda b, i: (b, jnp.minimum((i + 1) * hb, n_hb - 1), 0)),
            _const_spec(gm.shape), _const_spec(pw.shape), _const_spec(ps.shape), _const_spec(g.shape),
            _const_spec(w1.shape), _const_spec(w2.shape), _const_spec(fn.shape)],
        out_specs=pl.BlockSpec((1, tm, D_MODEL), tok),
        out_shape=jax.ShapeDtypeStruct(x.shape, F32),
        compiler_params=_cparams(("parallel", "arbitrary")),
        name="odd_pool_mlp",
    )(x, x, x, gm, pw, ps, g, w1, w2, fn)


def _coarse_fwd_kernel(x_ref, g_ref, o_ref):
    x = x_ref[0, :, 0]
    a, _, cb = x.shape
    xb = x.reshape(a * SUBLANES, cb).astype(BF16)
    y = jnp.dot(g_ref[...], xb, preferred_element_type=F32)
    o_ref[0, :, 0] = y.reshape(y.shape[0] // SUBLANES, SUBLANES, cb)


def _coarse_fwd_call(x, g_a, cb):
    B, L, C = x.shape
    a = L // FFT_N2
    n1 = 2 * a
    nb = FFT_N2 // SUBLANES
    x5 = x.reshape(B, a, nb, SUBLANES, C)
    out = pl.pallas_call(
        _coarse_fwd_kernel,
        grid=(B, nb, C // cb),
        in_specs=[pl.BlockSpec((1, a, 1, SUBLANES, cb), lambda b, r, c: (b, 0, r, 0, c)),
                  _const_spec(g_a.shape)],
        out_specs=pl.BlockSpec((1, n1, 1, SUBLANES, cb), lambda b, r, c: (b, 0, r, 0, c)),
        out_shape=jax.ShapeDtypeStruct((B, n1, nb, SUBLANES, C), F32),
        compiler_params=_cparams(("parallel", "parallel", "arbitrary")),
        name="hyena_coarse_fwd",
    )(x5, g_a)
    return out.reshape(B, n1, FFT_N2, C)


def _cmul(x, kr, ki):
    xr, xi = x[:FFT_N2], x[FFT_N2:]
    return jnp.concatenate([xr * kr - xi * ki, xr * ki + xi * kr], axis=0)


def _conv_kernel(xf_ref, ga_ref, t_ref, ta_ref, kf_ref, kfn_ref, gb_ref, u_ref, gate_ref, bias_ref,
                 o_ref, s_ref, *, nb, nj, jb):
    t = pl.program_id(2)
    n1, _, cb = s_ref.shape

    @pl.when(t < nb)
    def _():
        x = xf_ref[0, :, 0]
        xb = x.reshape(x.shape[0] * SUBLANES, cb).astype(BF16)
        y = jnp.dot(ga_ref[...], xb, preferred_element_type=F32)
        r = pl.multiple_of(t * SUBLANES, SUBLANES)
        s_ref[:, pl.ds(r, SUBLANES), :] = y.reshape(n1, SUBLANES, cb)

    def load_pair(jj):
        return s_ref[pl.ds((t - nb) * (2 * jb) + 2 * jj, 2)].reshape(2 * FFT_N2, cb)

    def store_pair(jj, z):
        s_ref[pl.ds((t - nb) * (2 * jb) + 2 * jj, 2)] = z.reshape(2, FFT_N2, cb)

    def fine(jjs):
        xs = [_bdot(t_ref[jj], load_pair(jj)) for jj in jjs]
        ys = [_cmul(x, kf_ref[0, jj], kf_ref[1, jj]) for jj, x in zip(jjs, xs)]
        for jj, y in zip(jjs, ys):
            store_pair(jj, _bdot(ta_ref[jj], y))

    def fine_first_pair():
        s = load_pair(0)
        lo = lax.broadcasted_iota(jnp.int32, (2 * FFT_N2, 1), 0) < FFT_N2
        xa = _bdot(t_ref[0], jnp.where(lo, s, 0.0))
        xb = _bdot(t_ref[0], jnp.where(lo, 0.0, s))
        ya = _cmul(xa, kf_ref[0, 0], kf_ref[1, 0])
        yb = _cmul(xb, kfn_ref[0, 0], kfn_ref[1, 0])
        store_pair(0, jnp.where(lo, _bdot(ta_ref[0], ya), _bdot(ta_ref[0], yb)))

    @pl.when(t == nb)
    def _():
        fine_first_pair()
        fine(list(range(1, jb)))

    @pl.when((t > nb) & (t < nb + nj))
    def _():
        fine(list(range(jb)))

    @pl.when(t >= nb + nj)
    def _():
        r = pl.multiple_of((t - nb - nj) * SUBLANES, SUBLANES)
        z = s_ref[:, pl.ds(r, SUBLANES), :]
        y = jnp.dot(gb_ref[...], z.reshape(n1 * SUBLANES, cb).astype(BF16), preferred_element_type=F32)
        y = y.reshape(y.shape[0] // SUBLANES, SUBLANES, cb)
        o_ref[0, :, 0] = (y + u_ref[0, :, 0] * bias_ref[...]) * gate_ref[0, :, 0]


def _conv_call(u, gate, bias, kf, kfn, filt, tabs, cb, jb):
    g_a, g_b, t_f, t_i = tabs
    B, L, C = u.shape
    a = L // FFT_N2
    n1 = 2 * a
    nb = FFT_N2 // SUBLANES
    nj = (n1 // 2) // jb
    koff = filt * (C // cb)
    u5 = u.reshape(B, a, nb, SUBLANES, C)
    fwd_r = lambda b, c, t: (b, 0, jnp.minimum(t, nb - 1), 0, c)
    inv_r = lambda b, c, t: (b, 0, jnp.clip(t - nb - nj, 0, nb - 1), 0, c)
    fine_j = lambda t: jnp.clip(t - nb, 0, nj - 1)
    tok = (1, a, 1, SUBLANES, cb)
    tspec = pl.BlockSpec((jb, 2 * FFT_N2, 2 * FFT_N2), lambda b, c, t: (fine_j(t), 0, 0))
    out = pl.pallas_call(
        functools.partial(_conv_kernel, nb=nb, nj=nj, jb=jb),
        grid=(B, C // cb, 2 * nb + nj),
        in_specs=[pl.BlockSpec(tok, fwd_r), _const_spec(g_a.shape), tspec, tspec,
                  pl.BlockSpec((2, jb, FFT_N2, cb), lambda b, c, t: (0, fine_j(t), 0, koff + c)),
                  pl.BlockSpec((2, 1, FFT_N2, cb), lambda b, c, t: (0, 0, 0, koff + c)),
                  _const_spec(g_b.shape), pl.BlockSpec(tok, inv_r), pl.BlockSpec(tok, inv_r),
                  pl.BlockSpec((1, cb), lambda b, c, t: (0, c))],
        out_specs=pl.BlockSpec(tok, inv_r),
        out_shape=jax.ShapeDtypeStruct(u5.shape, F32),
        scratch_shapes=[pltpu.VMEM((n1, FFT_N2, cb), F32)],
        compiler_params=_cparams(("arbitrary", "arbitrary", "arbitrary")),
        name="hyena_conv",
    )(u5, g_a, t_f, t_i, kf, kfn, g_b, u5, gate.reshape(u5.shape), bias)
    return out.reshape(B, L, C)


def _filter_kernel(z_ref, w1_ref, b1_ref, fr_ref, w2_ref, b2_ref, w3_ref, dec_ref, o_ref, *, tl):
    hi = lax.Precision.HIGHEST
    fr = fr_ref[...]
    h = jnp.sin(fr * (jnp.dot(z_ref[...], w1_ref[...], precision=hi, preferred_element_type=F32) + b1_ref[...]))
    h = jnp.sin(fr * (jnp.dot(h, w2_ref[...], precision=hi, preferred_element_type=F32) + b2_ref[...]))
    h = jnp.dot(h, w3_ref[...], precision=hi, preferred_element_type=F32)
    dec = dec_ref[...]
    h = h * jnp.concatenate([dec] * (2 * N_FILT), axis=1)
    row = pl.program_id(0) * tl + lax.broadcasted_iota(jnp.int32, h.shape, 0)
    lane = lax.broadcasted_iota(jnp.int32, h.shape, 1)
    o_ref[...] = jnp.where((row == 0) & (lane >= N_FILT * D_HY), 0.0, h)


def _filter_call(z, w1, b1, fr, w2, b2, w3, dec, tl):
    L = z.shape[0]
    return pl.pallas_call(
        functools.partial(_filter_kernel, tl=tl),
        grid=(L // tl,),
        in_specs=[pl.BlockSpec((tl, LANES), lambda i: (i, 0)),
                  _const_spec(w1.shape), _const_spec(b1.shape), _const_spec(fr.shape),
                  _const_spec(w2.shape), _const_spec(b2.shape), _const_spec(w3.shape),
                  pl.BlockSpec((tl, D_HY), lambda i: (i, 0))],
        out_specs=pl.BlockSpec((tl, 2 * N_FILT * D_HY), lambda i: (i, 0)),
        out_shape=jax.ShapeDtypeStruct((L, 2 * N_FILT * D_HY), F32),
        compiler_params=_cparams(("arbitrary",)),
        name="hyena_filter_mlp",
    )(z, w1, b1, fr, w2, b2, w3, dec)


def _spec_kernel(sf_ref, sb_ref, t_ref, kf_ref, kfn_ref, *, jb):
    step = pl.program_id(1)
    cb = sf_ref.shape[-1]

    def combine(xf, xb):
        return jnp.concatenate([xf[:FFT_N2] + xb[:FFT_N2], xf[FFT_N2:] - xb[FFT_N2:]], axis=0)

    def generic(jj):
        xf = _bdot(t_ref[jj], sf_ref[0, 2 * jj:2 * jj + 2].reshape(2 * FFT_N2, cb))
        xb = _bdot(t_ref[jj], sb_ref[0, 2 * jj:2 * jj + 2].reshape(2 * FFT_N2, cb))
        kf_ref[:, jj] = combine(xf, xb).reshape(2, FFT_N2, cb)

    def special():
        row = lax.broadcasted_iota(jnp.int32, (2 * FFT_N2, 1), 0)
        lo = row < FFT_N2
        sf = sf_ref[0, 0:2].reshape(2 * FFT_N2, cb)
        sb = sb_ref[0, 0:2].reshape(2 * FFT_N2, cb)
        k0 = combine(_bdot(t_ref[0], jnp.where(lo, sf, 0.0)), _bdot(t_ref[0], jnp.where(lo, sb, 0.0)))
        kn = combine(_bdot(t_ref[0], jnp.where(lo, 0.0, sf)), _bdot(t_ref[0], jnp.where(lo, 0.0, sb)))
        kf_ref[:, 0] = k0.reshape(2, FFT_N2, cb)
        kfn_ref[:, 0] = kn.reshape(2, FFT_N2, cb)

    pl.when(step == 0)(special)
    pl.when(step != 0)(lambda: generic(0))
    for jj in range(1, jb):
        generic(jj)


def _spec_call(s, t_f, cb, jb):
    _, n1, _, c2 = s.shape
    npair = n1 // 2
    C = c2 // 2
    nc = C // cb
    return pl.pallas_call(
        functools.partial(_spec_kernel, jb=jb),
        grid=(nc, npair // jb),
        in_specs=[pl.BlockSpec((1, 2 * jb, FFT_N2, cb), lambda c, j: (0, j, 0, c)),
                  pl.BlockSpec((1, 2 * jb, FFT_N2, cb), lambda c, j: (0, j, 0, nc + c)),
                  pl.BlockSpec((jb, 2 * FFT_N2, 2 * FFT_N2), lambda c, j: (j, 0, 0))],
        out_specs=[pl.BlockSpec((2, jb, FFT_N2, cb), lambda c, j: (0, j, 0, c)),
                   pl.BlockSpec((2, 1, FFT_N2, cb), lambda c, j: (0, 0, 0, c))],
        out_shape=[jax.ShapeDtypeStruct((2, npair, FFT_N2, C), F32),
                   jax.ShapeDtypeStruct((2, 1, FFT_N2, C), F32)],
        compiler_params=_cparams(("arbitrary", "arbitrary")),
        name="hyena_filter_spectrum",
    )(s, s, t_f)


def _dft_tables(L):
    a_n = L // FFT_N2
    n1 = 2 * a_n
    n = 2 * L
    npair = n1 // 2
    two_pi = 2.0 * math.pi

    a = jnp.arange(a_n, dtype=jnp.int32)[None, :]
    slot = jnp.arange(n1, dtype=jnp.int32)[:, None]
    j = slot // 2
    ang = two_pi * ((a * j) % n1).astype(F32) / n1
    sign = jnp.where(a % 2 == 0, 1.0, -1.0)
    is_im = (slot % 2) == 1
    f1 = jnp.where(is_im, -jnp.sin(ang), jnp.cos(ang))
    f1 = jnp.where(slot == 0, 1.0, jnp.where(slot == 1, sign, f1))
    fb = jnp.where(is_im, -2.0 * jnp.sin(ang), 2.0 * jnp.cos(ang))
    fb = jnp.where(slot == 0, 1.0, jnp.where(slot == 1, sign, fb)) / n
    eye = jnp.eye(SUBLANES, dtype=F32)
    g_a = jnp.kron(f1, eye).astype(BF16)
    g_b = jnp.kron(fb.T, eye).astype(BF16)

    b = jnp.arange(FFT_N2, dtype=jnp.int32)[None, None, :]
    d = jnp.arange(FFT_N2, dtype=jnp.int32)[None, :, None]
    c = jnp.arange(npair + 1, dtype=jnp.int32)[:, None, None]
    phi = two_pi * ((b * (d * n1 + c)) % n).astype(F32) / n
    co, si = jnp.cos(phi), jnp.sin(phi)
    t_f = jnp.concatenate([jnp.concatenate([co, si], axis=2),
                           jnp.concatenate([-si, co], axis=2)], axis=1)[:npair]
    t0 = jnp.concatenate([jnp.concatenate([co[0], co[npair]], axis=1),
                          jnp.concatenate([-si[0], -si[npair]], axis=1)], axis=0)
    t_f = t_f.at[0].set(t0)
    t_i = jnp.swapaxes(t_f, 1, 2)
    return g_a, g_b, t_f.astype(BF16), t_i.astype(BF16)


def _filter_tables(L):
    t = jnp.linspace(0.0, 1.0, L, dtype=F32)[:, None]
    wpos = (2.0 * math.pi / L) * jnp.arange(L, dtype=F32)[:, None]
    bands = jnp.linspace(1e-4, N_BANDS - 1, N_BANDS, dtype=F32)[None, :]
    z = jnp.concatenate([t, jnp.cos(bands * wpos), -jnp.sin(bands * wpos)], axis=-1)
    z = jnp.pad(z, ((0, 0), (0, LANES - EMB_DIM)))
    max_decay = math.log(DECAY_TARGET) / DECAY_FAST
    min_decay = math.log(DECAY_TARGET) / DECAY_SLOW
    deltas = jnp.linspace(min_decay, max_decay, D_HY, dtype=F32)
    decay = jnp.exp(-t * jnp.abs(deltas))
    return z, decay


def _rope_tables(L):
    inv_freq = 1.0 / (ROPE_THETA ** (jnp.arange(0, ROPE_DIM, 2, dtype=F32) / ROPE_DIM))
    ang = jnp.arange(L, dtype=F32)[:, None] * inv_freq[None, :]
    cos, sin = jnp.cos(ang), jnp.sin(ang)
    one = jnp.ones((L, NOPE_DIM), F32)
    zero_n = jnp.zeros((L, NOPE_DIM), F32)
    zero_t = jnp.zeros((L, LANES - NOPE_DIM - ROPE_DIM), F32)
    cos_t = jnp.concatenate([one, cos, cos, zero_t], axis=1)
    sin_t = jnp.concatenate([zero_n, -sin, sin, zero_t], axis=1)
    return cos_t, sin_t


def _swap_halves(w):
    half = w.shape[-1] // 2
    return jnp.concatenate([w[..., half:], w[..., :half]], axis=-1)


def _place(w, width, offset):
    return jnp.pad(w, ((0, 0), (offset, width - offset - w.shape[1])))


def _even_weights(w_in, w_uq, w_ukv):
    why = w_in[:, :HY_COLS].astype(BF16)
    w_kr = w_in[:, HY_COLS + Q_LORA + KV_LORA:]
    wlat = jnp.concatenate([
        w_in[:, HY_COLS:HY_COLS + Q_LORA + KV_LORA],
        _place(w_kr, LANES, NOPE_DIM),
        _place(_swap_halves(w_kr), LANES, NOPE_DIM)], axis=1).astype(BF16)
    qh = w_uq.reshape(Q_LORA, N_HEADS, NOPE_DIM + ROPE_DIM)
    q_main = jnp.pad(qh, ((0, 0), (0, 0), (0, LANES - NOPE_DIM - ROPE_DIM)))
    q_sw = jnp.pad(_swap_halves(qh[..., NOPE_DIM:]),
                   ((0, 0), (0, 0), (NOPE_DIM, LANES - NOPE_DIM - ROPE_DIM)))
    wq = jnp.concatenate([q_main.reshape(Q_LORA, -1), q_sw.reshape(Q_LORA, -1)], axis=1).astype(BF16)
    kvh = w_ukv.reshape(KV_LORA, N_HEADS, NOPE_DIM + V_DIM)
    k_main = jnp.pad(kvh[..., :NOPE_DIM], ((0, 0), (0, 0), (0, LANES - NOPE_DIM)))
    wk = k_main.reshape(KV_LORA, -1).astype(BF16)
    v_rows = jnp.pad(kvh[..., NOPE_DIM:], ((0, 0), (0, 0), (0, VT_ROWS - V_DIM)))
    wvt = v_rows.reshape(KV_LORA, -1).T.astype(BF16)
    vone = (jnp.arange(N_HEADS * VT_ROWS) % VT_ROWS == V_DIM).astype(F32)[:, None]
    return why, wlat, wq, wk, wvt, vone


def _tile(L, want):
    return min(want, L)


def _hyena(x1, x2, v, kf, kfn, hy_bias, tabs, cb, jb):
    z = v
    for f, gate in enumerate((x1, x2)):
        z = _conv_call(z, gate, hy_bias[f][None, :], kf, kfn, f, tabs, cb, jb)
    return z


def _filter_spectrum(L, w1, b1, fr, w2, b2, w3, tabs, ftabs, cb, jb):
    g_a, _, t_f, _ = tabs
    z, decay = ftabs
    pad_h = LANES - FILT_HID
    w1p = jnp.pad(w1, ((0, LANES - EMB_DIM), (0, pad_h)))
    w2p = jnp.pad(w2, ((0, pad_h), (0, pad_h)))
    w3p = jnp.pad(w3, ((0, pad_h), (0, 0)))
    pad1 = lambda a: jnp.pad(a, (0, pad_h))[None, :]
    hf = _filter_call(z, w1p, pad1(b1), pad1(fr), w2p, pad1(b2), w3p, decay, _tile(L, 512))
    s = _coarse_fwd_call(hf[None], g_a, cb)
    return _spec_call(s, t_f, cb, jb)


def _trunk(x, mix_norm, w_in, hy_conv_w, hy_conv_b, hf_w1, hf_b1, hf_freq, hf_w2, hf_b2, hf_w3, hy_bias,
           q_norm, w_uq, kv_norm, w_ukv, w_out, pool_w, pool_scale, mlp_norm, mlp_w1, mlp_w2, final_norm):
    B, L, _ = x.shape
    depth = mix_norm.shape[0]
    tm = _tile(L, 512)
    cb = 256
    jb = min(4, L // FFT_N2)
    tabs = _dft_tables(L)
    ftabs = _filter_tables(L)
    cos_t, sin_t = _rope_tables(L)
    row = lambda a: a[None, :]
    for i in range(depth):
        w1 = mlp_w1[i].astype(BF16)
        w2 = mlp_w2[i].astype(BF16)
        g = row(mlp_norm[i])
        if i % 2 == 0:
            e = i // 2
            why, wlat, wq, wk, wvt, vone = _even_weights(w_in[e], w_uq[e], w_ukv[e])
            x1, x2, v, q, k, vt = _in_call(
                x, row(mix_norm[i]), why, wlat, hy_conv_w[e], row(hy_conv_b[e]),
                row(q_norm[e]), wq, row(kv_norm[e]), wk, wvt, vone, cos_t, sin_t, tm)
            kf, kfn = _filter_spectrum(L, hf_w1[e], hf_b1[e], hf_freq[e], hf_w2[e], hf_b2[e], hf_w3[e],
                                       tabs, ftabs, cb, jb)
            y_hy = _hyena(x1, x2, v, kf, kfn, hy_bias[e], tabs, cb, jb)
            y_att = _attn_call(q, k, vt, _tile(L, 512))
            x = _out_call(x, y_hy, y_att, w_out[e].astype(BF16), g, w1, w2, tm)
        else:
            o = i // 2
            x = _odd_call(x, row(mix_norm[i]), pool_w[o].astype(BF16), row(pool_scale[o]), g, w1, w2,
                          row(final_norm), tm, final=(i == depth - 1))
    if depth % 2 == 1:
        raise NotImplementedError("final norm is fused into the last odd layer")
    return x


def kernel(x_prompt, x_sample, mix_norm, w_in, hy_conv_w, hy_conv_b, hf_w1, hf_b1, hf_freq, hf_w2, hf_b2,
           hf_w3, hy_bias, q_norm, w_uq, kv_norm, w_ukv, w_out, pool_w, pool_scale, mlp_norm, mlp_w1,
           mlp_w2, final_norm):
    weights = (mix_norm, w_in, hy_conv_w, hy_conv_b, hf_w1, hf_b1, hf_freq, hf_w2, hf_b2, hf_w3, hy_bias,
               q_norm, w_uq, kv_norm, w_ukv, w_out, pool_w, pool_scale, mlp_norm, mlp_w1, mlp_w2, final_norm)
    return (_trunk(x_prompt, *weights), _trunk(x_sample, *weights))
```

```python
import functools
import math

import jax
import jax.numpy as jnp
from jax import lax
from jax.experimental import pallas as pl
from jax.experimental.pallas import tpu as pltpu

F32 = jnp.float32
BF16 = jnp.bfloat16

D_MODEL = 1024
D_HY = 512
N_FILT = 2
HY_COLS = (N_FILT + 1) * D_HY
EMB_DIM = 33
N_BANDS = (EMB_DIM - 1) // 2
FILT_HID = 64
DECAY_FAST = 0.3
DECAY_SLOW = 1.5
DECAY_TARGET = 1e-2
N_HEADS = 8
NOPE_DIM = 64
ROPE_DIM = 32
V_DIM = 64
Q_LORA = 384
KV_LORA = 256
ROPE_THETA = 10000.0
POOL_WINDOWS = (2, 4, 8, 16)
POOL_GRP = D_MODEL // len(POOL_WINDOWS)
D_FF = 4 * D_MODEL
EPS = 1e-6
ATT_SCALE = (NOPE_DIM + ROPE_DIM) ** -0.5
Q_SCALE = ATT_SCALE * math.log2(math.e)
VT_ROWS = 80

LANES = 128
SUBLANES = 8
HALO = 16
FFT_N2 = 128
VMEM_LIMIT = 56 * 1024 * 1024


def _cparams(sem):
    return pltpu.CompilerParams(dimension_semantics=sem, vmem_limit_bytes=VMEM_LIMIT)


def _const_spec(shape):
    nd = len(shape)
    return pl.BlockSpec(shape, lambda *_: (0,) * nd, pipeline_mode=pl.Buffered(1))


def _rms(xf, g):
    ms = jnp.mean(xf * xf, axis=-1, keepdims=True)
    return xf * lax.rsqrt(ms + EPS) * g


def _bdot(a, b):
    return jnp.dot(a.astype(BF16), b.astype(BF16), preferred_element_type=F32)


def _in_kernel(xp_ref, xc_ref, xn_ref, g_ref, why_ref, wlat_ref, cw_ref, cb_ref,
               qn_ref, wq_ref, kvn_ref, wk_ref, wvt_ref, vone_ref, cos_ref, sin_ref,
               x1_ref, x2_ref, v_ref, q_ref, k_ref, vt_ref, *, tm, n_tiles):
    i = pl.program_id(1)
    g = g_ref[...]
    hc = _rms(xc_ref[0], g)
    hp = _rms(xp_ref[0], g) * jnp.where(i > 0, 1.0, 0.0)
    hn = _rms(xn_ref[0], g) * jnp.where(i < n_tiles - 1, 1.0, 0.0)
    hcb = hc.astype(BF16)
    h_ext = jnp.concatenate([hp.astype(BF16), hcb, hn.astype(BF16)], axis=0)
    p = jnp.dot(h_ext, why_ref[...], preferred_element_type=F32)
    n = tm + 2 * HALO
    cw = cw_ref[...]
    u = (pltpu.roll(p, 1, 0)[HALO:HALO + tm] * cw[0:1]
         + p[HALO:HALO + tm] * cw[1:2]
         + pltpu.roll(p, n - 1, 0)[HALO:HALO + tm] * cw[2:3]
         + cb_ref[...])
    x1_ref[0] = u[:, :D_HY]
    x2_ref[0] = u[:, D_HY:2 * D_HY]
    v_ref[0] = u[:, 2 * D_HY:]

    lat = jnp.dot(hcb, wlat_ref[...], preferred_element_type=F32)
    cq = lat[:, :Q_LORA]
    ckv = lat[:, Q_LORA:Q_LORA + KV_LORA]
    kr = lat[:, Q_LORA + KV_LORA:Q_LORA + KV_LORA + LANES]
    krs = lat[:, Q_LORA + KV_LORA + LANES:]
    q2 = _bdot(_rms(cq, qn_ref[...]), wq_ref[...])
    ckvn = _rms(ckv, kvn_ref[...]).astype(BF16)
    k2 = jnp.dot(ckvn, wk_ref[...], preferred_element_type=F32)
    v_t = lax.dot_general(wvt_ref[...], ckvn, (((1,), (1,)), ((), ())),
                          preferred_element_type=F32) + vone_ref[...]
    hw = N_HEADS * LANES
    cos_t = cos_ref[...]
    sin_t = sin_ref[...]
    kr_r = kr * cos_t + krs * sin_t
    for h in range(N_HEADS):
        sl = slice(h * LANES, (h + 1) * LANES)
        qh = (q2[:, sl] * cos_t + q2[:, hw + h * LANES:hw + (h + 1) * LANES] * sin_t) * Q_SCALE
        q_ref[0, h] = qh.astype(BF16)
        k_ref[0, h] = (k2[:, sl] + kr_r).astype(BF16)
        vt_ref[0, h, 0] = v_t[h * VT_ROWS:(h + 1) * VT_ROWS, :].astype(BF16)


def _in_call(x, g, why, wlat, cw, cb, qn, wq, kvn, wk, wvt, vone, cos_t, sin_t, tm):
    B, L, _ = x.shape
    n_tiles = L // tm
    hb = tm // HALO
    n_hb = L // HALO
    tok = lambda b, i: (b, i, 0)
    in_specs = [
        pl.BlockSpec((1, HALO, D_MODEL), lambda b, i: (b, jnp.maximum(i * hb - 1, 0), 0)),
        pl.BlockSpec((1, tm, D_MODEL), tok),
        pl.BlockSpec((1, HALO, D_MODEL), lambda b, i: (b, jnp.minimum((i + 1) * hb, n_hb - 1), 0)),
        _const_spec(g.shape), _const_spec(why.shape), _const_spec(wlat.shape),
        _const_spec(cw.shape), _const_spec(cb.shape), _const_spec(qn.shape),
        _const_spec(wq.shape), _const_spec(kvn.shape), _const_spec(wk.shape),
        _const_spec(wvt.shape), _const_spec(vone.shape),
        pl.BlockSpec((tm, LANES), lambda b, i: (i, 0)),
        pl.BlockSpec((tm, LANES), lambda b, i: (i, 0)),
    ]
    hy = jax.ShapeDtypeStruct((B, L, D_HY), F32)
    hd = jax.ShapeDtypeStruct((B, N_HEADS, L, LANES), BF16)
    hy_spec = pl.BlockSpec((1, tm, D_HY), tok)
    hd_spec = pl.BlockSpec((1, N_HEADS, tm, LANES), lambda b, i: (b, 0, i, 0))
    vt = jax.ShapeDtypeStruct((B, N_HEADS, n_tiles, VT_ROWS, tm), BF16)
    vt_spec = pl.BlockSpec((1, N_HEADS, 1, VT_ROWS, tm), lambda b, i: (b, 0, i, 0, 0))
    return pl.pallas_call(
        functools.partial(_in_kernel, tm=tm, n_tiles=n_tiles),
        grid=(B, n_tiles),
        in_specs=in_specs,
        out_specs=[hy_spec, hy_spec, hy_spec, hd_spec, hd_spec, vt_spec],
        out_shape=[hy, hy, hy, hd, hd, vt],
        compiler_params=_cparams(("parallel", "arbitrary")),
        name="even_in",
    )(x, x, x, g, why, wlat, cw, cb, qn, wq, kvn, wk, wvt, vone, cos_t, sin_t)


def _attn_kernel(q_ref, k_ref, vt_ref, o_ref, sa_ref, sb_ref, *, tq, tk, n_chunks):
    qs = [q_ref[0, hh] for hh in range(2)]

    def scores(hh, j, s_ref):
        start = pl.multiple_of(j * tk, tk)
        kc = k_ref[0, hh, pl.ds(start, tk), :]
        s = lax.dot_general(kc, qs[hh], (((1,), (1,)), ((), ())), preferred_element_type=F32)
        s_ref[hh] = s
        return jnp.max(s, axis=0, keepdims=True)

    def consume(hh, j, s_ref, cm, m, acc):
        m_new = jnp.maximum(m, cm)
        alpha = jnp.exp2(m - m_new)
        p = jnp.exp2(s_ref[hh] - m_new).astype(BF16)
        acc = acc * alpha + jnp.dot(vt_ref[0, hh, j], p, preferred_element_type=F32)
        return m_new, acc

    def step(j, src_ref, dst_ref, state, cms, fetch):
        new_state, new_cms = [], []
        for hh in range(2):
            if fetch:
                new_cms.append(scores(hh, j + 1, dst_ref))
            new_state += consume(hh, j, src_ref, cms[hh], state[2 * hh], state[2 * hh + 1])
        return new_state, new_cms

    def run_chunks(j, state, cms, last):
        for c in range(unroll):
            src, dst = (sa_ref, sb_ref) if c % 2 == 0 else (sb_ref, sa_ref)
            state, cms = step(j + c, src, dst, state, cms, not (last and c == unroll - 1))
        return state, cms

    m0 = jnp.full((1, tq), -1e30, F32)
    acc0 = jnp.zeros((VT_ROWS, tq), F32)
    state = [m0, acc0, m0, acc0]
    cms = [scores(hh, 0, sa_ref) for hh in range(2)]
    if n_chunks == 1:
        res, _ = step(0, sa_ref, sb_ref, state, cms, False)
    else:
        unroll = 4 if n_chunks % 4 == 0 else 2

        def body(i, carry):
            st, cm = run_chunks(unroll * i, list(carry[:4]), list(carry[4:]), last=False)
            return tuple(st) + tuple(cm)

        carry = lax.fori_loop(0, n_chunks // unroll - 1, body, tuple(state) + tuple(cms))
        res, _ = run_chunks(n_chunks - unroll, list(carry[:4]), list(carry[4:]), last=True)
    o_t = jnp.concatenate([res[1][:V_DIM] / res[1][V_DIM:V_DIM + 1],
                           res[3][:V_DIM] / res[3][V_DIM:V_DIM + 1]], axis=0)
    o_ref[0] = o_t.T


def _attn_call(q, k, vt, tq):
    B, _, L, _ = q.shape
    _, _, n_chunks, _, tk = vt.shape
    assert n_chunks == 1 or n_chunks % 2 == 0, n_chunks
    return pl.pallas_call(
        functools.partial(_attn_kernel, tq=tq, tk=tk, n_chunks=n_chunks),
        grid=(B, N_HEADS // 2, L // tq),
        in_specs=[pl.BlockSpec((1, 2, tq, LANES), lambda b, hp, i: (b, hp, i, 0)),
                  pl.BlockSpec((1, 2, L, LANES), lambda b, hp, i: (b, hp, 0, 0)),
                  pl.BlockSpec((1, 2, n_chunks, VT_ROWS, tk), lambda b, hp, i: (b, hp, 0, 0, 0))],
        out_specs=pl.BlockSpec((1, tq, LANES), lambda b, hp, i: (b, i, hp)),
        out_shape=jax.ShapeDtypeStruct((B, L, N_HEADS * V_DIM), F32),
        scratch_shapes=[pltpu.VMEM((2, tk, tq), F32), pltpu.VMEM((2, tk, tq), F32)],
        compiler_params=_cparams(("parallel", "parallel", "arbitrary")),
        name="attention",
    )(q, k, vt)


def _mlp(x, g_ref, w1_ref, w2_ref):
    h = _rms(x, g_ref[...]).astype(BF16)
    acc = x
    chunk = D_MODEL
    for c in range(D_FF // chunk):
        hid = jnp.dot(h, w1_ref[:, c * chunk:(c + 1) * chunk], preferred_element_type=F32)
        hid = jnp.square(jnp.maximum(hid, 0.0)).astype(BF16)
        acc = acc + jnp.dot(hid, w2_ref[c * chunk:(c + 1) * chunk, :], preferred_element_type=F32)
    return acc


def _out_kernel(x_ref, yh_ref, ya_ref, wo_ref, g_ref, w1_ref, w2_ref, o_ref):
    y = jnp.concatenate([yh_ref[0], ya_ref[0]], axis=1).astype(BF16)
    x = x_ref[0] + jnp.dot(y, wo_ref[...], preferred_element_type=F32)
    o_ref[0] = _mlp(x, g_ref, w1_ref, w2_ref)


def _out_call(x, yh, ya, wo, g, w1, w2, tm):
    B, L, _ = x.shape
    tok = lambda b, i: (b, i, 0)
    return pl.pallas_call(
        _out_kernel,
        grid=(B, L // tm),
        in_specs=[pl.BlockSpec((1, tm, D_MODEL), tok), pl.BlockSpec((1, tm, D_HY), tok),
                  pl.BlockSpec((1, tm, D_HY), tok), _const_spec(wo.shape), _const_spec(g.shape),
                  _const_spec(w1.shape), _const_spec(w2.shape)],
        out_specs=pl.BlockSpec((1, tm, D_MODEL), tok),
        out_shape=jax.ShapeDtypeStruct(x.shape, F32),
        compiler_params=_cparams(("parallel", "arbitrary")),
        name="even_out_mlp",
    )(x, yh, ya, wo, g, w1, w2)


def _odd_kernel(xp_ref, xc_ref, xn_ref, gm_ref, pw_ref, ps_ref, g_ref, w1_ref, w2_ref, fn_ref,
                o_ref, *, tm, n_tiles, seq, final):
    i = pl.program_id(1)
    gm = gm_ref[...]
    x = xc_ref[0]
    hp = _rms(xp_ref[0], gm) * jnp.where(i > 0, 1.0, 0.0)
    hn = _rms(xn_ref[0], gm) * jnp.where(i < n_tiles - 1, 1.0, 0.0)
    hc = _rms(x, gm)
    e = jnp.concatenate([hp, hc, hn], axis=0)
    n = tm + 2 * HALO
    t = i * tm + lax.broadcasted_iota(jnp.int32, (tm, 1), 0)
    ps = ps_ref[...]
    mixed = []
    for gi, w in enumerate(POOL_WINDOWS):
        sl = slice(gi * POOL_GRP, (gi + 1) * POOL_GRP)
        eg = e[:, sl]
        s = pltpu.roll(eg, 1, 0) + eg
        half = 1
        while 2 * half < w:
            s = pltpu.roll(s, half, 0) + pltpu.roll(s, n - half, 0)
            half *= 2
        cnt = (jnp.minimum(t + w // 2, seq) - jnp.maximum(t - w // 2, 0)).astype(F32)
        d = s[HALO:HALO + tm] / cnt - hc[:, sl]
        mixed.append(_bdot(d, pw_ref[gi]) * ps[:, sl])
    x = x + jnp.concatenate(mixed, axis=1)
    y = _mlp(x, g_ref, w1_ref, w2_ref)
    if final:
        y = _rms(y, fn_ref[...])
    o_ref[0] = y


def _odd_call(x, gm, pw, ps, g, w1, w2, fn, tm, final):
    B, L, _ = x.shape
    n_tiles = L // tm
    hb = tm // HALO
    n_hb = L // HALO
    tok = lambda b, i: (b, i, 0)
    return pl.pallas_call(
        functools.partial(_odd_kernel, tm=tm, n_tiles=n_tiles, seq=L, final=final),
        grid=(B, n_tiles),
        in_specs=[
            pl.BlockSpec((1, HALO, D_MODEL), lambda b, i: (b, jnp.maximum(i * hb - 1, 0), 0)),
            pl.BlockSpec((1, tm, D_MODEL), tok),
            pl.BlockSpec((1, HALO, D_MODEL), lambda b, i: (b, jnp.minimum((i + 1) * hb, n_hb - 1), 0)),
            _const_spec(gm.shape), _const_spec(pw.shape), _const_spec(ps.shape), _const_spec(g.shape),
            _const_spec(w1.shape), _const_spec(w2.shape), _const_spec(fn.shape)],
        out_specs=pl.BlockSpec((1, tm, D_MODEL), tok),
        out_shape=jax.ShapeDtypeStruct(x.shape, F32),
        compiler_params=_cparams(("parallel", "arbitrary")),
        name="odd_pool_mlp",
    )(x, x, x, gm, pw, ps, g, w1, w2, fn)


def _coarse_fwd_kernel(x_ref, g_ref, o_ref):
    x = x_ref[0, :, 0]
    a, _, cb = x.shape
    xb = x.reshape(a * SUBLANES, cb).astype(BF16)
    y = jnp.dot(g_ref[...], xb, preferred_element_type=F32)
    o_ref[0, :, 0] = y.reshape(y.shape[0] // SUBLANES, SUBLANES, cb)


def _coarse_fwd_call(x, g_a, cb):
    B, L, C = x.shape
    a = L // FFT_N2
    n1 = 2 * a
    nb = FFT_N2 // SUBLANES
    x5 = x.reshape(B, a, nb, SUBLANES, C)
    out = pl.pallas_call(
        _coarse_fwd_kernel,
        grid=(B, nb, C // cb),
        in_specs=[pl.BlockSpec((1, a, 1, SUBLANES, cb), lambda b, r, c: (b, 0, r, 0, c)),
                  _const_spec(g_a.shape)],
        out_specs=pl.BlockSpec((1, n1, 1, SUBLANES, cb), lambda b, r, c: (b, 0, r, 0, c)),
        out_shape=jax.ShapeDtypeStruct((B, n1, nb, SUBLANES, C), F32),
        compiler_params=_cparams(("parallel", "parallel", "arbitrary")),
        name="hyena_coarse_fwd",
    )(x5, g_a)
    return out.reshape(B, n1, FFT_N2, C)


def _cmul(x, kr, ki):
    xr, xi = x[:FFT_N2], x[FFT_N2:]
    return jnp.concatenate([xr * kr - xi * ki, xr * ki + xi * kr], axis=0)


def _conv_kernel(xf_ref, ga_ref, t_ref, ta_ref, kf_ref, kfn_ref, gb_ref, u_ref, gate_ref, bias_ref,
                 o_ref, s_ref, *, nb, nj, jb):
    t = pl.program_id(2)
    n1, _, cb = s_ref.shape

    @pl.when(t < nb)
    def _():
        x = xf_ref[0, :, 0]
        xb = x.reshape(x.shape[0] * SUBLANES, cb).astype(BF16)
        y = jnp.dot(ga_ref[...], xb, preferred_element_type=F32)
        r = pl.multiple_of(t * SUBLANES, SUBLANES)
        s_ref[:, pl.ds(r, SUBLANES), :] = y.reshape(n1, SUBLANES, cb)

    def pair(jj):
        return (t - nb) * jb + jj

    def load_pair(jj):
        return s_ref[pl.ds(2 * pair(jj), 2)].reshape(2 * FFT_N2, cb)

    def store_pair(jj, z):
        s_ref[pl.ds(2 * pair(jj), 2)] = z.reshape(2, FFT_N2, cb)

    def fine(jjs):
        xs = [_bdot(t_ref[pair(jj)], load_pair(jj)) for jj in jjs]
        ys = [_cmul(x, kf_ref[0, pair(jj)], kf_ref[1, pair(jj)]) for jj, x in zip(jjs, xs)]
        for jj, y in zip(jjs, ys):
            store_pair(jj, _bdot(ta_ref[pair(jj)], y))

    def fine_first_pair():
        s = load_pair(0)
        lo = lax.broadcasted_iota(jnp.int32, (2 * FFT_N2, 1), 0) < FFT_N2
        xa = _bdot(t_ref[0], jnp.where(lo, s, 0.0))
        xb = _bdot(t_ref[0], jnp.where(lo, 0.0, s))
        ya = _cmul(xa, kf_ref[0, 0], kf_ref[1, 0])
        yb = _cmul(xb, kfn_ref[0, 0], kfn_ref[1, 0])
        store_pair(0, jnp.where(lo, _bdot(ta_ref[0], ya), _bdot(ta_ref[0], yb)))

    @pl.when(t == nb)
    def _():
        fine_first_pair()
        fine(list(range(1, jb)))

    @pl.when((t > nb) & (t < nb + nj))
    def _():
        fine(list(range(jb)))

    @pl.when(t >= nb + nj)
    def _():
        r = pl.multiple_of((t - nb - nj) * SUBLANES, SUBLANES)
        z = s_ref[:, pl.ds(r, SUBLANES), :]
        y = jnp.dot(gb_ref[...], z.reshape(n1 * SUBLANES, cb).astype(BF16), preferred_element_type=F32)
        y = y.reshape(y.shape[0] // SUBLANES, SUBLANES, cb)
        o_ref[0, :, 0] = (y + u_ref[0, :, 0] * bias_ref[...]) * gate_ref[0, :, 0]


def _conv_call(u, gate, bias, kf, kfn, filt, tabs, cb, jb):
    g_a, g_b, t_f, t_i = tabs
    B, L, C = u.shape
    a = L // FFT_N2
    n1 = 2 * a
    nb = FFT_N2 // SUBLANES
    nj = (n1 // 2) // jb
    koff = filt * (C // cb)
    u5 = u.reshape(B, a, nb, SUBLANES, C)
    fwd_r = lambda c, b, t: (b, 0, jnp.minimum(t, nb - 1), 0, c)
    inv_r = lambda c, b, t: (b, 0, jnp.clip(t - nb - nj, 0, nb - 1), 0, c)
    tok = (1, a, 1, SUBLANES, cb)
    kspec = lambda n: pl.BlockSpec((2, n, FFT_N2, cb), lambda c, b, t: (0, 0, 0, koff + c),
                                   pipeline_mode=pl.Buffered(1))
    out = pl.pallas_call(
        functools.partial(_conv_kernel, nb=nb, nj=nj, jb=jb),
        grid=(C // cb, B, 2 * nb + nj),
        in_specs=[pl.BlockSpec(tok, fwd_r), _const_spec(g_a.shape), _const_spec(t_f.shape),
                  _const_spec(t_i.shape), kspec(n1 // 2), kspec(1),
                  _const_spec(g_b.shape), pl.BlockSpec(tok, inv_r), pl.BlockSpec(tok, inv_r),
                  pl.BlockSpec((1, cb), lambda c, b, t: (0, c))],
        out_specs=pl.BlockSpec(tok, inv_r),
        out_shape=jax.ShapeDtypeStruct(u5.shape, F32),
        scratch_shapes=[pltpu.VMEM((n1, FFT_N2, cb), F32)],
        compiler_params=_cparams(("arbitrary", "arbitrary", "arbitrary")),
        name="hyena_conv",
    )(u5, g_a, t_f, t_i, kf, kfn, g_b, u5, gate.reshape(u5.shape), bias)
    return out.reshape(B, L, C)


def _filter_kernel(z_ref, w1_ref, b1_ref, fr_ref, w2_ref, b2_ref, w3_ref, dec_ref, o_ref, *, tl):
    hi = lax.Precision.HIGHEST
    fr = fr_ref[...]
    h = jnp.sin(fr * (jnp.dot(z_ref[...], w1_ref[...], precision=hi, preferred_element_type=F32) + b1_ref[...]))
    h = jnp.sin(fr * (jnp.dot(h, w2_ref[...], precision=hi, preferred_element_type=F32) + b2_ref[...]))
    h = jnp.dot(h, w3_ref[...], precision=hi, preferred_element_type=F32)
    dec = dec_ref[...]
    h = h * jnp.concatenate([dec] * (2 * N_FILT), axis=1)
    row = pl.program_id(0) * tl + lax.broadcasted_iota(jnp.int32, h.shape, 0)
    lane = lax.broadcasted_iota(jnp.int32, h.shape, 1)
    o_ref[...] = jnp.where((row == 0) & (lane >= N_FILT * D_HY), 0.0, h)


def _filter_call(z, w1, b1, fr, w2, b2, w3, dec, tl):
    L = z.shape[0]
    return pl.pallas_call(
        functools.partial(_filter_kernel, tl=tl),
        grid=(L // tl,),
        in_specs=[pl.BlockSpec((tl, LANES), lambda i: (i, 0)),
                  _const_spec(w1.shape), _const_spec(b1.shape), _const_spec(fr.shape),
                  _const_spec(w2.shape), _const_spec(b2.shape), _const_spec(w3.shape),
                  pl.BlockSpec((tl, D_HY), lambda i: (i, 0))],
        out_specs=pl.BlockSpec((tl, 2 * N_FILT * D_HY), lambda i: (i, 0)),
        out_shape=jax.ShapeDtypeStruct((L, 2 * N_FILT * D_HY), F32),
        compiler_params=_cparams(("arbitrary",)),
        name="hyena_filter_mlp",
    )(z, w1, b1, fr, w2, b2, w3, dec)


def _spec_kernel(sf_ref, sb_ref, t_ref, kf_ref, kfn_ref, *, jb):
    step = pl.program_id(1)
    cb = sf_ref.shape[-1]

    def combine(xf, xb):
        return jnp.concatenate([xf[:FFT_N2] + xb[:FFT_N2], xf[FFT_N2:] - xb[FFT_N2:]], axis=0)

    def generic(jj):
        xf = _bdot(t_ref[jj], sf_ref[0, 2 * jj:2 * jj + 2].reshape(2 * FFT_N2, cb))
        xb = _bdot(t_ref[jj], sb_ref[0, 2 * jj:2 * jj + 2].reshape(2 * FFT_N2, cb))
        kf_ref[:, jj] = combine(xf, xb).reshape(2, FFT_N2, cb).astype(kf_ref.dtype)

    def special():
        row = lax.broadcasted_iota(jnp.int32, (2 * FFT_N2, 1), 0)
        lo = row < FFT_N2
        sf = sf_ref[0, 0:2].reshape(2 * FFT_N2, cb)
        sb = sb_ref[0, 0:2].reshape(2 * FFT_N2, cb)
        k0 = combine(_bdot(t_ref[0], jnp.where(lo, sf, 0.0)), _bdot(t_ref[0], jnp.where(lo, sb, 0.0)))
        kn = combine(_bdot(t_ref[0], jnp.where(lo, 0.0, sf)), _bdot(t_ref[0], jnp.where(lo, 0.0, sb)))
        kf_ref[:, 0] = k0.reshape(2, FFT_N2, cb).astype(kf_ref.dtype)
        kfn_ref[:, 0] = kn.reshape(2, FFT_N2, cb).astype(kfn_ref.dtype)

    pl.when(step == 0)(special)
    pl.when(step != 0)(lambda: generic(0))
    for jj in range(1, jb):
        generic(jj)


def _spec_call(s, t_f, cb, jb):
    _, n1, _, c2 = s.shape
    npair = n1 // 2
    C = c2 // 2
    nc = C // cb
    return pl.pallas_call(
        functools.partial(_spec_kernel, jb=jb),
        grid=(nc, npair // jb),
        in_specs=[pl.BlockSpec((1, 2 * jb, FFT_N2, cb), lambda c, j: (0, j, 0, c)),
                  pl.BlockSpec((1, 2 * jb, FFT_N2, cb), lambda c, j: (0, j, 0, nc + c)),
                  pl.BlockSpec((jb, 2 * FFT_N2, 2 * FFT_N2), lambda c, j: (j, 0, 0))],
        out_specs=[pl.BlockSpec((2, jb, FFT_N2, cb), lambda c, j: (0, j, 0, c)),
                   pl.BlockSpec((2, 1, FFT_N2, cb), lambda c, j: (0, 0, 0, c))],
        out_shape=[jax.ShapeDtypeStruct((2, npair, FFT_N2, C), BF16),
                   jax.ShapeDtypeStruct((2, 1, FFT_N2, C), BF16)],
        compiler_params=_cparams(("arbitrary", "arbitrary")),
        name="hyena_filter_spectrum",
    )(s, s, t_f)


def _dft_tables(L):
    a_n = L // FFT_N2
    n1 = 2 * a_n
    n = 2 * L
    npair = n1 // 2
    two_pi = 2.0 * math.pi

    a = jnp.arange(a_n, dtype=jnp.int32)[None, :]
    slot = jnp.arange(n1, dtype=jnp.int32)[:, None]
    j = slot // 2
    ang = two_pi * ((a * j) % n1).astype(F32) / n1
    sign = jnp.where(a % 2 == 0, 1.0, -1.0)
    is_im = (slot % 2) == 1
    f1 = jnp.where(is_im, -jnp.sin(ang), jnp.cos(ang))
    f1 = jnp.where(slot == 0, 1.0, jnp.where(slot == 1, sign, f1))
    fb = jnp.where(is_im, -2.0 * jnp.sin(ang), 2.0 * jnp.cos(ang))
    fb = jnp.where(slot == 0, 1.0, jnp.where(slot == 1, sign, fb)) / n
    eye = jnp.eye(SUBLANES, dtype=F32)
    g_a = jnp.kron(f1, eye).astype(BF16)
    g_b = jnp.kron(fb.T, eye).astype(BF16)

    b = jnp.arange(FFT_N2, dtype=jnp.int32)[None, None, :]
    d = jnp.arange(FFT_N2, dtype=jnp.int32)[None, :, None]
    c = jnp.arange(npair + 1, dtype=jnp.int32)[:, None, None]
    phi = two_pi * ((b * (d * n1 + c)) % n).astype(F32) / n
    co, si = jnp.cos(phi), jnp.sin(phi)
    t_f = jnp.concatenate([jnp.concatenate([co, si], axis=2),
                           jnp.concatenate([-si, co], axis=2)], axis=1)[:npair]
    t0 = jnp.concatenate([jnp.concatenate([co[0], co[npair]], axis=1),
                          jnp.concatenate([-si[0], -si[npair]], axis=1)], axis=0)
    t_f = t_f.at[0].set(t0)
    t_i = jnp.swapaxes(t_f, 1, 2)
    return g_a, g_b, t_f.astype(BF16), t_i.astype(BF16)


def _filter_tables(L):
    t = jnp.linspace(0.0, 1.0, L, dtype=F32)[:, None]
    wpos = (2.0 * math.pi / L) * jnp.arange(L, dtype=F32)[:, None]
    bands = jnp.linspace(1e-4, N_BANDS - 1, N_BANDS, dtype=F32)[None, :]
    z = jnp.concatenate([t, jnp.cos(bands * wpos), -jnp.sin(bands * wpos)], axis=-1)
    z = jnp.pad(z, ((0, 0), (0, LANES - EMB_DIM)))
    max_decay = math.log(DECAY_TARGET) / DECAY_FAST
    min_decay = math.log(DECAY_TARGET) / DECAY_SLOW
    deltas = jnp.linspace(min_decay, max_decay, D_HY, dtype=F32)
    decay = jnp.exp(-t * jnp.abs(deltas))
    return z, decay


def _rope_tables(L):
    inv_freq = 1.0 / (ROPE_THETA ** (jnp.arange(0, ROPE_DIM, 2, dtype=F32) / ROPE_DIM))
    ang = jnp.arange(L, dtype=F32)[:, None] * inv_freq[None, :]
    cos, sin = jnp.cos(ang), jnp.sin(ang)
    one = jnp.ones((L, NOPE_DIM), F32)
    zero_n = jnp.zeros((L, NOPE_DIM), F32)
    zero_t = jnp.zeros((L, LANES - NOPE_DIM - ROPE_DIM), F32)
    cos_t = jnp.concatenate([one, cos, cos, zero_t], axis=1)
    sin_t = jnp.concatenate([zero_n, -sin, sin, zero_t], axis=1)
    return cos_t, sin_t


def _swap_halves(w):
    half = w.shape[-1] // 2
    return jnp.concatenate([w[..., half:], w[..., :half]], axis=-1)


def _place(w, width, offset):
    return jnp.pad(w, ((0, 0), (offset, width - offset - w.shape[1])))


def _even_weights(w_in, w_uq, w_ukv):
    why = w_in[:, :HY_COLS].astype(BF16)
    w_kr = w_in[:, HY_COLS + Q_LORA + KV_LORA:]
    wlat = jnp.concatenate([
        w_in[:, HY_COLS:HY_COLS + Q_LORA + KV_LORA],
        _place(w_kr, LANES, NOPE_DIM),
        _place(_swap_halves(w_kr), LANES, NOPE_DIM)], axis=1).astype(BF16)
    qh = w_uq.reshape(Q_LORA, N_HEADS, NOPE_DIM + ROPE_DIM)
    q_main = jnp.pad(qh, ((0, 0), (0, 0), (0, LANES - NOPE_DIM - ROPE_DIM)))
    q_sw = jnp.pad(_swap_halves(qh[..., NOPE_DIM:]),
                   ((0, 0), (0, 0), (NOPE_DIM, LANES - NOPE_DIM - ROPE_DIM)))
    wq = jnp.concatenate([q_main.reshape(Q_LORA, -1), q_sw.reshape(Q_LORA, -1)], axis=1).astype(BF16)
    kvh = w_ukv.reshape(KV_LORA, N_HEADS, NOPE_DIM + V_DIM)
    k_main = jnp.pad(kvh[..., :NOPE_DIM], ((0, 0), (0, 0), (0, LANES - NOPE_DIM)))
    wk = k_main.reshape(KV_LORA, -1).astype(BF16)
    v_rows = jnp.pad(kvh[..., NOPE_DIM:], ((0, 0), (0, 0), (0, VT_ROWS - V_DIM)))
    wvt = v_rows.reshape(KV_LORA, -1).T.astype(BF16)
    vone = (jnp.arange(N_HEADS * VT_ROWS) % VT_ROWS == V_DIM).astype(F32)[:, None]
    return why, wlat, wq, wk, wvt, vone


def _tile(L, want):
    return min(want, L)


def _hyena(x1, x2, v, kf, kfn, hy_bias, tabs, cb, jb):
    z = v
    for f, gate in enumerate((x1, x2)):
        z = _conv_call(z, gate, hy_bias[f][None, :], kf, kfn, f, tabs, cb, jb)
    return z


def _filter_spectrum(L, w1, b1, fr, w2, b2, w3, tabs, ftabs, cb, jb):
    g_a, _, t_f, _ = tabs
    z, decay = ftabs
    pad_h = LANES - FILT_HID
    w1p = jnp.pad(w1, ((0, LANES - EMB_DIM), (0, pad_h)))
    w2p = jnp.pad(w2, ((0, pad_h), (0, pad_h)))
    w3p = jnp.pad(w3, ((0, pad_h), (0, 0)))
    pad1 = lambda a: jnp.pad(a, (0, pad_h))[None, :]
    hf = _filter_call(z, w1p, pad1(b1), pad1(fr), w2p, pad1(b2), w3p, decay, _tile(L, 512))
    s = _coarse_fwd_call(hf[None], g_a, cb)
    return _spec_call(s, t_f, cb, jb)


def _trunk(x, mix_norm, w_in, hy_conv_w, hy_conv_b, hf_w1, hf_b1, hf_freq, hf_w2, hf_b2, hf_w3, hy_bias,
           q_norm, w_uq, kv_norm, w_ukv, w_out, pool_w, pool_scale, mlp_norm, mlp_w1, mlp_w2, final_norm):
    B, L, _ = x.shape
    depth = mix_norm.shape[0]
    tm = _tile(L, 512)
    cb = 256
    jb = min(4, L // FFT_N2)
    tabs = _dft_tables(L)
    ftabs = _filter_tables(L)
    cos_t, sin_t = _rope_tables(L)
    row = lambda a: a[None, :]
    for i in range(depth):
        w1 = mlp_w1[i].astype(BF16)
        w2 = mlp_w2[i].astype(BF16)
        g = row(mlp_norm[i])
        if i % 2 == 0:
            e = i // 2
            why, wlat, wq, wk, wvt, vone = _even_weights(w_in[e], w_uq[e], w_ukv[e])
            x1, x2, v, q, k, vt = _in_call(
                x, row(mix_norm[i]), why, wlat, hy_conv_w[e], row(hy_conv_b[e]),
                row(q_norm[e]), wq, row(kv_norm[e]), wk, wvt, vone, cos_t, sin_t, tm)
            kf, kfn = _filter_spectrum(L, hf_w1[e], hf_b1[e], hf_freq[e], hf_w2[e], hf_b2[e], hf_w3[e],
                                       tabs, ftabs, cb, jb)
            y_hy = _hyena(x1, x2, v, kf, kfn, hy_bias[e], tabs, cb, min(8, L // FFT_N2))
            y_att = _attn_call(q, k, vt, _tile(L, 512))
            x = _out_call(x, y_hy, y_att, w_out[e].astype(BF16), g, w1, w2, tm)
        else:
            o = i // 2
            x = _odd_call(x, row(mix_norm[i]), pool_w[o].astype(BF16), row(pool_scale[o]), g, w1, w2,
                          row(final_norm), tm, final=(i == depth - 1))
    if depth % 2 == 1:
        raise NotImplementedError("final norm is fused into the last odd layer")
    return x


def kernel(x_prompt, x_sample, mix_norm, w_in, hy_conv_w, hy_conv_b, hf_w1, hf_b1, hf_freq, hf_w2, hf_b2,
           hf_w3, hy_bias, q_norm, w_uq, kv_norm, w_ukv, w_out, pool_w, pool_scale, mlp_norm, mlp_w1,
           mlp_w2, final_norm):
    weights = (mix_norm, w_in, hy_conv_w, hy_conv_b, hf_w1, hf_b1, hf_freq, hf_w2, hf_b2, hf_w3, hy_bias,
               q_norm, w_uq, kv_norm, w_ukv, w_out, pool_w, pool_scale, mlp_norm, mlp_w1, mlp_w2, final_norm)
    return (_trunk(x_prompt, *weights), _trunk(x_sample, *weights))
```

```python
import functools
import math

import jax
import jax.numpy as jnp
from jax import lax
from jax.experimental import pallas as pl
from jax.experimental.pallas import tpu as pltpu

F32 = jnp.float32
BF16 = jnp.bfloat16

D_MODEL = 1024
D_HY = 512
N_FILT = 2
HY_COLS = (N_FILT + 1) * D_HY
EMB_DIM = 33
N_BANDS = (EMB_DIM - 1) // 2
FILT_HID = 64
DECAY_FAST = 0.3
DECAY_SLOW = 1.5
DECAY_TARGET = 1e-2
N_HEADS = 8
NOPE_DIM = 64
ROPE_DIM = 32
V_DIM = 64
Q_LORA = 384
KV_LORA = 256
ROPE_THETA = 10000.0
POOL_WINDOWS = (2, 4, 8, 16)
POOL_GRP = D_MODEL // len(POOL_WINDOWS)
D_FF = 4 * D_MODEL
EPS = 1e-6
ATT_SCALE = (NOPE_DIM + ROPE_DIM) ** -0.5
Q_SCALE = ATT_SCALE * math.log2(math.e)
VT_ROWS = 80
HY_CB = 256
HY_RG = 2
MLP_PARTS = 2

LANES = 128
SUBLANES = 8
HALO = 16
FFT_N2 = 128
VMEM_LIMIT = 56 * 1024 * 1024


def _cparams(sem):
    return pltpu.CompilerParams(dimension_semantics=sem, vmem_limit_bytes=VMEM_LIMIT)


def _const_spec(shape):
    nd = len(shape)
    return pl.BlockSpec(shape, lambda *_: (0,) * nd, pipeline_mode=pl.Buffered(1))


def _rms(xf, g):
    ms = jnp.mean(xf * xf, axis=-1, keepdims=True)
    return xf * lax.rsqrt(ms + EPS) * g


def _bdot(a, b):
    return jnp.dot(a.astype(BF16), b.astype(BF16), preferred_element_type=F32)


def _in_kernel(xp_ref, xc_ref, xn_ref, g_ref, why_ref, wlat_ref, cw_ref, cb_ref,
               qn_ref, wq_ref, kvn_ref, wk_ref, wvt_ref, vone_ref, cos_ref, sin_ref,
               x1_ref, x2_ref, v_ref, q_ref, k_ref, vt_ref, *, tm, n_tiles):
    i = pl.program_id(1)
    g = g_ref[...]
    hc = _rms(xc_ref[0], g)
    hp = _rms(xp_ref[0], g) * jnp.where(i > 0, 1.0, 0.0)
    hn = _rms(xn_ref[0], g) * jnp.where(i < n_tiles - 1, 1.0, 0.0)
    hcb = hc.astype(BF16)
    h_ext = jnp.concatenate([hp.astype(BF16), hcb, hn.astype(BF16)], axis=0)
    p = jnp.dot(h_ext, why_ref[...], preferred_element_type=F32)
    n = tm + 2 * HALO
    cw = cw_ref[...]
    u = (pltpu.roll(p, 1, 0)[HALO:HALO + tm] * cw[0:1]
         + p[HALO:HALO + tm] * cw[1:2]
         + pltpu.roll(p, n - 1, 0)[HALO:HALO + tm] * cw[2:3]
         + cb_ref[...])
    for i_out, ref in enumerate((x1_ref, x2_ref, v_ref)):
        for c in range(D_HY // HY_CB):
            lo = i_out * D_HY + c * HY_CB
            ref[0, c] = u[:, lo:lo + HY_CB]

    lat = jnp.dot(hcb, wlat_ref[...], preferred_element_type=F32)
    cq = lat[:, :Q_LORA]
    ckv = lat[:, Q_LORA:Q_LORA + KV_LORA]
    kr = lat[:, Q_LORA + KV_LORA:Q_LORA + KV_LORA + LANES]
    krs = lat[:, Q_LORA + KV_LORA + LANES:]
    q2 = _bdot(_rms(cq, qn_ref[...]), wq_ref[...])
    ckvn = _rms(ckv, kvn_ref[...]).astype(BF16)
    k2 = jnp.dot(ckvn, wk_ref[...], preferred_element_type=F32)
    v_t = lax.dot_general(wvt_ref[...], ckvn, (((1,), (1,)), ((), ())),
                          preferred_element_type=F32) + vone_ref[...]
    hw = N_HEADS * LANES
    cos_t = cos_ref[...]
    sin_t = sin_ref[...]
    kr_r = kr * cos_t + krs * sin_t
    for h in range(N_HEADS):
        sl = slice(h * LANES, (h + 1) * LANES)
        qh = (q2[:, sl] * cos_t + q2[:, hw + h * LANES:hw + (h + 1) * LANES] * sin_t) * Q_SCALE
        q_ref[0, h] = qh.astype(BF16)
        k_ref[0, h] = (k2[:, sl] + kr_r).astype(BF16)
        vt_ref[0, h, 0] = v_t[h * VT_ROWS:(h + 1) * VT_ROWS, :].astype(BF16)


def _in_call(x, g, why, wlat, cw, cb, qn, wq, kvn, wk, wvt, vone, cos_t, sin_t, tm):
    B, L, _ = x.shape
    n_tiles = L // tm
    hb = tm // HALO
    n_hb = L // HALO
    tok = lambda b, i: (b, i, 0)
    in_specs = [
        pl.BlockSpec((1, HALO, D_MODEL), lambda b, i: (b, jnp.maximum(i * hb - 1, 0), 0)),
        pl.BlockSpec((1, tm, D_MODEL), tok),
        pl.BlockSpec((1, HALO, D_MODEL), lambda b, i: (b, jnp.minimum((i + 1) * hb, n_hb - 1), 0)),
        _const_spec(g.shape), _const_spec(why.shape), _const_spec(wlat.shape),
        _const_spec(cw.shape), _const_spec(cb.shape), _const_spec(qn.shape),
        _const_spec(wq.shape), _const_spec(kvn.shape), _const_spec(wk.shape),
        _const_spec(wvt.shape), _const_spec(vone.shape),
        pl.BlockSpec((tm, LANES), lambda b, i: (i, 0)),
        pl.BlockSpec((tm, LANES), lambda b, i: (i, 0)),
    ]
    hy = jax.ShapeDtypeStruct((B, D_HY // HY_CB, L, HY_CB), F32)
    hd = jax.ShapeDtypeStruct((B, N_HEADS, L, LANES), BF16)
    hy_spec = pl.BlockSpec((1, D_HY // HY_CB, tm, HY_CB), lambda b, i: (b, 0, i, 0))
    hd_spec = pl.BlockSpec((1, N_HEADS, tm, LANES), lambda b, i: (b, 0, i, 0))
    vt = jax.ShapeDtypeStruct((B, N_HEADS, n_tiles, VT_ROWS, tm), BF16)
    vt_spec = pl.BlockSpec((1, N_HEADS, 1, VT_ROWS, tm), lambda b, i: (b, 0, i, 0, 0))
    return pl.pallas_call(
        functools.partial(_in_kernel, tm=tm, n_tiles=n_tiles),
        grid=(B, n_tiles),
        in_specs=in_specs,
        out_specs=[hy_spec, hy_spec, hy_spec, hd_spec, hd_spec, vt_spec],
        out_shape=[hy, hy, hy, hd, hd, vt],
        compiler_params=_cparams(("parallel", "arbitrary")),
        name="even_in",
    )(x, x, x, g, why, wlat, cw, cb, qn, wq, kvn, wk, wvt, vone, cos_t, sin_t)


def _attn_kernel(q_ref, k_ref, vt_ref, o_ref, sa_ref, sb_ref, *, tq, tk, n_chunks):
    qs = [q_ref[0, hh] for hh in range(2)]

    def scores(hh, j, s_ref):
        start = pl.multiple_of(j * tk, tk)
        kc = k_ref[0, hh, pl.ds(start, tk), :]
        s = lax.dot_general(kc, qs[hh], (((1,), (1,)), ((), ())), preferred_element_type=F32)
        s_ref[hh] = s
        return jnp.max(s, axis=0, keepdims=True)

    def consume(hh, j, s_ref, cm, m, acc):
        m_new = jnp.maximum(m, cm)
        alpha = jnp.exp2(m - m_new)
        p = jnp.exp2(s_ref[hh] - m_new).astype(BF16)
        acc = acc * alpha + jnp.dot(vt_ref[0, hh, j], p, preferred_element_type=F32)
        return m_new, acc

    def step(j, src_ref, dst_ref, state, cms, fetch):
        new_state, new_cms = [], []
        for hh in range(2):
            if fetch:
                new_cms.append(scores(hh, j + 1, dst_ref))
            new_state += consume(hh, j, src_ref, cms[hh], state[2 * hh], state[2 * hh + 1])
        return new_state, new_cms

    def run_chunks(j, state, cms, last):
        for c in range(unroll):
            src, dst = (sa_ref, sb_ref) if c % 2 == 0 else (sb_ref, sa_ref)
            state, cms = step(j + c, src, dst, state, cms, not (last and c == unroll - 1))
        return state, cms

    m0 = jnp.full((1, tq), -1e30, F32)
    acc0 = jnp.zeros((VT_ROWS, tq), F32)
    state = [m0, acc0, m0, acc0]
    cms = [scores(hh, 0, sa_ref) for hh in range(2)]
    if n_chunks == 1:
        res, _ = step(0, sa_ref, sb_ref, state, cms, False)
    else:
        unroll = 4 if (n_chunks % 4 == 0 and n_chunks >= 16) else 2

        def body(i, carry):
            st, cm = run_chunks(unroll * i, list(carry[:4]), list(carry[4:]), last=False)
            return tuple(st) + tuple(cm)

        carry = lax.fori_loop(0, n_chunks // unroll - 1, body, tuple(state) + tuple(cms))
        res, _ = run_chunks(n_chunks - unroll, list(carry[:4]), list(carry[4:]), last=True)
    o_t = jnp.concatenate([res[1][:V_DIM] / res[1][V_DIM:V_DIM + 1],
                           res[3][:V_DIM] / res[3][V_DIM:V_DIM + 1]], axis=0)
    o_ref[0] = o_t.T


def _attn_call(q, k, vt, tq):
    B, _, L, _ = q.shape
    _, _, n_chunks, _, tk = vt.shape
    assert n_chunks == 1 or n_chunks % 2 == 0, n_chunks
    return pl.pallas_call(
        functools.partial(_attn_kernel, tq=tq, tk=tk, n_chunks=n_chunks),
        grid=(B, N_HEADS // 2, L // tq),
        in_specs=[pl.BlockSpec((1, 2, tq, LANES), lambda b, hp, i: (b, hp, i, 0)),
                  pl.BlockSpec((1, 2, L, LANES), lambda b, hp, i: (b, hp, 0, 0)),
                  pl.BlockSpec((1, 2, n_chunks, VT_ROWS, tk), lambda b, hp, i: (b, hp, 0, 0, 0))],
        out_specs=pl.BlockSpec((1, tq, LANES), lambda b, hp, i: (b, i, hp)),
        out_shape=jax.ShapeDtypeStruct((B, L, N_HEADS * V_DIM), F32),
        scratch_shapes=[pltpu.VMEM((2, tk, tq), F32), pltpu.VMEM((2, tk, tq), F32)],
        compiler_params=_cparams(("parallel", "parallel", "arbitrary")),
        name="attention",
    )(q, k, vt)


def _row_parts(tm):
    part = tm // MLP_PARTS if tm % (MLP_PARTS * HALO) == 0 else tm
    return [slice(r, r + part) for r in range(0, tm, part)]


def _mlp(x, g_ref, w1_ref, w2_ref):
    h = _rms(x, g_ref[...]).astype(BF16)
    acc = x
    chunk = D_MODEL
    for c in range(D_FF // chunk):
        hid = jnp.dot(h, w1_ref[:, c * chunk:(c + 1) * chunk], preferred_element_type=F32)
        hid = jnp.square(jnp.maximum(hid, 0.0)).astype(BF16)
        acc = acc + jnp.dot(hid, w2_ref[c * chunk:(c + 1) * chunk, :], preferred_element_type=F32)
    return acc


def _out_kernel(x_ref, yh_ref, ya_ref, wo_ref, g_ref, w1_ref, w2_ref, o_ref):
    tm = x_ref.shape[1]
    for rows in _row_parts(tm):
        y = jnp.concatenate([yh_ref[0, c, rows, :] for c in range(D_HY // HY_CB)] + [ya_ref[0, rows, :]],
                            axis=1).astype(BF16)
        x = x_ref[0, rows, :] + jnp.dot(y, wo_ref[...], preferred_element_type=F32)
        o_ref[0, rows, :] = _mlp(x, g_ref, w1_ref, w2_ref)


def _out_call(x, yh, ya, wo, g, w1, w2, tm):
    B, L, _ = x.shape
    tok = lambda b, i: (b, i, 0)
    return pl.pallas_call(
        _out_kernel,
        grid=(B, L // tm),
        in_specs=[pl.BlockSpec((1, tm, D_MODEL), tok),
                  pl.BlockSpec((1, D_HY // HY_CB, tm, HY_CB), lambda b, i: (b, 0, i, 0)),
                  pl.BlockSpec((1, tm, D_HY), tok), _const_spec(wo.shape), _const_spec(g.shape),
                  _const_spec(w1.shape), _const_spec(w2.shape)],
        out_specs=pl.BlockSpec((1, tm, D_MODEL), tok),
        out_shape=jax.ShapeDtypeStruct(x.shape, F32),
        compiler_params=_cparams(("parallel", "arbitrary")),
        name="even_out_mlp",
    )(x, yh, ya, wo, g, w1, w2)


def _odd_kernel(xp_ref, xc_ref, xn_ref, gm_ref, pw_ref, ps_ref, g_ref, w1_ref, w2_ref, fn_ref,
                o_ref, *, tm, n_tiles, seq, final):
    i = pl.program_id(1)
    gm = gm_ref[...]
    x = xc_ref[0]
    hp = _rms(xp_ref[0], gm) * jnp.where(i > 0, 1.0, 0.0)
    hn = _rms(xn_ref[0], gm) * jnp.where(i < n_tiles - 1, 1.0, 0.0)
    hc = _rms(x, gm)
    e = jnp.concatenate([hp, hc, hn], axis=0)
    ps = ps_ref[...]
    for rows in _row_parts(tm):
        part = rows.stop - rows.start
        n = part + 2 * HALO
        ep = e[rows.start:rows.start + n]
        t = i * tm + rows.start + lax.broadcasted_iota(jnp.int32, (part, 1), 0)
        mixed = []
        for gi, w in enumerate(POOL_WINDOWS):
            sl = slice(gi * POOL_GRP, (gi + 1) * POOL_GRP)
            eg = ep[:, sl]
            s = pltpu.roll(eg, 1, 0) + eg
            half = 1
            while 2 * half < w:
                s = pltpu.roll(s, half, 0) + pltpu.roll(s, n - half, 0)
                half *= 2
            cnt = (jnp.minimum(t + w // 2, seq) - jnp.maximum(t - w // 2, 0)).astype(F32)
            d = s[HALO:HALO + part] / cnt - eg[HALO:HALO + part]
            mixed.append(_bdot(d, pw_ref[gi]) * ps[:, sl])
        y = _mlp(x[rows] + jnp.concatenate(mixed, axis=1), g_ref, w1_ref, w2_ref)
        if final:
            y = _rms(y, fn_ref[...])
        o_ref[0, rows, :] = y


def _odd_call(x, gm, pw, ps, g, w1, w2, fn, tm, final):
    B, L, _ = x.shape
    n_tiles = L // tm
    hb = tm // HALO
    n_hb = L // HALO
    tok = lambda b, i: (b, i, 0)
    return pl.pallas_call(
        functools.partial(_odd_kernel, tm=tm, n_tiles=n_tiles, seq=L, final=final),
        grid=(B, n_tiles),
        in_specs=[
            pl.BlockSpec((1, HALO, D_MODEL), lambda b, i: (b, jnp.maximum(i * hb - 1, 0), 0)),
            pl.BlockSpec((1, tm, D_MODEL), tok),
            pl.BlockSpec((1, HALO, D_MODEL), lambda b, i: (b, jnp.minimum((i + 1) * hb, n_hb - 1), 0)),
            _const_spec(gm.shape), _const_spec(pw.shape), _const_spec(ps.shape), _const_spec(g.shape),
            _const_spec(w1.shape), _const_spec(w2.shape), _const_spec(fn.shape)],
        out_specs=pl.BlockSpec((1, tm, D_MODEL), tok),
        out_shape=jax.ShapeDtypeStruct(x.shape, F32),
        compiler_params=_cparams(("parallel", "arbitrary")),
        name="odd_pool_mlp",
    )(x, x, x, gm, pw, ps, g, w1, w2, fn)


def _coarse_fwd_kernel(x_ref, g_ref, o_ref):
    x = x_ref[0, :, 0]
    a, _, cb = x.shape
    xb = x.reshape(a * SUBLANES, cb).astype(BF16)
    y = jnp.dot(g_ref[...], xb, preferred_element_type=F32)
    o_ref[0, :, 0] = y.reshape(y.shape[0] // SUBLANES, SUBLANES, cb)


def _coarse_fwd_call(x, g_a, cb):
    B, L, C = x.shape
    a = L // FFT_N2
    n1 = 2 * a
    nb = FFT_N2 // SUBLANES
    x5 = x.reshape(B, a, nb, SUBLANES, C)
    out = pl.pallas_call(
        _coarse_fwd_kernel,
        grid=(B, nb, C // cb),
        in_specs=[pl.BlockSpec((1, a, 1, SUBLANES, cb), lambda b, r, c: (b, 0, r, 0, c)),
                  _const_spec(g_a.shape)],
        out_specs=pl.BlockSpec((1, n1, 1, SUBLANES, cb), lambda b, r, c: (b, 0, r, 0, c)),
        out_shape=jax.ShapeDtypeStruct((B, n1, nb, SUBLANES, C), F32),
        compiler_params=_cparams(("parallel", "parallel", "arbitrary")),
        name="hyena_coarse_fwd",
    )(x5, g_a)
    return out.reshape(B, n1, FFT_N2, C)


def _cmul(x, kr, ki):
    xr, xi = x[:FFT_N2], x[FFT_N2:]
    return jnp.concatenate([xr * kr - xi * ki, xr * ki + xi * kr], axis=0)


def _conv_kernel(xf_ref, ga_ref, t_ref, ta_ref, kf_ref, kfn_ref, gb_ref, u_ref, gate_ref, bias_ref,
                 o_ref, s_ref, *, nb, nj, jb):
    t = pl.program_id(2)
    n1, _, cb = s_ref.shape

    def rows(step, g):
        return pl.ds(pl.multiple_of((step * HY_RG + g) * SUBLANES, SUBLANES), SUBLANES)

    @pl.when(t < nb)
    def _():
        for g in range(HY_RG):
            x = xf_ref[0, 0, :, 0, g * SUBLANES:(g + 1) * SUBLANES, :]
            xb = x.reshape(x.shape[0] * SUBLANES, cb).astype(BF16)
            y = jnp.dot(ga_ref[...], xb, preferred_element_type=F32)
            s_ref[:, rows(t, g), :] = y.reshape(n1, SUBLANES, cb)

    def pair(jj):
        return (t - nb) * jb + jj

    def load_pair(jj):
        return s_ref[pl.ds(2 * pair(jj), 2)].reshape(2 * FFT_N2, cb)

    def store_pair(jj, z):
        s_ref[pl.ds(2 * pair(jj), 2)] = z.reshape(2, FFT_N2, cb)

    def fine(jjs):
        xs = [_bdot(t_ref[pair(jj)], load_pair(jj)) for jj in jjs]
        ys = [_cmul(x, kf_ref[0, pair(jj)], kf_ref[1, pair(jj)]) for jj, x in zip(jjs, xs)]
        for jj, y in zip(jjs, ys):
            store_pair(jj, _bdot(ta_ref[pair(jj)], y))

    def fine_first_pair():
        s = load_pair(0)
        lo = lax.broadcasted_iota(jnp.int32, (2 * FFT_N2, 1), 0) < FFT_N2
        xa = _bdot(t_ref[0], jnp.where(lo, s, 0.0))
        xb = _bdot(t_ref[0], jnp.where(lo, 0.0, s))
        ya = _cmul(xa, kf_ref[0, 0], kf_ref[1, 0])
        yb = _cmul(xb, kfn_ref[0, 0], kfn_ref[1, 0])
        store_pair(0, jnp.where(lo, _bdot(ta_ref[0], ya), _bdot(ta_ref[0], yb)))

    @pl.when(t == nb)
    def _():
        fine_first_pair()
        fine(list(range(1, jb)))

    @pl.when((t > nb) & (t < nb + nj))
    def _():
        fine(list(range(jb)))

    @pl.when(t >= nb + nj)
    def _():
        for g in range(HY_RG):
            sub = slice(g * SUBLANES, (g + 1) * SUBLANES)
            z = s_ref[:, rows(t - nb - nj, g), :]
            y = jnp.dot(gb_ref[...], z.reshape(n1 * SUBLANES, cb).astype(BF16), preferred_element_type=F32)
            y = y.reshape(y.shape[0] // SUBLANES, SUBLANES, cb)
            o_ref[0, 0, :, 0, sub, :] = ((y + u_ref[0, 0, :, 0, sub, :] * bias_ref[...])
                                         * gate_ref[0, 0, :, 0, sub, :])


def _conv_call(u, gate, bias, kf, kfn, filt, tabs, jb):
    g_a, g_b, t_f, t_i = tabs
    B, nc, L, cb = u.shape
    a = L // FFT_N2
    n1 = 2 * a
    rg_rows = HY_RG * SUBLANES
    nb = FFT_N2 // rg_rows
    nj = (n1 // 2) // jb
    koff = filt * nc
    u5 = u.reshape(B, nc, a, nb, rg_rows, cb)
    fwd_r = lambda c, b, t: (b, c, 0, jnp.minimum(t, nb - 1), 0, 0)
    inv_r = lambda c, b, t: (b, c, 0, jnp.clip(t - nb - nj, 0, nb - 1), 0, 0)
    tok = (1, 1, a, 1, rg_rows, cb)
    kspec = lambda n: pl.BlockSpec((2, n, FFT_N2, cb), lambda c, b, t: (0, 0, 0, koff + c),
                                   pipeline_mode=pl.Buffered(1))
    out = pl.pallas_call(
        functools.partial(_conv_kernel, nb=nb, nj=nj, jb=jb),
        grid=(nc, B, 2 * nb + nj),
        in_specs=[pl.BlockSpec(tok, fwd_r), _const_spec(g_a.shape), _const_spec(t_f.shape),
                  _const_spec(t_i.shape), kspec(n1 // 2), kspec(1),
                  _const_spec(g_b.shape), pl.BlockSpec(tok, inv_r), pl.BlockSpec(tok, inv_r),
                  pl.BlockSpec((1, cb), lambda c, b, t: (0, c))],
        out_specs=pl.BlockSpec(tok, inv_r),
        out_shape=jax.ShapeDtypeStruct(u5.shape, F32),
        scratch_shapes=[pltpu.VMEM((n1, FFT_N2, cb), F32)],
        compiler_params=_cparams(("arbitrary", "arbitrary", "arbitrary")),
        name="hyena_conv",
    )(u5, g_a, t_f, t_i, kf, kfn, g_b, u5, gate.reshape(u5.shape), bias)
    return out.reshape(u.shape)


def _filter_kernel(z_ref, w1_ref, b1_ref, fr_ref, w2_ref, b2_ref, w3_ref, dec_ref, o_ref, *, tl):
    hi = lax.Precision.HIGHEST
    fr = fr_ref[...]
    h = jnp.sin(fr * (jnp.dot(z_ref[...], w1_ref[...], precision=hi, preferred_element_type=F32) + b1_ref[...]))
    h = jnp.sin(fr * (jnp.dot(h, w2_ref[...], precision=hi, preferred_element_type=F32) + b2_ref[...]))
    h = jnp.dot(h, w3_ref[...], precision=hi, preferred_element_type=F32)
    dec = dec_ref[...]
    h = h * jnp.concatenate([dec] * (2 * N_FILT), axis=1)
    row = pl.program_id(0) * tl + lax.broadcasted_iota(jnp.int32, h.shape, 0)
    lane = lax.broadcasted_iota(jnp.int32, h.shape, 1)
    o_ref[...] = jnp.where((row == 0) & (lane >= N_FILT * D_HY), 0.0, h)


def _filter_call(z, w1, b1, fr, w2, b2, w3, dec, tl):
    L = z.shape[0]
    return pl.pallas_call(
        functools.partial(_filter_kernel, tl=tl),
        grid=(L // tl,),
        in_specs=[pl.BlockSpec((tl, LANES), lambda i: (i, 0)),
                  _const_spec(w1.shape), _const_spec(b1.shape), _const_spec(fr.shape),
                  _const_spec(w2.shape), _const_spec(b2.shape), _const_spec(w3.shape),
                  pl.BlockSpec((tl, D_HY), lambda i: (i, 0))],
        out_specs=pl.BlockSpec((tl, 2 * N_FILT * D_HY), lambda i: (i, 0)),
        out_shape=jax.ShapeDtypeStruct((L, 2 * N_FILT * D_HY), F32),
        compiler_params=_cparams(("arbitrary",)),
        name="hyena_filter_mlp",
    )(z, w1, b1, fr, w2, b2, w3, dec)


def _spec_kernel(sf_ref, sb_ref, t_ref, kf_ref, kfn_ref, *, jb):
    step = pl.program_id(1)
    cb = sf_ref.shape[-1]

    def combine(xf, xb):
        return jnp.concatenate([xf[:FFT_N2] + xb[:FFT_N2], xf[FFT_N2:] - xb[FFT_N2:]], axis=0)

    def generic(jj):
        xf = _bdot(t_ref[jj], sf_ref[0, 2 * jj:2 * jj + 2].reshape(2 * FFT_N2, cb))
        xb = _bdot(t_ref[jj], sb_ref[0, 2 * jj:2 * jj + 2].reshape(2 * FFT_N2, cb))
        kf_ref[:, jj] = combine(xf, xb).reshape(2, FFT_N2, cb).astype(kf_ref.dtype)

    def special():
        row = lax.broadcasted_iota(jnp.int32, (2 * FFT_N2, 1), 0)
        lo = row < FFT_N2
        sf = sf_ref[0, 0:2].reshape(2 * FFT_N2, cb)
        sb = sb_ref[0, 0:2].reshape(2 * FFT_N2, cb)
        k0 = combine(_bdot(t_ref[0], jnp.where(lo, sf, 0.0)), _bdot(t_ref[0], jnp.where(lo, sb, 0.0)))
        kn = combine(_bdot(t_ref[0], jnp.where(lo, 0.0, sf)), _bdot(t_ref[0], jnp.where(lo, 0.0, sb)))
        kf_ref[:, 0] = k0.reshape(2, FFT_N2, cb).astype(kf_ref.dtype)
        kfn_ref[:, 0] = kn.reshape(2, FFT_N2, cb).astype(kfn_ref.dtype)

    pl.when(step == 0)(special)
    pl.when(step != 0)(lambda: generic(0))
    for jj in range(1, jb):
        generic(jj)


def _spec_call(s, t_f, cb, jb):
    _, n1, _, c2 = s.shape
    npair = n1 // 2
    C = c2 // 2
    nc = C // cb
    return pl.pallas_call(
        functools.partial(_spec_kernel, jb=jb),
        grid=(nc, npair // jb),
        in_specs=[pl.BlockSpec((1, 2 * jb, FFT_N2, cb), lambda c, j: (0, j, 0, c)),
                  pl.BlockSpec((1, 2 * jb, FFT_N2, cb), lambda c, j: (0, j, 0, nc + c)),
                  pl.BlockSpec((jb, 2 * FFT_N2, 2 * FFT_N2), lambda c, j: (j, 0, 0))],
        out_specs=[pl.BlockSpec((2, jb, FFT_N2, cb), lambda c, j: (0, j, 0, c)),
                   pl.BlockSpec((2, 1, FFT_N2, cb), lambda c, j: (0, 0, 0, c))],
        out_shape=[jax.ShapeDtypeStruct((2, npair, FFT_N2, C), BF16),
                   jax.ShapeDtypeStruct((2, 1, FFT_N2, C), BF16)],
        compiler_params=_cparams(("arbitrary", "arbitrary")),
        name="hyena_filter_spectrum",
    )(s, s, t_f)


def _dft_tables(L):
    a_n = L // FFT_N2
    n1 = 2 * a_n
    n = 2 * L
    npair = n1 // 2
    two_pi = 2.0 * math.pi

    a = jnp.arange(a_n, dtype=jnp.int32)[None, :]
    slot = jnp.arange(n1, dtype=jnp.int32)[:, None]
    j = slot // 2
    ang = two_pi * ((a * j) % n1).astype(F32) / n1
    sign = jnp.where(a % 2 == 0, 1.0, -1.0)
    is_im = (slot % 2) == 1
    f1 = jnp.where(is_im, -jnp.sin(ang), jnp.cos(ang))
    f1 = jnp.where(slot == 0, 1.0, jnp.where(slot == 1, sign, f1))
    fb = jnp.where(is_im, -2.0 * jnp.sin(ang), 2.0 * jnp.cos(ang))
    fb = jnp.where(slot == 0, 1.0, jnp.where(slot == 1, sign, fb)) / n
    eye = jnp.eye(SUBLANES, dtype=F32)
    g_a = jnp.kron(f1, eye).astype(BF16)
    g_b = jnp.kron(fb.T, eye).astype(BF16)

    b = jnp.arange(FFT_N2, dtype=jnp.int32)[None, None, :]
    d = jnp.arange(FFT_N2, dtype=jnp.int32)[None, :, None]
    c = jnp.arange(npair + 1, dtype=jnp.int32)[:, None, None]
    phi = two_pi * ((b * (d * n1 + c)) % n).astype(F32) / n
    co, si = jnp.cos(phi), jnp.sin(phi)
    t_f = jnp.concatenate([jnp.concatenate([co, si], axis=2),
                           jnp.concatenate([-si, co], axis=2)], axis=1)[:npair]
    t0 = jnp.concatenate([jnp.concatenate([co[0], co[npair]], axis=1),
                          jnp.concatenate([-si[0], -si[npair]], axis=1)], axis=0)
    t_f = t_f.at[0].set(t0)
    t_i = jnp.swapaxes(t_f, 1, 2)
    return g_a, g_b, t_f.astype(BF16), t_i.astype(BF16)


def _filter_tables(L):
    t = jnp.linspace(0.0, 1.0, L, dtype=F32)[:, None]
    wpos = (2.0 * math.pi / L) * jnp.arange(L, dtype=F32)[:, None]
    bands = jnp.linspace(1e-4, N_BANDS - 1, N_BANDS, dtype=F32)[None, :]
    z = jnp.concatenate([t, jnp.cos(bands * wpos), -jnp.sin(bands * wpos)], axis=-1)
    z = jnp.pad(z, ((0, 0), (0, LANES - EMB_DIM)))
    max_decay = math.log(DECAY_TARGET) / DECAY_FAST
    min_decay = math.log(DECAY_TARGET) / DECAY_SLOW
    deltas = jnp.linspace(min_decay, max_decay, D_HY, dtype=F32)
    decay = jnp.exp(-t * jnp.abs(deltas))
    return z, decay


def _rope_tables(L):
    inv_freq = 1.0 / (ROPE_THETA ** (jnp.arange(0, ROPE_DIM, 2, dtype=F32) / ROPE_DIM))
    ang = jnp.arange(L, dtype=F32)[:, None] * inv_freq[None, :]
    cos, sin = jnp.cos(ang), jnp.sin(ang)
    one = jnp.ones((L, NOPE_DIM), F32)
    zero_n = jnp.zeros((L, NOPE_DIM), F32)
    zero_t = jnp.zeros((L, LANES - NOPE_DIM - ROPE_DIM), F32)
    cos_t = jnp.concatenate([one, cos, cos, zero_t], axis=1)
    sin_t = jnp.concatenate([zero_n, -sin, sin, zero_t], axis=1)
    return cos_t, sin_t


def _swap_halves(w):
    half = w.shape[-1] // 2
    return jnp.concatenate([w[..., half:], w[..., :half]], axis=-1)


def _place(w, width, offset):
    return jnp.pad(w, ((0, 0), (offset, width - offset - w.shape[1])))


def _even_weights(w_in, w_uq, w_ukv):
    why = w_in[:, :HY_COLS].astype(BF16)
    w_kr = w_in[:, HY_COLS + Q_LORA + KV_LORA:]
    wlat = jnp.concatenate([
        w_in[:, HY_COLS:HY_COLS + Q_LORA + KV_LORA],
        _place(w_kr, LANES, NOPE_DIM),
        _place(_swap_halves(w_kr), LANES, NOPE_DIM)], axis=1).astype(BF16)
    qh = w_uq.reshape(Q_LORA, N_HEADS, NOPE_DIM + ROPE_DIM)
    q_main = jnp.pad(qh, ((0, 0), (0, 0), (0, LANES - NOPE_DIM - ROPE_DIM)))
    q_sw = jnp.pad(_swap_halves(qh[..., NOPE_DIM:]),
                   ((0, 0), (0, 0), (NOPE_DIM, LANES - NOPE_DIM - ROPE_DIM)))
    wq = jnp.concatenate([q_main.reshape(Q_LORA, -1), q_sw.reshape(Q_LORA, -1)], axis=1).astype(BF16)
    kvh = w_ukv.reshape(KV_LORA, N_HEADS, NOPE_DIM + V_DIM)
    k_main = jnp.pad(kvh[..., :NOPE_DIM], ((0, 0), (0, 0), (0, LANES - NOPE_DIM)))
    wk = k_main.reshape(KV_LORA, -1).astype(BF16)
    v_rows = jnp.pad(kvh[..., NOPE_DIM:], ((0, 0), (0, 0), (0, VT_ROWS - V_DIM)))
    wvt = v_rows.reshape(KV_LORA, -1).T.astype(BF16)
    vone = (jnp.arange(N_HEADS * VT_ROWS) % VT_ROWS == V_DIM).astype(F32)[:, None]
    return why, wlat, wq, wk, wvt, vone


def _tile(L, want):
    return min(want, L)


def _hyena(x1, x2, v, kf, kfn, hy_bias, tabs, jb):
    z = v
    for f, gate in enumerate((x1, x2)):
        z = _conv_call(z, gate, hy_bias[f][None, :], kf, kfn, f, tabs, jb)
    return z


def _filter_spectrum(L, w1, b1, fr, w2, b2, w3, tabs, ftabs, cb, jb):
    g_a, _, t_f, _ = tabs
    z, decay = ftabs
    pad_h = LANES - FILT_HID
    w1p = jnp.pad(w1, ((0, LANES - EMB_DIM), (0, pad_h)))
    w2p = jnp.pad(w2, ((0, pad_h), (0, pad_h)))
    w3p = jnp.pad(w3, ((0, pad_h), (0, 0)))
    pad1 = lambda a: jnp.pad(a, (0, pad_h))[None, :]
    hf = _filter_call(z, w1p, pad1(b1), pad1(fr), w2p, pad1(b2), w3p, decay, _tile(L, 512))
    s = _coarse_fwd_call(hf[None], g_a, cb)
    return _spec_call(s, t_f, cb, jb)


def _trunk(x, mix_norm, w_in, hy_conv_w, hy_conv_b, hf_w1, hf_b1, hf_freq, hf_w2, hf_b2, hf_w3, hy_bias,
           q_norm, w_uq, kv_norm, w_ukv, w_out, pool_w, pool_scale, mlp_norm, mlp_w1, mlp_w2, final_norm):
    B, L, _ = x.shape
    depth = mix_norm.shape[0]
    tm = _tile(L, 512)
    cb = 256
    jb = min(4, L // FFT_N2)
    tabs = _dft_tables(L)
    ftabs = _filter_tables(L)
    cos_t, sin_t = _rope_tables(L)
    row = lambda a: a[None, :]
    for i in range(depth):
        w1 = mlp_w1[i].astype(BF16)
        w2 = mlp_w2[i].astype(BF16)
        g = row(mlp_norm[i])
        if i % 2 == 0:
            e = i // 2
            why, wlat, wq, wk, wvt, vone = _even_weights(w_in[e], w_uq[e], w_ukv[e])
            x1, x2, v, q, k, vt = _in_call(
                x, row(mix_norm[i]), why, wlat, hy_conv_w[e], row(hy_conv_b[e]),
                row(q_norm[e]), wq, row(kv_norm[e]), wk, wvt, vone, cos_t, sin_t, tm)
            kf, kfn = _filter_spectrum(L, hf_w1[e], hf_b1[e], hf_freq[e], hf_w2[e], hf_b2[e], hf_w3[e],
                                       tabs, ftabs, cb, jb)
            y_hy = _hyena(x1, x2, v, kf, kfn, hy_bias[e], tabs, min(8, L // FFT_N2))
            y_att = _attn_call(q, k, vt, _tile(L, 512))
            x = _out_call(x, y_hy, y_att, w_out[e].astype(BF16), g, w1, w2, tm)
        else:
            o = i // 2
            x = _odd_call(x, row(mix_norm[i]), pool_w[o].astype(BF16), row(pool_scale[o]), g, w1, w2,
                          row(final_norm), tm, final=(i == depth - 1))
    if depth % 2 == 1:
        raise NotImplementedError("final norm is fused into the last odd layer")
    return x


def kernel(x_prompt, x_sample, mix_norm, w_in, hy_conv_w, hy_conv_b, hf_w1, hf_b1, hf_freq, hf_w2, hf_b2,
           hf_w3, hy_bias, q_norm, w_uq, kv_norm, w_ukv, w_out, pool_w, pool_scale, mlp_norm, mlp_w1,
           mlp_w2, final_norm):
    weights = (mix_norm, w_in, hy_conv_w, hy_conv_b, hf_w1, hf_b1, hf_freq, hf_w2, hf_b2, hf_w3, hy_bias,
               q_norm, w_uq, kv_norm, w_ukv, w_out, pool_w, pool_scale, mlp_norm, mlp_w1, mlp_w2, final_norm)
    return (_trunk(x_prompt, *weights), _trunk(x_sample, *weights))
```

```python
import functools
import math

import jax
import jax.numpy as jnp
from jax import lax
from jax.experimental import pallas as pl
from jax.experimental.pallas import tpu as pltpu

F32 = jnp.float32
BF16 = jnp.bfloat16

D_MODEL = 1024
D_HY = 512
N_FILT = 2
HY_COLS = (N_FILT + 1) * D_HY
EMB_DIM = 33
N_BANDS = (EMB_DIM - 1) // 2
FILT_HID = 64
DECAY_FAST = 0.3
DECAY_SLOW = 1.5
DECAY_TARGET = 1e-2
N_HEADS = 8
NOPE_DIM = 64
ROPE_DIM = 32
V_DIM = 64
Q_LORA = 384
KV_LORA = 256
ROPE_THETA = 10000.0
POOL_WINDOWS = (2, 4, 8, 16)
POOL_GRP = D_MODEL // len(POOL_WINDOWS)
D_FF = 4 * D_MODEL
EPS = 1e-6
ATT_SCALE = (NOPE_DIM + ROPE_DIM) ** -0.5
Q_SCALE = ATT_SCALE * math.log2(math.e)
VT_ROWS = 80
HY_CB = 256
HY_RG = 2
MLP_PARTS = 1
ATT_SPLIT = 2

LANES = 128
SUBLANES = 8
HALO = 16
FFT_N2 = 128
VMEM_LIMIT = 56 * 1024 * 1024


def _cparams(sem):
    return pltpu.CompilerParams(dimension_semantics=sem, vmem_limit_bytes=VMEM_LIMIT)


def _const_spec(shape):
    nd = len(shape)
    return pl.BlockSpec(shape, lambda *_: (0,) * nd, pipeline_mode=pl.Buffered(1))


def _rms(xf, g):
    ms = jnp.mean(xf * xf, axis=-1, keepdims=True)
    return xf * lax.rsqrt(ms + EPS) * g


def _bdot(a, b):
    return jnp.dot(a.astype(BF16), b.astype(BF16), preferred_element_type=F32)


def _in_kernel(xp_ref, xc_ref, xn_ref, g_ref, why_ref, wlat_ref, cw_ref, cb_ref,
               qn_ref, wq_ref, kvn_ref, wk_ref, wvt_ref, vone_ref, cos_ref, sin_ref,
               x1_ref, x2_ref, v_ref, q_ref, k_ref, vt_ref, *, tm, n_tiles):
    i = pl.program_id(1)
    g = g_ref[...]
    hc = _rms(xc_ref[0], g)
    hp = _rms(xp_ref[0], g) * jnp.where(i > 0, 1.0, 0.0)
    hn = _rms(xn_ref[0], g) * jnp.where(i < n_tiles - 1, 1.0, 0.0)
    hcb = hc.astype(BF16)
    h_ext = jnp.concatenate([hp.astype(BF16), hcb, hn.astype(BF16)], axis=0)
    p = jnp.dot(h_ext, why_ref[...], preferred_element_type=F32)
    n = tm + 2 * HALO
    cw = cw_ref[...]
    u = (pltpu.roll(p, 1, 0)[HALO:HALO + tm] * cw[0:1]
         + p[HALO:HALO + tm] * cw[1:2]
         + pltpu.roll(p, n - 1, 0)[HALO:HALO + tm] * cw[2:3]
         + cb_ref[...])
    for i_out, ref in enumerate((x1_ref, x2_ref, v_ref)):
        for c in range(D_HY // HY_CB):
            lo = i_out * D_HY + c * HY_CB
            ref[0, c] = u[:, lo:lo + HY_CB].astype(ref.dtype)

    lat = jnp.dot(hcb, wlat_ref[...], preferred_element_type=F32)
    cq = lat[:, :Q_LORA]
    ckv = lat[:, Q_LORA:Q_LORA + KV_LORA]
    kr = lat[:, Q_LORA + KV_LORA:Q_LORA + KV_LORA + LANES]
    krs = lat[:, Q_LORA + KV_LORA + LANES:]
    q2 = _bdot(_rms(cq, qn_ref[...]), wq_ref[...])
    ckvn = _rms(ckv, kvn_ref[...]).astype(BF16)
    k2 = jnp.dot(ckvn, wk_ref[...], preferred_element_type=F32)
    v_t = lax.dot_general(wvt_ref[...], ckvn, (((1,), (1,)), ((), ())),
                          preferred_element_type=F32) + vone_ref[...]
    hw = N_HEADS * LANES
    cos_t = cos_ref[...]
    sin_t = sin_ref[...]
    kr_r = kr * cos_t + krs * sin_t
    for h in range(N_HEADS):
        sl = slice(h * LANES, (h + 1) * LANES)
        qh = (q2[:, sl] * cos_t + q2[:, hw + h * LANES:hw + (h + 1) * LANES] * sin_t) * Q_SCALE
        q_ref[0, h] = qh.astype(BF16)
        k_ref[0, h] = (k2[:, sl] + kr_r).astype(BF16)
        vt_ref[0, h, 0] = v_t[h * VT_ROWS:(h + 1) * VT_ROWS, :].astype(BF16)


def _in_call(x, g, why, wlat, cw, cb, qn, wq, kvn, wk, wvt, vone, rope_tabs, tm):
    B, L, _ = x.shape
    n_tiles = L // tm
    hb = tm // HALO
    n_hb = L // HALO
    tok = lambda b, i: (b, i, 0)
    in_specs = [
        pl.BlockSpec((1, HALO, D_MODEL), lambda b, i: (b, jnp.maximum(i * hb - 1, 0), 0)),
        pl.BlockSpec((1, tm, D_MODEL), tok),
        pl.BlockSpec((1, HALO, D_MODEL), lambda b, i: (b, jnp.minimum((i + 1) * hb, n_hb - 1), 0)),
        _const_spec(g.shape), _const_spec(why.shape), _const_spec(wlat.shape),
        _const_spec(cw.shape), _const_spec(cb.shape), _const_spec(qn.shape),
        _const_spec(wq.shape), _const_spec(kvn.shape), _const_spec(wk.shape),
        _const_spec(wvt.shape), _const_spec(vone.shape),
    ] + [pl.BlockSpec((tm, LANES), lambda b, i: (i, 0)) for _ in rope_tabs]
    hy = jax.ShapeDtypeStruct((B, D_HY // HY_CB, L, HY_CB), BF16)
    hd = jax.ShapeDtypeStruct((B, N_HEADS, L, LANES), BF16)
    hy_spec = pl.BlockSpec((1, D_HY // HY_CB, tm, HY_CB), lambda b, i: (b, 0, i, 0))
    hd_spec = pl.BlockSpec((1, N_HEADS, tm, LANES), lambda b, i: (b, 0, i, 0))
    vt = jax.ShapeDtypeStruct((B, N_HEADS, n_tiles, VT_ROWS, tm), BF16)
    vt_spec = pl.BlockSpec((1, N_HEADS, 1, VT_ROWS, tm), lambda b, i: (b, 0, i, 0, 0))
    return pl.pallas_call(
        functools.partial(_in_kernel, tm=tm, n_tiles=n_tiles),
        grid=(B, n_tiles),
        in_specs=in_specs,
        out_specs=[hy_spec, hy_spec, hy_spec, hd_spec, hd_spec, vt_spec],
        out_shape=[hy, hy, hy, hd, hd, vt],
        compiler_params=_cparams(("parallel", "arbitrary")),
        name="even_in",
    )(x, x, x, g, why, wlat, cw, cb, qn, wq, kvn, wk, wvt, vone, *rope_tabs)


def _attn_kernel(q_ref, k_ref, vt_ref, o_ref, sa_ref, sb_ref, *, tq, tk, n_chunks):
    qs = [q_ref[0, hh] for hh in range(2)]

    th = tk // ATT_SPLIT

    def scores(hh, j, s_ref, r):
        start = pl.multiple_of(j * tk + r * th, th)
        kc = k_ref[0, hh, pl.ds(start, th), :]
        s = lax.dot_general(kc, qs[hh], (((1,), (1,)), ((), ())), preferred_element_type=F32)
        s_ref[hh, r * th:(r + 1) * th, :] = s
        return jnp.max(s, axis=0, keepdims=True)

    def step(j, src_ref, dst_ref, state, cms, fetch):
        new_state, new_cms = [], []
        for hh in range(2):
            m, acc = state[2 * hh], state[2 * hh + 1]
            m_new = jnp.maximum(m, cms[hh])
            acc = acc * jnp.exp2(m - m_new)
            cm_next = None
            for r in range(ATT_SPLIT):
                if fetch:
                    cm_r = scores(hh, j + 1, dst_ref, r)
                    cm_next = cm_r if cm_next is None else jnp.maximum(cm_next, cm_r)
                p = jnp.exp2(src_ref[hh, r * th:(r + 1) * th, :] - m_new).astype(BF16)
                acc = acc + jnp.dot(vt_ref[0, hh, j, :, r * th:(r + 1) * th], p,
                                    preferred_element_type=F32)
            new_state += [m_new, acc]
            if fetch:
                new_cms.append(cm_next)
        return new_state, new_cms

    def run_chunks(j, state, cms, last):
        for c in range(unroll):
            src, dst = (sa_ref, sb_ref) if c % 2 == 0 else (sb_ref, sa_ref)
            state, cms = step(j + c, src, dst, state, cms, not (last and c == unroll - 1))
        return state, cms

    m0 = jnp.full((1, tq), -1e30, F32)
    acc0 = jnp.zeros((VT_ROWS, tq), F32)
    state = [m0, acc0, m0, acc0]
    cms = [functools.reduce(jnp.maximum, [scores(hh, 0, sa_ref, r) for r in range(ATT_SPLIT)])
           for hh in range(2)]
    if n_chunks == 1:
        res, _ = step(0, sa_ref, sb_ref, state, cms, False)
    else:
        unroll = 4 if (n_chunks % 4 == 0 and n_chunks >= 16) else 2

        def body(i, carry):
            st, cm = run_chunks(unroll * i, list(carry[:4]), list(carry[4:]), last=False)
            return tuple(st) + tuple(cm)

        carry = lax.fori_loop(0, n_chunks // unroll - 1, body, tuple(state) + tuple(cms))
        res, _ = run_chunks(n_chunks - unroll, list(carry[:4]), list(carry[4:]), last=True)
    o_t = jnp.concatenate([res[1][:V_DIM] / res[1][V_DIM:V_DIM + 1],
                           res[3][:V_DIM] / res[3][V_DIM:V_DIM + 1]], axis=0)
    o_ref[0] = o_t.T


def _attn_call(q, k, vt, tq):
    B, _, L, _ = q.shape
    _, _, n_chunks, _, tk = vt.shape
    assert n_chunks == 1 or n_chunks % 2 == 0, n_chunks
    return pl.pallas_call(
        functools.partial(_attn_kernel, tq=tq, tk=tk, n_chunks=n_chunks),
        grid=(B, N_HEADS // 2, L // tq),
        in_specs=[pl.BlockSpec((1, 2, tq, LANES), lambda b, hp, i: (b, hp, i, 0)),
                  pl.BlockSpec((1, 2, L, LANES), lambda b, hp, i: (b, hp, 0, 0)),
                  pl.BlockSpec((1, 2, n_chunks, VT_ROWS, tk), lambda b, hp, i: (b, hp, 0, 0, 0))],
        out_specs=pl.BlockSpec((1, tq, LANES), lambda b, hp, i: (b, i, hp)),
        out_shape=jax.ShapeDtypeStruct((B, L, N_HEADS * V_DIM), F32),
        scratch_shapes=[pltpu.VMEM((2, tk, tq), F32), pltpu.VMEM((2, tk, tq), F32)],
        compiler_params=_cparams(("parallel", "parallel", "arbitrary")),
        name="attention",
    )(q, k, vt)


def _row_parts(tm):
    part = tm // MLP_PARTS if tm % (MLP_PARTS * HALO) == 0 else tm
    return [slice(r, r + part) for r in range(0, tm, part)]


def _mlp(x, g_ref, w1_ref, w2_ref):
    h = _rms(x, g_ref[...]).astype(BF16)
    acc = x
    chunk = D_MODEL
    for c in range(D_FF // chunk):
        hid = jnp.dot(h, w1_ref[:, c * chunk:(c + 1) * chunk], preferred_element_type=F32)
        hid = jnp.square(jnp.maximum(hid, 0.0)).astype(BF16)
        acc = acc + jnp.dot(hid, w2_ref[c * chunk:(c + 1) * chunk, :], preferred_element_type=F32)
    return acc


def _out_kernel(x_ref, yh_ref, ya_ref, wo_ref, g_ref, w1_ref, w2_ref, o_ref):
    tm = x_ref.shape[1]
    for rows in _row_parts(tm):
        y = jnp.concatenate([yh_ref[0, c, rows, :] for c in range(D_HY // HY_CB)]
                            + [ya_ref[0, rows, :].astype(BF16)], axis=1)
        x = x_ref[0, rows, :] + jnp.dot(y, wo_ref[...], preferred_element_type=F32)
        o_ref[0, rows, :] = _mlp(x, g_ref, w1_ref, w2_ref)


def _out_call(x, yh, ya, wo, g, w1, w2, tm):
    B, L, _ = x.shape
    tok = lambda b, i: (b, i, 0)
    return pl.pallas_call(
        _out_kernel,
        grid=(B, L // tm),
        in_specs=[pl.BlockSpec((1, tm, D_MODEL), tok),
                  pl.BlockSpec((1, D_HY // HY_CB, tm, HY_CB), lambda b, i: (b, 0, i, 0)),
                  pl.BlockSpec((1, tm, D_HY), tok), _const_spec(wo.shape), _const_spec(g.shape),
                  _const_spec(w1.shape), _const_spec(w2.shape)],
        out_specs=pl.BlockSpec((1, tm, D_MODEL), tok),
        out_shape=jax.ShapeDtypeStruct(x.shape, F32),
        compiler_params=_cparams(("parallel", "arbitrary")),
        name="even_out_mlp",
    )(x, yh, ya, wo, g, w1, w2)


def _odd_kernel(xp_ref, xc_ref, xn_ref, gm_ref, pw_ref, ps_ref, g_ref, w1_ref, w2_ref, fn_ref,
                o_ref, *, tm, n_tiles, seq, final):
    i = pl.program_id(1)
    gm = gm_ref[...]
    x = xc_ref[0]
    hp = _rms(xp_ref[0], gm) * jnp.where(i > 0, 1.0, 0.0)
    hn = _rms(xn_ref[0], gm) * jnp.where(i < n_tiles - 1, 1.0, 0.0)
    hc = _rms(x, gm)
    e = jnp.concatenate([hp, hc, hn], axis=0)
    ps = ps_ref[...]
    for rows in _row_parts(tm):
        part = rows.stop - rows.start
        n = part + 2 * HALO
        ep = e[rows.start:rows.start + n]
        t = i * tm + rows.start + lax.broadcasted_iota(jnp.int32, (part, 1), 0)
        mixed = []
        for gi, w in enumerate(POOL_WINDOWS):
            sl = slice(gi * POOL_GRP, (gi + 1) * POOL_GRP)
            eg = ep[:, sl]
            s = pltpu.roll(eg, 1, 0) + eg
            half = 1
            while 2 * half < w:
                s = pltpu.roll(s, half, 0) + pltpu.roll(s, n - half, 0)
                half *= 2
            cnt = (jnp.minimum(t + w // 2, seq) - jnp.maximum(t - w // 2, 0)).astype(F32)
            d = s[HALO:HALO + part] / cnt - eg[HALO:HALO + part]
            mixed.append(_bdot(d, pw_ref[gi]) * ps[:, sl])
        y = _mlp(x[rows] + jnp.concatenate(mixed, axis=1), g_ref, w1_ref, w2_ref)
        if final:
            y = _rms(y, fn_ref[...])
        o_ref[0, rows, :] = y


def _odd_call(x, gm, pw, ps, g, w1, w2, fn, tm, final):
    B, L, _ = x.shape
    n_tiles = L // tm
    hb = tm // HALO
    n_hb = L // HALO
    tok = lambda b, i: (b, i, 0)
    return pl.pallas_call(
        functools.partial(_odd_kernel, tm=tm, n_tiles=n_tiles, seq=L, final=final),
        grid=(B, n_tiles),
        in_specs=[
            pl.BlockSpec((1, HALO, D_MODEL), lambda b, i: (b, jnp.maximum(i * hb - 1, 0), 0)),
            pl.BlockSpec((1, tm, D_MODEL), tok),
            pl.BlockSpec((1, HALO, D_MODEL), lambda b, i: (b, jnp.minimum((i + 1) * hb, n_hb - 1), 0)),
            _const_spec(gm.shape), _const_spec(pw.shape), _const_spec(ps.shape), _const_spec(g.shape),
            _const_spec(w1.shape), _const_spec(w2.shape), _const_spec(fn.shape)],
        out_specs=pl.BlockSpec((1, tm, D_MODEL), tok),
        out_shape=jax.ShapeDtypeStruct(x.shape, F32),
        compiler_params=_cparams(("parallel", "arbitrary")),
        name="odd_pool_mlp",
    )(x, x, x, gm, pw, ps, g, w1, w2, fn)


def _coarse_fwd_kernel(x_ref, g_ref, o_ref):
    x = x_ref[0, :, 0]
    a, _, cb = x.shape
    xb = x.reshape(a * SUBLANES, cb).astype(BF16)
    y = jnp.dot(g_ref[...], xb, preferred_element_type=F32)
    o_ref[0, :, 0] = y.reshape(y.shape[0] // SUBLANES, SUBLANES, cb)


def _coarse_fwd_call(x, g_a, cb):
    B, L, C = x.shape
    a = L // FFT_N2
    n1 = 2 * a
    nb = FFT_N2 // SUBLANES
    x5 = x.reshape(B, a, nb, SUBLANES, C)
    out = pl.pallas_call(
        _coarse_fwd_kernel,
        grid=(B, nb, C // cb),
        in_specs=[pl.BlockSpec((1, a, 1, SUBLANES, cb), lambda b, r, c: (b, 0, r, 0, c)),
                  _const_spec(g_a.shape)],
        out_specs=pl.BlockSpec((1, n1, 1, SUBLANES, cb), lambda b, r, c: (b, 0, r, 0, c)),
        out_shape=jax.ShapeDtypeStruct((B, n1, nb, SUBLANES, C), F32),
        compiler_params=_cparams(("parallel", "parallel", "arbitrary")),
        name="hyena_coarse_fwd",
    )(x5, g_a)
    return out.reshape(B, n1, FFT_N2, C)


def _cmul(x, kr, ki):
    xr, xi = x[:FFT_N2], x[FFT_N2:]
    return jnp.concatenate([xr * kr - xi * ki, xr * ki + xi * kr], axis=0)


def _conv_kernel(xf_ref, ga_ref, t_ref, ta_ref, kf_ref, kfn_ref, gb_ref, u_ref, gate_ref, bias_ref,
                 o_ref, s_ref, *, nb, nj, jb):
    t = pl.program_id(2)
    n1, _, cb = s_ref.shape

    def rows(step, g):
        return pl.ds(pl.multiple_of((step * HY_RG + g) * SUBLANES, SUBLANES), SUBLANES)

    @pl.when(t < nb)
    def _():
        x_all = xf_ref[0, 0, :, 0].astype(F32)
        for g in range(HY_RG):
            x = x_all[:, g * SUBLANES:(g + 1) * SUBLANES, :]
            xb = x.reshape(x.shape[0] * SUBLANES, cb).astype(BF16)
            y = jnp.dot(ga_ref[...], xb, preferred_element_type=F32)
            s_ref[:, rows(t, g), :] = y.reshape(n1, SUBLANES, cb)

    def pair(jj):
        return (t - nb) * jb + jj

    def load_pair(jj):
        return s_ref[pl.ds(2 * pair(jj), 2)].reshape(2 * FFT_N2, cb)

    def store_pair(jj, z):
        s_ref[pl.ds(2 * pair(jj), 2)] = z.reshape(2, FFT_N2, cb)

    def fine(jjs):
        xs = [_bdot(t_ref[pair(jj)], load_pair(jj)) for jj in jjs]
        ys = [_cmul(x, kf_ref[0, pair(jj)], kf_ref[1, pair(jj)]) for jj, x in zip(jjs, xs)]
        for jj, y in zip(jjs, ys):
            store_pair(jj, _bdot(ta_ref[pair(jj)], y))

    def fine_first_pair():
        s = load_pair(0)
        lo = lax.broadcasted_iota(jnp.int32, (2 * FFT_N2, 1), 0) < FFT_N2
        xa = _bdot(t_ref[0], jnp.where(lo, s, 0.0))
        xb = _bdot(t_ref[0], jnp.where(lo, 0.0, s))
        ya = _cmul(xa, kf_ref[0, 0], kf_ref[1, 0])
        yb = _cmul(xb, kfn_ref[0, 0], kfn_ref[1, 0])
        store_pair(0, jnp.where(lo, _bdot(ta_ref[0], ya), _bdot(ta_ref[0], yb)))

    @pl.when(t == nb)
    def _():
        fine_first_pair()
        fine(list(range(1, jb)))

    @pl.when((t > nb) & (t < nb + nj))
    def _():
        fine(list(range(jb)))

    @pl.when(t >= nb + nj)
    def _():
        u_all = u_ref[0, 0, :, 0].astype(F32)
        gate_all = gate_ref[0, 0, :, 0].astype(F32)
        outs = []
        for g in range(HY_RG):
            sub = slice(g * SUBLANES, (g + 1) * SUBLANES)
            z = s_ref[:, rows(t - nb - nj, g), :]
            y = jnp.dot(gb_ref[...], z.reshape(n1 * SUBLANES, cb).astype(BF16), preferred_element_type=F32)
            y = y.reshape(y.shape[0] // SUBLANES, SUBLANES, cb)
            outs.append((y + u_all[:, sub, :] * bias_ref[...]) * gate_all[:, sub, :])
        o_ref[0, 0, :, 0] = jnp.concatenate(outs, axis=1).astype(o_ref.dtype)


def _conv_call(u, gate, bias, kf, kfn, filt, tabs, jb):
    g_a, g_b, t_f, t_i = tabs
    B, nc, L, cb = u.shape
    a = L // FFT_N2
    n1 = 2 * a
    rg_rows = HY_RG * SUBLANES
    assert u.dtype == BF16 and rg_rows == 2 * SUBLANES
    nb = FFT_N2 // rg_rows
    nj = (n1 // 2) // jb
    koff = filt * nc
    u5 = u.reshape(B, nc, a, nb, rg_rows, cb)
    fwd_r = lambda c, b, t: (b, c, 0, jnp.minimum(t, nb - 1), 0, 0)
    inv_r = lambda c, b, t: (b, c, 0, jnp.clip(t - nb - nj, 0, nb - 1), 0, 0)
    tok = (1, 1, a, 1, rg_rows, cb)
    kspec = lambda n: pl.BlockSpec((2, n, FFT_N2, cb), lambda c, b, t: (0, 0, 0, koff + c),
                                   pipeline_mode=pl.Buffered(1))
    out = pl.pallas_call(
        functools.partial(_conv_kernel, nb=nb, nj=nj, jb=jb),
        grid=(nc, B, 2 * nb + nj),
        in_specs=[pl.BlockSpec(tok, fwd_r), _const_spec(g_a.shape), _const_spec(t_f.shape),
                  _const_spec(t_i.shape), kspec(n1 // 2), kspec(1),
                  _const_spec(g_b.shape), pl.BlockSpec(tok, inv_r), pl.BlockSpec(tok, inv_r),
                  pl.BlockSpec((1, cb), lambda c, b, t: (0, c))],
        out_specs=pl.BlockSpec(tok, inv_r),
        out_shape=jax.ShapeDtypeStruct(u5.shape, u.dtype),
        scratch_shapes=[pltpu.VMEM((n1, FFT_N2, cb), F32)],
        compiler_params=_cparams(("arbitrary", "arbitrary", "arbitrary")),
        name="hyena_conv",
    )(u5, g_a, t_f, t_i, kf, kfn, g_b, u5, gate.reshape(u5.shape), bias)
    return out.reshape(u.shape)


def _filter_kernel(z_ref, w1_ref, b1_ref, fr_ref, w2_ref, b2_ref, w3_ref, dec_ref, o_ref, *, tl):
    hi = lax.Precision.HIGHEST
    fr = fr_ref[...]
    h = jnp.sin(fr * (jnp.dot(z_ref[...], w1_ref[...], precision=hi, preferred_element_type=F32) + b1_ref[...]))
    h = jnp.sin(fr * (jnp.dot(h, w2_ref[...], precision=hi, preferred_element_type=F32) + b2_ref[...]))
    h = jnp.dot(h, w3_ref[...], precision=hi, preferred_element_type=F32)
    dec = dec_ref[...]
    h = h * jnp.concatenate([dec] * (2 * N_FILT), axis=1)
    row = pl.program_id(0) * tl + lax.broadcasted_iota(jnp.int32, h.shape, 0)
    lane = lax.broadcasted_iota(jnp.int32, h.shape, 1)
    o_ref[...] = jnp.where((row == 0) & (lane >= N_FILT * D_HY), 0.0, h)


def _filter_call(z, w1, b1, fr, w2, b2, w3, dec, tl):
    L = z.shape[0]
    return pl.pallas_call(
        functools.partial(_filter_kernel, tl=tl),
        grid=(L // tl,),
        in_specs=[pl.BlockSpec((tl, LANES), lambda i: (i, 0)),
                  _const_spec(w1.shape), _const_spec(b1.shape), _const_spec(fr.shape),
                  _const_spec(w2.shape), _const_spec(b2.shape), _const_spec(w3.shape),
                  pl.BlockSpec((tl, D_HY), lambda i: (i, 0))],
        out_specs=pl.BlockSpec((tl, 2 * N_FILT * D_HY), lambda i: (i, 0)),
        out_shape=jax.ShapeDtypeStruct((L, 2 * N_FILT * D_HY), F32),
        compiler_params=_cparams(("arbitrary",)),
        name="hyena_filter_mlp",
    )(z, w1, b1, fr, w2, b2, w3, dec)


def _spec_kernel(sf_ref, sb_ref, t_ref, kf_ref, kfn_ref, *, jb):
    step = pl.program_id(1)
    cb = sf_ref.shape[-1]

    def combine(xf, xb):
        return jnp.concatenate([xf[:FFT_N2] + xb[:FFT_N2], xf[FFT_N2:] - xb[FFT_N2:]], axis=0)

    def generic(jj):
        xf = _bdot(t_ref[jj], sf_ref[0, 2 * jj:2 * jj + 2].reshape(2 * FFT_N2, cb))
        xb = _bdot(t_ref[jj], sb_ref[0, 2 * jj:2 * jj + 2].reshape(2 * FFT_N2, cb))
        kf_ref[:, jj] = combine(xf, xb).reshape(2, FFT_N2, cb).astype(kf_ref.dtype)

    def special():
        row = lax.broadcasted_iota(jnp.int32, (2 * FFT_N2, 1), 0)
        lo = row < FFT_N2
        sf = sf_ref[0, 0:2].reshape(2 * FFT_N2, cb)
        sb = sb_ref[0, 0:2].reshape(2 * FFT_N2, cb)
        k0 = combine(_bdot(t_ref[0], jnp.where(lo, sf, 0.0)), _bdot(t_ref[0], jnp.where(lo, sb, 0.0)))
        kn = combine(_bdot(t_ref[0], jnp.where(lo, 0.0, sf)), _bdot(t_ref[0], jnp.where(lo, 0.0, sb)))
        kf_ref[:, 0] = k0.reshape(2, FFT_N2, cb).astype(kf_ref.dtype)
        kfn_ref[:, 0] = kn.reshape(2, FFT_N2, cb).astype(kfn_ref.dtype)

    pl.when(step == 0)(special)
    pl.when(step != 0)(lambda: generic(0))
    for jj in range(1, jb):
        generic(jj)


def _spec_call(s, t_f, cb, jb):
    _, n1, _, c2 = s.shape
    npair = n1 // 2
    C = c2 // 2
    nc = C // cb
    return pl.pallas_call(
        functools.partial(_spec_kernel, jb=jb),
        grid=(nc, npair // jb),
        in_specs=[pl.BlockSpec((1, 2 * jb, FFT_N2, cb), lambda c, j: (0, j, 0, c)),
                  pl.BlockSpec((1, 2 * jb, FFT_N2, cb), lambda c, j: (0, j, 0, nc + c)),
                  pl.BlockSpec((jb, 2 * FFT_N2, 2 * FFT_N2), lambda c, j: (j, 0, 0))],
        out_specs=[pl.BlockSpec((2, jb, FFT_N2, cb), lambda c, j: (0, j, 0, c)),
                   pl.BlockSpec((2, 1, FFT_N2, cb), lambda c, j: (0, 0, 0, c))],
        out_shape=[jax.ShapeDtypeStruct((2, npair, FFT_N2, C), BF16),
                   jax.ShapeDtypeStruct((2, 1, FFT_N2, C), BF16)],
        compiler_params=_cparams(("arbitrary", "arbitrary")),
        name="hyena_filter_spectrum",
    )(s, s, t_f)


def _dft_tables(L):
    a_n = L // FFT_N2
    n1 = 2 * a_n
    n = 2 * L
    npair = n1 // 2
    two_pi = 2.0 * math.pi

    a = jnp.arange(a_n, dtype=jnp.int32)[None, :]
    slot = jnp.arange(n1, dtype=jnp.int32)[:, None]
    j = slot // 2
    ang = two_pi * ((a * j) % n1).astype(F32) / n1
    sign = jnp.where(a % 2 == 0, 1.0, -1.0)
    is_im = (slot % 2) == 1
    f1 = jnp.where(is_im, -jnp.sin(ang), jnp.cos(ang))
    f1 = jnp.where(slot == 0, 1.0, jnp.where(slot == 1, sign, f1))
    fb = jnp.where(is_im, -2.0 * jnp.sin(ang), 2.0 * jnp.cos(ang))
    fb = jnp.where(slot == 0, 1.0, jnp.where(slot == 1, sign, fb)) / n
    eye = jnp.eye(SUBLANES, dtype=F32)
    g_a = jnp.kron(f1, eye).astype(BF16)
    g_b = jnp.kron(fb.T, eye).astype(BF16)

    b = jnp.arange(FFT_N2, dtype=jnp.int32)[None, None, :]
    d = jnp.arange(FFT_N2, dtype=jnp.int32)[None, :, None]
    c = jnp.arange(npair + 1, dtype=jnp.int32)[:, None, None]
    phi = two_pi * ((b * (d * n1 + c)) % n).astype(F32) / n
    co, si = jnp.cos(phi), jnp.sin(phi)
    t_f = jnp.concatenate([jnp.concatenate([co, si], axis=2),
                           jnp.concatenate([-si, co], axis=2)], axis=1)[:npair]
    t0 = jnp.concatenate([jnp.concatenate([co[0], co[npair]], axis=1),
                          jnp.concatenate([-si[0], -si[npair]], axis=1)], axis=0)
    t_f = t_f.at[0].set(t0)
    t_i = jnp.swapaxes(t_f, 1, 2)
    return g_a, g_b, t_f.astype(BF16), t_i.astype(BF16)


def _filter_tables(L):
    t = jnp.linspace(0.0, 1.0, L, dtype=F32)[:, None]
    wpos = (2.0 * math.pi / L) * jnp.arange(L, dtype=F32)[:, None]
    bands = jnp.linspace(1e-4, N_BANDS - 1, N_BANDS, dtype=F32)[None, :]
    z = jnp.concatenate([t, jnp.cos(bands * wpos), -jnp.sin(bands * wpos)], axis=-1)
    z = jnp.pad(z, ((0, 0), (0, LANES - EMB_DIM)))
    max_decay = math.log(DECAY_TARGET) / DECAY_FAST
    min_decay = math.log(DECAY_TARGET) / DECAY_SLOW
    deltas = jnp.linspace(min_decay, max_decay, D_HY, dtype=F32)
    decay = jnp.exp(-t * jnp.abs(deltas))
    return z, decay


def _rope_tables(L):
    inv_freq = 1.0 / (ROPE_THETA ** (jnp.arange(0, ROPE_DIM, 2, dtype=F32) / ROPE_DIM))
    ang = jnp.arange(L, dtype=F32)[:, None] * inv_freq[None, :]
    cos, sin = jnp.cos(ang), jnp.sin(ang)
    one = jnp.ones((L, NOPE_DIM), F32)
    zero_n = jnp.zeros((L, NOPE_DIM), F32)
    zero_t = jnp.zeros((L, LANES - NOPE_DIM - ROPE_DIM), F32)
    cos_t = jnp.concatenate([one, cos, cos, zero_t], axis=1)
    sin_t = jnp.concatenate([zero_n, -sin, sin, zero_t], axis=1)
    return cos_t, sin_t


def _swap_halves(w):
    half = w.shape[-1] // 2
    return jnp.concatenate([w[..., half:], w[..., :half]], axis=-1)


def _place(w, width, offset):
    return jnp.pad(w, ((0, 0), (offset, width - offset - w.shape[1])))


def _even_weights(w_in, w_uq, w_ukv):
    why = w_in[:, :HY_COLS].astype(BF16)
    w_kr = w_in[:, HY_COLS + Q_LORA + KV_LORA:]
    wlat = jnp.concatenate([
        w_in[:, HY_COLS:HY_COLS + Q_LORA + KV_LORA],
        _place(w_kr, LANES, NOPE_DIM),
        _place(_swap_halves(w_kr), LANES, NOPE_DIM)], axis=1).astype(BF16)
    qh = w_uq.reshape(Q_LORA, N_HEADS, NOPE_DIM + ROPE_DIM)
    q_main = jnp.pad(qh, ((0, 0), (0, 0), (0, LANES - NOPE_DIM - ROPE_DIM)))
    q_sw = jnp.pad(_swap_halves(qh[..., NOPE_DIM:]),
                   ((0, 0), (0, 0), (NOPE_DIM, LANES - NOPE_DIM - ROPE_DIM)))
    wq = jnp.concatenate([q_main.reshape(Q_LORA, -1), q_sw.reshape(Q_LORA, -1)], axis=1).astype(BF16)
    kvh = w_ukv.reshape(KV_LORA, N_HEADS, NOPE_DIM + V_DIM)
    k_main = jnp.pad(kvh[..., :NOPE_DIM], ((0, 0), (0, 0), (0, LANES - NOPE_DIM)))
    wk = k_main.reshape(KV_LORA, -1).astype(BF16)
    v_rows = jnp.pad(kvh[..., NOPE_DIM:], ((0, 0), (0, 0), (0, VT_ROWS - V_DIM)))
    wvt = v_rows.reshape(KV_LORA, -1).T.astype(BF16)
    vone = (jnp.arange(N_HEADS * VT_ROWS) % VT_ROWS == V_DIM).astype(F32)[:, None]
    return why, wlat, wq, wk, wvt, vone


def _tile(L, want):
    return min(want, L)


def _hyena(x1, x2, v, kf, kfn, hy_bias, tabs, jb):
    z = v
    for f, gate in enumerate((x1, x2)):
        z = _conv_call(z, gate, hy_bias[f][None, :], kf, kfn, f, tabs, jb)
    return z


def _filter_spectrum(L, w1, b1, fr, w2, b2, w3, tabs, ftabs, cb, jb):
    g_a, _, t_f, _ = tabs
    z, decay = ftabs
    pad_h = LANES - FILT_HID
    w1p = jnp.pad(w1, ((0, LANES - EMB_DIM), (0, pad_h)))
    w2p = jnp.pad(w2, ((0, pad_h), (0, pad_h)))
    w3p = jnp.pad(w3, ((0, pad_h), (0, 0)))
    pad1 = lambda a: jnp.pad(a, (0, pad_h))[None, :]
    hf = _filter_call(z, w1p, pad1(b1), pad1(fr), w2p, pad1(b2), w3p, decay, _tile(L, 512))
    s = _coarse_fwd_call(hf[None], g_a, cb)
    return _spec_call(s, t_f, cb, jb)


def _trunk(x, mix_norm, w_in, hy_conv_w, hy_conv_b, hf_w1, hf_b1, hf_freq, hf_w2, hf_b2, hf_w3, hy_bias,
           q_norm, w_uq, kv_norm, w_ukv, w_out, pool_w, pool_scale, mlp_norm, mlp_w1, mlp_w2, final_norm):
    B, L, _ = x.shape
    depth = mix_norm.shape[0]
    tm = _tile(L, 512)
    cb = 512
    jb = min(4, L // FFT_N2)
    tabs = _dft_tables(L)
    ftabs = _filter_tables(L)
    rope_tabs = _rope_tables(L)
    row = lambda a: a[None, :]
    for i in range(depth):
        w1 = mlp_w1[i].astype(BF16)
        w2 = mlp_w2[i].astype(BF16)
        g = row(mlp_norm[i])
        if i % 2 == 0:
            e = i // 2
            why, wlat, wq, wk, wvt, vone = _even_weights(w_in[e], w_uq[e], w_ukv[e])
            x1, x2, v, q, k, vt = _in_call(
                x, row(mix_norm[i]), why, wlat, hy_conv_w[e], row(hy_conv_b[e]),
                row(q_norm[e]), wq, row(kv_norm[e]), wk, wvt, vone, rope_tabs, tm)
            kf, kfn = _filter_spectrum(L, hf_w1[e], hf_b1[e], hf_freq[e], hf_w2[e], hf_b2[e], hf_w3[e],
                                       tabs, ftabs, cb, jb)
            y_hy = _hyena(x1, x2, v, kf, kfn, hy_bias[e], tabs, min(8, L // FFT_N2))
            y_att = _attn_call(q, k, vt, _tile(L, 512))
            x = _out_call(x, y_hy, y_att, w_out[e].astype(BF16), g, w1, w2, tm)
        else:
            o = i // 2
            x = _odd_call(x, row(mix_norm[i]), pool_w[o].astype(BF16), row(pool_scale[o]), g, w1, w2,
                          row(final_norm), tm, final=(i == depth - 1))
    if depth % 2 == 1:
        raise NotImplementedError("final norm is fused into the last odd layer")
    return x


def kernel(x_prompt, x_sample, mix_norm, w_in, hy_conv_w, hy_conv_b, hf_w1, hf_b1, hf_freq, hf_w2, hf_b2,
           hf_w3, hy_bias, q_norm, w_uq, kv_norm, w_ukv, w_out, pool_w, pool_scale, mlp_norm, mlp_w1,
           mlp_w2, final_norm):
    weights = (mix_norm, w_in, hy_conv_w, hy_conv_b, hf_w1, hf_b1, hf_freq, hf_w2, hf_b2, hf_w3, hy_bias,
               q_norm, w_uq, kv_norm, w_ukv, w_out, pool_w, pool_scale, mlp_norm, mlp_w1, mlp_w2, final_norm)
    return (_trunk(x_prompt, *weights), _trunk(x_sample, *weights))
```

```python
import functools
import math

import jax
import jax.numpy as jnp
from jax import lax
from jax.experimental import pallas as pl
from jax.experimental.pallas import tpu as pltpu

F32 = jnp.float32
BF16 = jnp.bfloat16

D_MODEL = 1024
D_HY = 512
N_FILT = 2
HY_COLS = (N_FILT + 1) * D_HY
EMB_DIM = 33
N_BANDS = (EMB_DIM - 1) // 2
FILT_HID = 64
DECAY_FAST = 0.3
DECAY_SLOW = 1.5
DECAY_TARGET = 1e-2
N_HEADS = 8
NOPE_DIM = 64
ROPE_DIM = 32
V_DIM = 64
Q_LORA = 384
KV_LORA = 256
ROPE_THETA = 10000.0
POOL_WINDOWS = (2, 4, 8, 16)
POOL_GRP = D_MODEL // len(POOL_WINDOWS)
D_FF = 4 * D_MODEL
EPS = 1e-6
ATT_SCALE = (NOPE_DIM + ROPE_DIM) ** -0.5
Q_SCALE = ATT_SCALE * math.log2(math.e)
VT_ROWS = 80
HY_CB = 256
HY_RG = 2
MLP_PARTS = 1
ATT_SPLIT = 1
ATT_UNROLL = 4

LANES = 128
SUBLANES = 8
HALO = 16
FFT_N2 = 128
VMEM_LIMIT = 56 * 1024 * 1024


def _cparams(sem):
    return pltpu.CompilerParams(dimension_semantics=sem, vmem_limit_bytes=VMEM_LIMIT)


def _const_spec(shape):
    nd = len(shape)
    return pl.BlockSpec(shape, lambda *_: (0,) * nd, pipeline_mode=pl.Buffered(1))


def _rms(xf, g):
    ms = jnp.mean(xf * xf, axis=-1, keepdims=True)
    return xf * lax.rsqrt(ms + EPS) * g


def _bdot(a, b):
    return jnp.dot(a.astype(BF16), b.astype(BF16), preferred_element_type=F32)


def _in_kernel(xp_ref, xc_ref, xn_ref, g_ref, why_ref, wlat_ref, cw_ref, cb_ref,
               qn_ref, wq_ref, kvn_ref, wk_ref, wvt_ref, vone_ref, cos_ref, sin_ref,
               x1_ref, x2_ref, v_ref, q_ref, k_ref, vt_ref, *, tm, n_tiles):
    i = pl.program_id(1)
    g = g_ref[...]
    hc = _rms(xc_ref[0], g)
    hp = _rms(xp_ref[0], g) * jnp.where(i > 0, 1.0, 0.0)
    hn = _rms(xn_ref[0], g) * jnp.where(i < n_tiles - 1, 1.0, 0.0)
    hcb = hc.astype(BF16)
    h_ext = jnp.concatenate([hp.astype(BF16), hcb, hn.astype(BF16)], axis=0)
    p = jnp.dot(h_ext, why_ref[...], preferred_element_type=F32)
    n = tm + 2 * HALO
    cw = cw_ref[...]
    u = (pltpu.roll(p, 1, 0)[HALO:HALO + tm] * cw[0:1]
         + p[HALO:HALO + tm] * cw[1:2]
         + pltpu.roll(p, n - 1, 0)[HALO:HALO + tm] * cw[2:3]
         + cb_ref[...])
    for i_out, ref in enumerate((x1_ref, x2_ref, v_ref)):
        for c in range(D_HY // HY_CB):
            lo = i_out * D_HY + c * HY_CB
            ref[0, c] = u[:, lo:lo + HY_CB].astype(ref.dtype)

    lat = jnp.dot(hcb, wlat_ref[...], preferred_element_type=F32)
    cq = lat[:, :Q_LORA]
    ckv = lat[:, Q_LORA:Q_LORA + KV_LORA]
    kr = lat[:, Q_LORA + KV_LORA:Q_LORA + KV_LORA + LANES]
    krs = lat[:, Q_LORA + KV_LORA + LANES:]
    q2 = _bdot(_rms(cq, qn_ref[...]), wq_ref[...])
    ckvn = _rms(ckv, kvn_ref[...]).astype(BF16)
    k2 = jnp.dot(ckvn, wk_ref[...], preferred_element_type=F32)
    v_t = lax.dot_general(wvt_ref[...], ckvn, (((1,), (1,)), ((), ())),
                          preferred_element_type=F32) + vone_ref[...]
    hw = N_HEADS * LANES
    cos_t = cos_ref[...]
    sin_t = sin_ref[...]
    kr_r = kr * cos_t + krs * sin_t
    for h in range(N_HEADS):
        sl = slice(h * LANES, (h + 1) * LANES)
        qh = (q2[:, sl] * cos_t + q2[:, hw + h * LANES:hw + (h + 1) * LANES] * sin_t) * Q_SCALE
        q_ref[0, h] = qh.astype(BF16)
        k_ref[0, h] = (k2[:, sl] + kr_r).astype(BF16)
        vt_ref[0, h, 0] = v_t[h * VT_ROWS:(h + 1) * VT_ROWS, :].astype(BF16)


def _in_call(x, g, why, wlat, cw, cb, qn, wq, kvn, wk, wvt, vone, rope_tabs, tm):
    B, L, _ = x.shape
    n_tiles = L // tm
    hb = tm // HALO
    n_hb = L // HALO
    tok = lambda b, i: (b, i, 0)
    in_specs = [
        pl.BlockSpec((1, HALO, D_MODEL), lambda b, i: (b, jnp.maximum(i * hb - 1, 0), 0)),
        pl.BlockSpec((1, tm, D_MODEL), tok),
        pl.BlockSpec((1, HALO, D_MODEL), lambda b, i: (b, jnp.minimum((i + 1) * hb, n_hb - 1), 0)),
        _const_spec(g.shape), _const_spec(why.shape), _const_spec(wlat.shape),
        _const_spec(cw.shape), _const_spec(cb.shape), _const_spec(qn.shape),
        _const_spec(wq.shape), _const_spec(kvn.shape), _const_spec(wk.shape),
        _const_spec(wvt.shape), _const_spec(vone.shape),
    ] + [pl.BlockSpec((tm, LANES), lambda b, i: (i, 0)) for _ in rope_tabs]
    hy = jax.ShapeDtypeStruct((B, D_HY // HY_CB, L, HY_CB), BF16)
    hd = jax.ShapeDtypeStruct((B, N_HEADS, L, LANES), BF16)
    hy_spec = pl.BlockSpec((1, D_HY // HY_CB, tm, HY_CB), lambda b, i: (b, 0, i, 0))
    hd_spec = pl.BlockSpec((1, N_HEADS, tm, LANES), lambda b, i: (b, 0, i, 0))
    vt = jax.ShapeDtypeStruct((B, N_HEADS, n_tiles, VT_ROWS, tm), BF16)
    vt_spec = pl.BlockSpec((1, N_HEADS, 1, VT_ROWS, tm), lambda b, i: (b, 0, i, 0, 0))
    return pl.pallas_call(
        functools.partial(_in_kernel, tm=tm, n_tiles=n_tiles),
        grid=(B, n_tiles),
        in_specs=in_specs,
        out_specs=[hy_spec, hy_spec, hy_spec, hd_spec, hd_spec, vt_spec],
        out_shape=[hy, hy, hy, hd, hd, vt],
        compiler_params=_cparams(("parallel", "arbitrary")),
        name="even_in",
    )(x, x, x, g, why, wlat, cw, cb, qn, wq, kvn, wk, wvt, vone, *rope_tabs)


def _attn_kernel(q_ref, k_ref, vt_ref, o_ref, sa_ref, sb_ref, *, tq, tk, n_chunks):
    qs = [q_ref[0, hh] for hh in range(2)]

    th = tk // ATT_SPLIT

    def scores(hh, j, s_ref, r):
        start = pl.multiple_of(j * tk + r * th, th)
        kc = k_ref[0, hh, pl.ds(start, th), :]
        s = lax.dot_general(kc, qs[hh], (((1,), (1,)), ((), ())), preferred_element_type=F32)
        s_ref[hh, r * th:(r + 1) * th, :] = s
        return jnp.max(s, axis=0, keepdims=True)

    def step(j, src_ref, dst_ref, state, cms, fetch):
        new_state, new_cms = [], []
        for hh in range(2):
            m, acc = state[2 * hh], state[2 * hh + 1]
            m_new = jnp.maximum(m, cms[hh])
            acc = acc * jnp.exp2(m - m_new)
            cm_next = None
            for r in range(ATT_SPLIT):
                if fetch:
                    cm_r = scores(hh, j + 1, dst_ref, r)
                    cm_next = cm_r if cm_next is None else jnp.maximum(cm_next, cm_r)
                p = jnp.exp2(src_ref[hh, r * th:(r + 1) * th, :] - m_new).astype(BF16)
                acc = acc + jnp.dot(vt_ref[0, hh, j, :, r * th:(r + 1) * th], p,
                                    preferred_element_type=F32)
            new_state += [m_new, acc]
            if fetch:
                new_cms.append(cm_next)
        return new_state, new_cms

    def run_chunks(j, state, cms, last):
        for c in range(unroll):
            src, dst = (sa_ref, sb_ref) if c % 2 == 0 else (sb_ref, sa_ref)
            state, cms = step(j + c, src, dst, state, cms, not (last and c == unroll - 1))
        return state, cms

    m0 = jnp.full((1, tq), -1e30, F32)
    acc0 = jnp.zeros((VT_ROWS, tq), F32)
    state = [m0, acc0, m0, acc0]
    cms = [functools.reduce(jnp.maximum, [scores(hh, 0, sa_ref, r) for r in range(ATT_SPLIT)])
           for hh in range(2)]
    if n_chunks == 1:
        res, _ = step(0, sa_ref, sb_ref, state, cms, False)
    else:
        unroll = ATT_UNROLL if (n_chunks % ATT_UNROLL == 0 and n_chunks >= 4 * ATT_UNROLL) else 2

        def body(i, carry):
            st, cm = run_chunks(unroll * i, list(carry[:4]), list(carry[4:]), last=False)
            return tuple(st) + tuple(cm)

        carry = lax.fori_loop(0, n_chunks // unroll - 1, body, tuple(state) + tuple(cms))
        res, _ = run_chunks(n_chunks - unroll, list(carry[:4]), list(carry[4:]), last=True)
    o_t = jnp.concatenate([res[1][:V_DIM] / res[1][V_DIM:V_DIM + 1],
                           res[3][:V_DIM] / res[3][V_DIM:V_DIM + 1]], axis=0)
    o_ref[0] = o_t.T


def _attn_call(q, k, vt, tq):
    B, _, L, _ = q.shape
    _, _, n_chunks, _, tk = vt.shape
    assert n_chunks == 1 or n_chunks % 2 == 0, n_chunks
    return pl.pallas_call(
        functools.partial(_attn_kernel, tq=tq, tk=tk, n_chunks=n_chunks),
        grid=(B, N_HEADS // 2, L // tq),
        in_specs=[pl.BlockSpec((1, 2, tq, LANES), lambda b, hp, i: (b, hp, i, 0)),
                  pl.BlockSpec((1, 2, L, LANES), lambda b, hp, i: (b, hp, 0, 0)),
                  pl.BlockSpec((1, 2, n_chunks, VT_ROWS, tk), lambda b, hp, i: (b, hp, 0, 0, 0))],
        out_specs=pl.BlockSpec((1, tq, LANES), lambda b, hp, i: (b, i, hp)),
        out_shape=jax.ShapeDtypeStruct((B, L, N_HEADS * V_DIM), F32),
        scratch_shapes=[pltpu.VMEM((2, tk, tq), F32), pltpu.VMEM((2, tk, tq), F32)],
        compiler_params=_cparams(("parallel", "parallel", "arbitrary")),
        name="attention",
    )(q, k, vt)


def _row_parts(tm):
    part = tm // MLP_PARTS if tm % (MLP_PARTS * HALO) == 0 else tm
    return [slice(r, r + part) for r in range(0, tm, part)]


def _mlp(x, g_ref, w1_ref, w2_ref):
    h = _rms(x, g_ref[...]).astype(BF16)
    acc = x
    chunk = D_MODEL
    for c in range(D_FF // chunk):
        hid = jnp.dot(h, w1_ref[:, c * chunk:(c + 1) * chunk], preferred_element_type=F32)
        hid = jnp.square(jnp.maximum(hid, 0.0)).astype(BF16)
        acc = acc + jnp.dot(hid, w2_ref[c * chunk:(c + 1) * chunk, :], preferred_element_type=F32)
    return acc


def _out_kernel(x_ref, yh_ref, ya_ref, wo_ref, g_ref, w1_ref, w2_ref, o_ref):
    tm = x_ref.shape[1]
    for rows in _row_parts(tm):
        y = jnp.concatenate([yh_ref[0, c, rows, :] for c in range(D_HY // HY_CB)]
                            + [ya_ref[0, rows, :].astype(BF16)], axis=1)
        x = x_ref[0, rows, :] + jnp.dot(y, wo_ref[...], preferred_element_type=F32)
        o_ref[0, rows, :] = _mlp(x, g_ref, w1_ref, w2_ref)


def _out_call(x, yh, ya, wo, g, w1, w2, tm):
    B, L, _ = x.shape
    tok = lambda b, i: (b, i, 0)
    return pl.pallas_call(
        _out_kernel,
        grid=(B, L // tm),
        in_specs=[pl.BlockSpec((1, tm, D_MODEL), tok),
                  pl.BlockSpec((1, D_HY // HY_CB, tm, HY_CB), lambda b, i: (b, 0, i, 0)),
                  pl.BlockSpec((1, tm, D_HY), tok), _const_spec(wo.shape), _const_spec(g.shape),
                  _const_spec(w1.shape), _const_spec(w2.shape)],
        out_specs=pl.BlockSpec((1, tm, D_MODEL), tok),
        out_shape=jax.ShapeDtypeStruct(x.shape, F32),
        compiler_params=_cparams(("parallel", "arbitrary")),
        name="even_out_mlp",
    )(x, yh, ya, wo, g, w1, w2)


def _odd_kernel(xp_ref, xc_ref, xn_ref, gm_ref, pw_ref, ps_ref, g_ref, w1_ref, w2_ref, fn_ref,
                o_ref, *, tm, n_tiles, seq, final):
    i = pl.program_id(1)
    gm = gm_ref[...]
    x = xc_ref[0]
    hp = _rms(xp_ref[0], gm) * jnp.where(i > 0, 1.0, 0.0)
    hn = _rms(xn_ref[0], gm) * jnp.where(i < n_tiles - 1, 1.0, 0.0)
    hc = _rms(x, gm)
    e = jnp.concatenate([hp, hc, hn], axis=0)
    ps = ps_ref[...]
    for rows in _row_parts(tm):
        part = rows.stop - rows.start
        n = part + 2 * HALO
        ep = e[rows.start:rows.start + n]
        t = i * tm + rows.start + lax.broadcasted_iota(jnp.int32, (part, 1), 0)
        mixed = []
        for gi, w in enumerate(POOL_WINDOWS):
            sl = slice(gi * POOL_GRP, (gi + 1) * POOL_GRP)
            eg = ep[:, sl]
            s = pltpu.roll(eg, 1, 0) + eg
            half = 1
            while 2 * half < w:
                s = pltpu.roll(s, half, 0) + pltpu.roll(s, n - half, 0)
                half *= 2
            cnt = (jnp.minimum(t + w // 2, seq) - jnp.maximum(t - w // 2, 0)).astype(F32)
            d = s[HALO:HALO + part] / cnt - eg[HALO:HALO + part]
            mixed.append(_bdot(d, pw_ref[gi]) * ps[:, sl])
        y = _mlp(x[rows] + jnp.concatenate(mixed, axis=1), g_ref, w1_ref, w2_ref)
        if final:
            y = _rms(y, fn_ref[...])
        o_ref[0, rows, :] = y


def _odd_call(x, gm, pw, ps, g, w1, w2, fn, tm, final):
    B, L, _ = x.shape
    n_tiles = L // tm
    hb = tm // HALO
    n_hb = L // HALO
    tok = lambda b, i: (b, i, 0)
    return pl.pallas_call(
        functools.partial(_odd_kernel, tm=tm, n_tiles=n_tiles, seq=L, final=final),
        grid=(B, n_tiles),
        in_specs=[
            pl.BlockSpec((1, HALO, D_MODEL), lambda b, i: (b, jnp.maximum(i * hb - 1, 0), 0)),
            pl.BlockSpec((1, tm, D_MODEL), tok),
            pl.BlockSpec((1, HALO, D_MODEL), lambda b, i: (b, jnp.minimum((i + 1) * hb, n_hb - 1), 0)),
            _const_spec(gm.shape), _const_spec(pw.shape), _const_spec(ps.shape), _const_spec(g.shape),
            _const_spec(w1.shape), _const_spec(w2.shape), _const_spec(fn.shape)],
        out_specs=pl.BlockSpec((1, tm, D_MODEL), tok),
        out_shape=jax.ShapeDtypeStruct(x.shape, F32),
        compiler_params=_cparams(("parallel", "arbitrary")),
        name="odd_pool_mlp",
    )(x, x, x, gm, pw, ps, g, w1, w2, fn)


def _coarse_fwd_kernel(x_ref, g_ref, o_ref):
    x = x_ref[0, :, 0]
    a, _, cb = x.shape
    xb = x.reshape(a * SUBLANES, cb).astype(BF16)
    y = jnp.dot(g_ref[...], xb, preferred_element_type=F32)
    o_ref[0, :, 0] = y.reshape(y.shape[0] // SUBLANES, SUBLANES, cb)


def _coarse_fwd_call(x, g_a, cb):
    B, L, C = x.shape
    a = L // FFT_N2
    n1 = 2 * a
    nb = FFT_N2 // SUBLANES
    x5 = x.reshape(B, a, nb, SUBLANES, C)
    out = pl.pallas_call(
        _coarse_fwd_kernel,
        grid=(B, nb, C // cb),
        in_specs=[pl.BlockSpec((1, a, 1, SUBLANES, cb), lambda b, r, c: (b, 0, r, 0, c)),
                  _const_spec(g_a.shape)],
        out_specs=pl.BlockSpec((1, n1, 1, SUBLANES, cb), lambda b, r, c: (b, 0, r, 0, c)),
        out_shape=jax.ShapeDtypeStruct((B, n1, nb, SUBLANES, C), F32),
        compiler_params=_cparams(("parallel", "parallel", "arbitrary")),
        name="hyena_coarse_fwd",
    )(x5, g_a)
    return out.reshape(B, n1, FFT_N2, C)


def _cmul(x, kr, ki):
    xr, xi = x[:FFT_N2], x[FFT_N2:]
    return jnp.concatenate([xr * kr - xi * ki, xr * ki + xi * kr], axis=0)


def _conv_kernel(xf_ref, ga_ref, t_ref, ta_ref, kf_ref, kfn_ref, gb_ref, u_ref, gate_ref, bias_ref,
                 o_ref, s_ref, *, nb, nj, jb):
    t = pl.program_id(2)
    n1, _, cb = s_ref.shape

    def rows(step, g):
        return pl.ds(pl.multiple_of((step * HY_RG + g) * SUBLANES, SUBLANES), SUBLANES)

    @pl.when(t < nb)
    def _():
        x_all = xf_ref[0, 0, :, 0].astype(F32)
        for g in range(HY_RG):
            x = x_all[:, g * SUBLANES:(g + 1) * SUBLANES, :]
            xb = x.reshape(x.shape[0] * SUBLANES, cb).astype(BF16)
            y = jnp.dot(ga_ref[...], xb, preferred_element_type=F32)
            s_ref[:, rows(t, g), :] = y.reshape(n1, SUBLANES, cb)

    def pair(jj):
        return (t - nb) * jb + jj

    def load_pair(jj):
        return s_ref[pl.ds(2 * pair(jj), 2)].reshape(2 * FFT_N2, cb)

    def store_pair(jj, z):
        s_ref[pl.ds(2 * pair(jj), 2)] = z.reshape(2, FFT_N2, cb)

    def fine(jjs):
        xs = [_bdot(t_ref[pair(jj)], load_pair(jj)) for jj in jjs]
        ys = [_cmul(x, kf_ref[0, pair(jj)], kf_ref[1, pair(jj)]) for jj, x in zip(jjs, xs)]
        for jj, y in zip(jjs, ys):
            store_pair(jj, _bdot(ta_ref[pair(jj)], y))

    def fine_first_pair():
        s = load_pair(0)
        lo = lax.broadcasted_iota(jnp.int32, (2 * FFT_N2, 1), 0) < FFT_N2
        xa = _bdot(t_ref[0], jnp.where(lo, s, 0.0))
        xb = _bdot(t_ref[0], jnp.where(lo, 0.0, s))
        ya = _cmul(xa, kf_ref[0, 0], kf_ref[1, 0])
        yb = _cmul(xb, kfn_ref[0, 0], kfn_ref[1, 0])
        store_pair(0, jnp.where(lo, _bdot(ta_ref[0], ya), _bdot(ta_ref[0], yb)))

    @pl.when(t == nb)
    def _():
        fine_first_pair()
        fine(list(range(1, jb)))

    @pl.when((t > nb) & (t < nb + nj))
    def _():
        fine(list(range(jb)))

    @pl.when(t >= nb + nj)
    def _():
        u_all = u_ref[0, 0, :, 0].astype(F32)
        gate_all = gate_ref[0, 0, :, 0].astype(F32)
        outs = []
        for g in range(HY_RG):
            sub = slice(g * SUBLANES, (g + 1) * SUBLANES)
            z = s_ref[:, rows(t - nb - nj, g), :]
            y = jnp.dot(gb_ref[...], z.reshape(n1 * SUBLANES, cb).astype(BF16), preferred_element_type=F32)
            y = y.reshape(y.shape[0] // SUBLANES, SUBLANES, cb)
            outs.append((y + u_all[:, sub, :] * bias_ref[...]) * gate_all[:, sub, :])
        o_ref[0, 0, :, 0] = jnp.concatenate(outs, axis=1).astype(o_ref.dtype)


def _conv_call(u, gate, bias, kf, kfn, filt, tabs, jb):
    g_a, g_b, t_f, t_i = tabs
    B, nc, L, cb = u.shape
    a = L // FFT_N2
    n1 = 2 * a
    rg_rows = HY_RG * SUBLANES
    assert u.dtype == BF16 and rg_rows == 2 * SUBLANES
    nb = FFT_N2 // rg_rows
    nj = (n1 // 2) // jb
    koff = filt * nc
    u5 = u.reshape(B, nc, a, nb, rg_rows, cb)
    fwd_r = lambda c, b, t: (b, c, 0, jnp.minimum(t, nb - 1), 0, 0)
    inv_r = lambda c, b, t: (b, c, 0, jnp.clip(t - nb - nj, 0, nb - 1), 0, 0)
    tok = (1, 1, a, 1, rg_rows, cb)
    kspec = lambda n: pl.BlockSpec((2, n, FFT_N2, cb), lambda c, b, t: (0, 0, 0, koff + c),
                                   pipeline_mode=pl.Buffered(1))
    out = pl.pallas_call(
        functools.partial(_conv_kernel, nb=nb, nj=nj, jb=jb),
        grid=(nc, B, 2 * nb + nj),
        in_specs=[pl.BlockSpec(tok, fwd_r), _const_spec(g_a.shape), _const_spec(t_f.shape),
                  _const_spec(t_i.shape), kspec(n1 // 2), kspec(1),
                  _const_spec(g_b.shape), pl.BlockSpec(tok, inv_r), pl.BlockSpec(tok, inv_r),
                  pl.BlockSpec((1, cb), lambda c, b, t: (0, c))],
        out_specs=pl.BlockSpec(tok, inv_r),
        out_shape=jax.ShapeDtypeStruct(u5.shape, u.dtype),
        scratch_shapes=[pltpu.VMEM((n1, FFT_N2, cb), F32)],
        compiler_params=_cparams(("arbitrary", "arbitrary", "arbitrary")),
        name="hyena_conv",
    )(u5, g_a, t_f, t_i, kf, kfn, g_b, u5, gate.reshape(u5.shape), bias)
    return out.reshape(u.shape)


def _filter_kernel(z_ref, w1_ref, b1_ref, fr_ref, w2_ref, b2_ref, w3_ref, dec_ref, o_ref, *, tl):
    hi = lax.Precision.HIGHEST
    fr = fr_ref[...]
    h = jnp.sin(fr * (jnp.dot(z_ref[...], w1_ref[...], precision=hi, preferred_element_type=F32) + b1_ref[...]))
    h = jnp.sin(fr * (jnp.dot(h, w2_ref[...], precision=hi, preferred_element_type=F32) + b2_ref[...]))
    h_hi = h.astype(BF16)
    h_lo = (h - h_hi.astype(F32)).astype(BF16)
    w_hi = w3_ref[0]
    h = (jnp.dot(h_hi, w_hi, preferred_element_type=F32) + jnp.dot(h_lo, w_hi, preferred_element_type=F32)
         + jnp.dot(h_hi, w3_ref[1], preferred_element_type=F32))
    dec = dec_ref[...]
    h = h * jnp.concatenate([dec] * (2 * N_FILT), axis=1)
    row = pl.program_id(0) * tl + lax.broadcasted_iota(jnp.int32, h.shape, 0)
    lane = lax.broadcasted_iota(jnp.int32, h.shape, 1)
    o_ref[...] = jnp.where((row == 0) & (lane >= N_FILT * D_HY), 0.0, h)


def _filter_call(z, w1, b1, fr, w2, b2, w3, dec, tl):
    L = z.shape[0]
    return pl.pallas_call(
        functools.partial(_filter_kernel, tl=tl),
        grid=(L // tl,),
        in_specs=[pl.BlockSpec((tl, LANES), lambda i: (i, 0)),
                  _const_spec(w1.shape), _const_spec(b1.shape), _const_spec(fr.shape),
                  _const_spec(w2.shape), _const_spec(b2.shape), _const_spec(w3.shape),
                  pl.BlockSpec((tl, D_HY), lambda i: (i, 0))],
        out_specs=pl.BlockSpec((tl, 2 * N_FILT * D_HY), lambda i: (i, 0)),
        out_shape=jax.ShapeDtypeStruct((L, 2 * N_FILT * D_HY), F32),
        compiler_params=_cparams(("arbitrary",)),
        name="hyena_filter_mlp",
    )(z, w1, b1, fr, w2, b2, w3, dec)


def _spec_kernel(sf_ref, sb_ref, t_ref, kf_ref, kfn_ref, *, jb):
    step = pl.program_id(1)
    cb = sf_ref.shape[-1]

    def combine(xf, xb):
        return jnp.concatenate([xf[:FFT_N2] + xb[:FFT_N2], xf[FFT_N2:] - xb[FFT_N2:]], axis=0)

    def generic(jj):
        xf = _bdot(t_ref[jj], sf_ref[0, 2 * jj:2 * jj + 2].reshape(2 * FFT_N2, cb))
        xb = _bdot(t_ref[jj], sb_ref[0, 2 * jj:2 * jj + 2].reshape(2 * FFT_N2, cb))
        kf_ref[:, jj] = combine(xf, xb).reshape(2, FFT_N2, cb).astype(kf_ref.dtype)

    def special():
        row = lax.broadcasted_iota(jnp.int32, (2 * FFT_N2, 1), 0)
        lo = row < FFT_N2
        sf = sf_ref[0, 0:2].reshape(2 * FFT_N2, cb)
        sb = sb_ref[0, 0:2].reshape(2 * FFT_N2, cb)
        k0 = combine(_bdot(t_ref[0], jnp.where(lo, sf, 0.0)), _bdot(t_ref[0], jnp.where(lo, sb, 0.0)))
        kn = combine(_bdot(t_ref[0], jnp.where(lo, 0.0, sf)), _bdot(t_ref[0], jnp.where(lo, 0.0, sb)))
        kf_ref[:, 0] = k0.reshape(2, FFT_N2, cb).astype(kf_ref.dtype)
        kfn_ref[:, 0] = kn.reshape(2, FFT_N2, cb).astype(kfn_ref.dtype)

    pl.when(step == 0)(special)
    pl.when(step != 0)(lambda: generic(0))
    for jj in range(1, jb):
        generic(jj)


def _spec_call(s, t_f, cb, jb):
    _, n1, _, c2 = s.shape
    npair = n1 // 2
    C = c2 // 2
    nc = C // cb
    return pl.pallas_call(
        functools.partial(_spec_kernel, jb=jb),
        grid=(nc, npair // jb),
        in_specs=[pl.BlockSpec((1, 2 * jb, FFT_N2, cb), lambda c, j: (0, j, 0, c)),
                  pl.BlockSpec((1, 2 * jb, FFT_N2, cb), lambda c, j: (0, j, 0, nc + c)),
                  pl.BlockSpec((jb, 2 * FFT_N2, 2 * FFT_N2), lambda c, j: (j, 0, 0))],
        out_specs=[pl.BlockSpec((2, jb, FFT_N2, cb), lambda c, j: (0, j, 0, c)),
                   pl.BlockSpec((2, 1, FFT_N2, cb), lambda c, j: (0, 0, 0, c))],
        out_shape=[jax.ShapeDtypeStruct((2, npair, FFT_N2, C), BF16),
                   jax.ShapeDtypeStruct((2, 1, FFT_N2, C), BF16)],
        compiler_params=_cparams(("arbitrary", "arbitrary")),
        name="hyena_filter_spectrum",
    )(s, s, t_f)


def _dft_tables(L):
    a_n = L // FFT_N2
    n1 = 2 * a_n
    n = 2 * L
    npair = n1 // 2
    two_pi = 2.0 * math.pi

    a = jnp.arange(a_n, dtype=jnp.int32)[None, :]
    slot = jnp.arange(n1, dtype=jnp.int32)[:, None]
    j = slot // 2
    ang = two_pi * ((a * j) % n1).astype(F32) / n1
    sign = jnp.where(a % 2 == 0, 1.0, -1.0)
    is_im = (slot % 2) == 1
    f1 = jnp.where(is_im, -jnp.sin(ang), jnp.cos(ang))
    f1 = jnp.where(slot == 0, 1.0, jnp.where(slot == 1, sign, f1))
    fb = jnp.where(is_im, -2.0 * jnp.sin(ang), 2.0 * jnp.cos(ang))
    fb = jnp.where(slot == 0, 1.0, jnp.where(slot == 1, sign, fb)) / n
    eye = jnp.eye(SUBLANES, dtype=F32)
    g_a = jnp.kron(f1, eye).astype(BF16)
    g_b = jnp.kron(fb.T, eye).astype(BF16)

    b = jnp.arange(FFT_N2, dtype=jnp.int32)[None, None, :]
    d = jnp.arange(FFT_N2, dtype=jnp.int32)[None, :, None]
    c = jnp.arange(npair + 1, dtype=jnp.int32)[:, None, None]
    phi = two_pi * ((b * (d * n1 + c)) % n).astype(F32) / n
    co, si = jnp.cos(phi), jnp.sin(phi)
    t_f = jnp.concatenate([jnp.concatenate([co, si], axis=2),
                           jnp.concatenate([-si, co], axis=2)], axis=1)[:npair]
    t0 = jnp.concatenate([jnp.concatenate([co[0], co[npair]], axis=1),
                          jnp.concatenate([-si[0], -si[npair]], axis=1)], axis=0)
    t_f = t_f.at[0].set(t0)
    t_i = jnp.swapaxes(t_f, 1, 2)
    return g_a, g_b, t_f.astype(BF16), t_i.astype(BF16)


def _filter_tables(L):
    t = jnp.linspace(0.0, 1.0, L, dtype=F32)[:, None]
    wpos = (2.0 * math.pi / L) * jnp.arange(L, dtype=F32)[:, None]
    bands = jnp.linspace(1e-4, N_BANDS - 1, N_BANDS, dtype=F32)[None, :]
    z = jnp.concatenate([t, jnp.cos(bands * wpos), -jnp.sin(bands * wpos)], axis=-1)
    z = jnp.pad(z, ((0, 0), (0, LANES - EMB_DIM)))
    max_decay = math.log(DECAY_TARGET) / DECAY_FAST
    min_decay = math.log(DECAY_TARGET) / DECAY_SLOW
    deltas = jnp.linspace(min_decay, max_decay, D_HY, dtype=F32)
    decay = jnp.exp(-t * jnp.abs(deltas))
    return z, decay


def _rope_tables(L):
    inv_freq = 1.0 / (ROPE_THETA ** (jnp.arange(0, ROPE_DIM, 2, dtype=F32) / ROPE_DIM))
    ang = jnp.arange(L, dtype=F32)[:, None] * inv_freq[None, :]
    cos, sin = jnp.cos(ang), jnp.sin(ang)
    one = jnp.ones((L, NOPE_DIM), F32)
    zero_n = jnp.zeros((L, NOPE_DIM), F32)
    zero_t = jnp.zeros((L, LANES - NOPE_DIM - ROPE_DIM), F32)
    cos_t = jnp.concatenate([one, cos, cos, zero_t], axis=1)
    sin_t = jnp.concatenate([zero_n, -sin, sin, zero_t], axis=1)
    return cos_t, sin_t


def _swap_halves(w):
    half = w.shape[-1] // 2
    return jnp.concatenate([w[..., half:], w[..., :half]], axis=-1)


def _place(w, width, offset):
    return jnp.pad(w, ((0, 0), (offset, width - offset - w.shape[1])))


def _even_weights(w_in, w_uq, w_ukv):
    why = w_in[:, :HY_COLS].astype(BF16)
    w_kr = w_in[:, HY_COLS + Q_LORA + KV_LORA:]
    wlat = jnp.concatenate([
        w_in[:, HY_COLS:HY_COLS + Q_LORA + KV_LORA],
        _place(w_kr, LANES, NOPE_DIM),
        _place(_swap_halves(w_kr), LANES, NOPE_DIM)], axis=1).astype(BF16)
    qh = w_uq.reshape(Q_LORA, N_HEADS, NOPE_DIM + ROPE_DIM)
    q_main = jnp.pad(qh, ((0, 0), (0, 0), (0, LANES - NOPE_DIM - ROPE_DIM)))
    q_sw = jnp.pad(_swap_halves(qh[..., NOPE_DIM:]),
                   ((0, 0), (0, 0), (NOPE_DIM, LANES - NOPE_DIM - ROPE_DIM)))
    wq = jnp.concatenate([q_main.reshape(Q_LORA, -1), q_sw.reshape(Q_LORA, -1)], axis=1).astype(BF16)
    kvh = w_ukv.reshape(KV_LORA, N_HEADS, NOPE_DIM + V_DIM)
    k_main = jnp.pad(kvh[..., :NOPE_DIM], ((0, 0), (0, 0), (0, LANES - NOPE_DIM)))
    wk = k_main.reshape(KV_LORA, -1).astype(BF16)
    v_rows = jnp.pad(kvh[..., NOPE_DIM:], ((0, 0), (0, 0), (0, VT_ROWS - V_DIM)))
    wvt = v_rows.reshape(KV_LORA, -1).T.astype(BF16)
    vone = (jnp.arange(N_HEADS * VT_ROWS) % VT_ROWS == V_DIM).astype(F32)[:, None]
    return why, wlat, wq, wk, wvt, vone


def _tile(L, want):
    return min(want, L)


def _hyena(x1, x2, v, kf, kfn, hy_bias, tabs, jb):
    z = v
    for f, gate in enumerate((x1, x2)):
        z = _conv_call(z, gate, hy_bias[f][None, :], kf, kfn, f, tabs, jb)
    return z


def _filter_spectrum(L, w1, b1, fr, w2, b2, w3, tabs, ftabs, cb, jb):
    g_a, _, t_f, _ = tabs
    z, decay = ftabs
    pad_h = LANES - FILT_HID
    w1p = jnp.pad(w1, ((0, LANES - EMB_DIM), (0, pad_h)))
    w2p = jnp.pad(w2, ((0, pad_h), (0, pad_h)))
    w3p = jnp.pad(w3, ((0, pad_h), (0, 0)))
    w3_hi = w3p.astype(BF16)
    w3p = jnp.stack([w3_hi, (w3p - w3_hi.astype(F32)).astype(BF16)])
    pad1 = lambda a: jnp.pad(a, (0, pad_h))[None, :]
    hf = _filter_call(z, w1p, pad1(b1), pad1(fr), w2p, pad1(b2), w3p, decay, _tile(L, 512))
    s = _coarse_fwd_call(hf[None], g_a, cb)
    return _spec_call(s, t_f, cb, jb)


def _trunk(x, mix_norm, w_in, hy_conv_w, hy_conv_b, hf_w1, hf_b1, hf_freq, hf_w2, hf_b2, hf_w3, hy_bias,
           q_norm, w_uq, kv_norm, w_ukv, w_out, pool_w, pool_scale, mlp_norm, mlp_w1, mlp_w2, final_norm):
    B, L, _ = x.shape
    depth = mix_norm.shape[0]
    tm = _tile(L, 512)
    cb = 512
    jb = min(4, L // FFT_N2)
    tabs = _dft_tables(L)
    ftabs = _filter_tables(L)
    rope_tabs = _rope_tables(L)
    row = lambda a: a[None, :]
    for i in range(depth):
        w1 = mlp_w1[i].astype(BF16)
        w2 = mlp_w2[i].astype(BF16)
        g = row(mlp_norm[i])
        if i % 2 == 0:
            e = i // 2
            why, wlat, wq, wk, wvt, vone = _even_weights(w_in[e], w_uq[e], w_ukv[e])
            x1, x2, v, q, k, vt = _in_call(
                x, row(mix_norm[i]), why, wlat, hy_conv_w[e], row(hy_conv_b[e]),
                row(q_norm[e]), wq, row(kv_norm[e]), wk, wvt, vone, rope_tabs, tm)
            kf, kfn = _filter_spectrum(L, hf_w1[e], hf_b1[e], hf_freq[e], hf_w2[e], hf_b2[e], hf_w3[e],
                                       tabs, ftabs, cb, jb)
            y_hy = _hyena(x1, x2, v, kf, kfn, hy_bias[e], tabs, min(8, L // FFT_N2))
            y_att = _attn_call(q, k, vt, _tile(L, 1024))
            x = _out_call(x, y_hy, y_att, w_out[e].astype(BF16), g, w1, w2, tm)
        else:
            o = i // 2
            x = _odd_call(x, row(mix_norm[i]), pool_w[o].astype(BF16), row(pool_scale[o]), g, w1, w2,
                          row(final_norm), tm, final=(i == depth - 1))
    if depth % 2 == 1:
        raise NotImplementedError("final norm is fused into the last odd layer")
    return x


def kernel(x_prompt, x_sample, mix_norm, w_in, hy_conv_w, hy_conv_b, hf_w1, hf_b1, hf_freq, hf_w2, hf_b2,
           hf_w3, hy_bias, q_norm, w_uq, kv_norm, w_ukv, w_out, pool_w, pool_scale, mlp_norm, mlp_w1,
           mlp_w2, final_norm):
    weights = (mix_norm, w_in, hy_conv_w, hy_conv_b, hf_w1, hf_b1, hf_freq, hf_w2, hf_b2, hf_w3, hy_bias,
               q_norm, w_uq, kv_norm, w_ukv, w_out, pool_w, pool_scale, mlp_norm, mlp_w1, mlp_w2, final_norm)
    return (_trunk(x_prompt, *weights), _trunk(x_sample, *weights))
```

```python
import functools
import math

import jax
import jax.numpy as jnp
from jax import lax
from jax.experimental import pallas as pl
from jax.experimental.pallas import tpu as pltpu

F32 = jnp.float32
BF16 = jnp.bfloat16

D_MODEL = 1024
D_HY = 512
N_FILT = 2
HY_COLS = (N_FILT + 1) * D_HY
EMB_DIM = 33
N_BANDS = (EMB_DIM - 1) // 2
FILT_HID = 64
DECAY_FAST = 0.3
DECAY_SLOW = 1.5
DECAY_TARGET = 1e-2
N_HEADS = 8
NOPE_DIM = 64
ROPE_DIM = 32
V_DIM = 64
Q_LORA = 384
KV_LORA = 256
ROPE_THETA = 10000.0
POOL_WINDOWS = (2, 4, 8, 16)
POOL_GRP = D_MODEL // len(POOL_WINDOWS)
D_FF = 4 * D_MODEL
EPS = 1e-6
ATT_SCALE = (NOPE_DIM + ROPE_DIM) ** -0.5
Q_SCALE = ATT_SCALE * math.log2(math.e)
VT_ROWS = 80
HY_CB = 256
HY_RG = 4
HY_JB = 16
MLP_PARTS = 1
ATT_SPLIT = 1
ATT_UNROLL = 4
ATT_MIN_GROUPS = 4

LANES = 128
SUBLANES = 8
HALO = 16
FFT_N2 = 128
VMEM_LIMIT = 56 * 1024 * 1024


def _cparams(sem):
    return pltpu.CompilerParams(dimension_semantics=sem, vmem_limit_bytes=VMEM_LIMIT)


def _const_spec(shape):
    nd = len(shape)
    return pl.BlockSpec(shape, lambda *_: (0,) * nd, pipeline_mode=pl.Buffered(1))


def _rms(xf, g):
    ms = jnp.mean(xf * xf, axis=-1, keepdims=True)
    return xf * lax.rsqrt(ms + EPS) * g


def _bdot(a, b):
    return jnp.dot(a.astype(BF16), b.astype(BF16), preferred_element_type=F32)


def _in_kernel(xp_ref, xc_ref, xn_ref, g_ref, why_ref, wlat_ref, cw_ref, cb_ref,
               qn_ref, wq_ref, kvn_ref, wk_ref, wvt_ref, vone_ref, cos_ref, sin_ref, cost_ref, sint_ref,
               x1_ref, x2_ref, v_ref, q_ref, k_ref, vt_ref, *, tm, n_tiles):
    i = pl.program_id(1)
    g = g_ref[...]
    hc = _rms(xc_ref[0], g)
    hp = _rms(xp_ref[0], g) * jnp.where(i > 0, 1.0, 0.0)
    hn = _rms(xn_ref[0], g) * jnp.where(i < n_tiles - 1, 1.0, 0.0)
    hcb = hc.astype(BF16)
    h_ext = jnp.concatenate([hp.astype(BF16), hcb, hn.astype(BF16)], axis=0)
    p = jnp.dot(h_ext, why_ref[...], preferred_element_type=F32)
    n = tm + 2 * HALO
    cw = cw_ref[...]
    u = (pltpu.roll(p, 1, 0)[HALO:HALO + tm] * cw[0:1]
         + p[HALO:HALO + tm] * cw[1:2]
         + pltpu.roll(p, n - 1, 0)[HALO:HALO + tm] * cw[2:3]
         + cb_ref[...])
    for i_out, ref in enumerate((x1_ref, x2_ref, v_ref)):
        for c in range(D_HY // HY_CB):
            lo = i_out * D_HY + c * HY_CB
            ref[0, c] = u[:, lo:lo + HY_CB].astype(ref.dtype)

    lat = jnp.dot(hcb, wlat_ref[...], preferred_element_type=F32)
    cq = lat[:, :Q_LORA]
    ckv = lat[:, Q_LORA:Q_LORA + KV_LORA]
    kr = lat[:, Q_LORA + KV_LORA:Q_LORA + KV_LORA + LANES]
    krs = lat[:, Q_LORA + KV_LORA + LANES:]
    q2_t = lax.dot_general(wq_ref[...], _rms(cq, qn_ref[...]).astype(BF16), (((1,), (1,)), ((), ())),
                           preferred_element_type=F32)
    ckvn = _rms(ckv, kvn_ref[...]).astype(BF16)
    k2 = jnp.dot(ckvn, wk_ref[...], preferred_element_type=F32)
    v_t = lax.dot_general(wvt_ref[...], ckvn, (((1,), (1,)), ((), ())),
                          preferred_element_type=F32) + vone_ref[...]
    hw = N_HEADS * LANES
    kr_r = kr * cos_ref[...] + krs * sin_ref[...]
    cos_tt = cost_ref[...]
    sin_tt = sint_ref[...]
    for h in range(N_HEADS):
        sl = slice(h * LANES, (h + 1) * LANES)
        qh = (q2_t[sl] * cos_tt + q2_t[hw + h * LANES:hw + (h + 1) * LANES] * sin_tt) * Q_SCALE
        q_ref[0, h, 0] = qh.astype(BF16)
        k_ref[0, h] = (k2[:, sl] + kr_r).astype(BF16)
        vt_ref[0, h, 0] = v_t[h * VT_ROWS:(h + 1) * VT_ROWS, :].astype(BF16)


def _in_call(x, g, why, wlat, cw, cb, qn, wq, kvn, wk, wvt, vone, rope_tabs, tm):
    B, L, _ = x.shape
    n_tiles = L // tm
    hb = tm // HALO
    n_hb = L // HALO
    tok = lambda b, i: (b, i, 0)
    in_specs = [
        pl.BlockSpec((1, HALO, D_MODEL), lambda b, i: (b, jnp.maximum(i * hb - 1, 0), 0)),
        pl.BlockSpec((1, tm, D_MODEL), tok),
        pl.BlockSpec((1, HALO, D_MODEL), lambda b, i: (b, jnp.minimum((i + 1) * hb, n_hb - 1), 0)),
        _const_spec(g.shape), _const_spec(why.shape), _const_spec(wlat.shape),
        _const_spec(cw.shape), _const_spec(cb.shape), _const_spec(qn.shape),
        _const_spec(wq.shape), _const_spec(kvn.shape), _const_spec(wk.shape),
        _const_spec(wvt.shape), _const_spec(vone.shape),
    ] + [pl.BlockSpec((tm, LANES), lambda b, i: (i, 0)) for _ in rope_tabs[:2]
         ] + [pl.BlockSpec((LANES, tm), lambda b, i: (0, i)) for _ in rope_tabs[2:]]
    hy = jax.ShapeDtypeStruct((B, D_HY // HY_CB, L, HY_CB), BF16)
    hd = jax.ShapeDtypeStruct((B, N_HEADS, L, LANES), BF16)
    hy_spec = pl.BlockSpec((1, D_HY // HY_CB, tm, HY_CB), lambda b, i: (b, 0, i, 0))
    hd_spec = pl.BlockSpec((1, N_HEADS, tm, LANES), lambda b, i: (b, 0, i, 0))
    qt = jax.ShapeDtypeStruct((B, N_HEADS, n_tiles, LANES, tm), BF16)
    qt_spec = pl.BlockSpec((1, N_HEADS, 1, LANES, tm), lambda b, i: (b, 0, i, 0, 0))
    vt = jax.ShapeDtypeStruct((B, N_HEADS, n_tiles, VT_ROWS, tm), BF16)
    vt_spec = pl.BlockSpec((1, N_HEADS, 1, VT_ROWS, tm), lambda b, i: (b, 0, i, 0, 0))
    return pl.pallas_call(
        functools.partial(_in_kernel, tm=tm, n_tiles=n_tiles),
        grid=(B, n_tiles),
        in_specs=in_specs,
        out_specs=[hy_spec, hy_spec, hy_spec, qt_spec, hd_spec, vt_spec],
        out_shape=[hy, hy, hy, qt, hd, vt],
        compiler_params=_cparams(("parallel", "arbitrary")),
        name="even_in",
    )(x, x, x, g, why, wlat, cw, cb, qn, wq, kvn, wk, wvt, vone, *rope_tabs)


def _attn_kernel(q_ref, k_ref, vt_ref, o_ref, sa_ref, sb_ref, *, tq, tk, n_chunks):
    qs = [jnp.concatenate([q_ref[0, hh, c] for c in range(q_ref.shape[2])], axis=1) for hh in range(2)]

    th = tk // ATT_SPLIT

    def scores(hh, j, s_ref, r):
        start = pl.multiple_of(j * tk + r * th, th)
        kc = k_ref[0, hh, pl.ds(start, th), :]
        s = jnp.dot(kc, qs[hh], preferred_element_type=F32)
        s_ref[hh, r * th:(r + 1) * th, :] = s
        return jnp.max(s, axis=0, keepdims=True)

    def step(j, src_ref, dst_ref, state, cms, fetch):
        new_state, new_cms = [], []
        for hh in range(2):
            m, acc = state[2 * hh], state[2 * hh + 1]
            m_new = jnp.maximum(m, cms[hh])
            acc = acc * jnp.exp2(m - m_new)
            cm_next = None
            for r in range(ATT_SPLIT):
                if fetch:
                    cm_r = scores(hh, j + 1, dst_ref, r)
                    cm_next = cm_r if cm_next is None else jnp.maximum(cm_next, cm_r)
                p = jnp.exp2(src_ref[hh, r * th:(r + 1) * th, :] - m_new).astype(BF16)
                acc = acc + jnp.dot(vt_ref[0, hh, j, :, r * th:(r + 1) * th], p,
                                    preferred_element_type=F32)
            new_state += [m_new, acc]
            if fetch:
                new_cms.append(cm_next)
        return new_state, new_cms

    def run_chunks(j, state, cms, last):
        for c in range(unroll):
            src, dst = (sa_ref, sb_ref) if c % 2 == 0 else (sb_ref, sa_ref)
            state, cms = step(j + c, src, dst, state, cms, not (last and c == unroll - 1))
        return state, cms

    m0 = jnp.full((1, tq), -1e30, F32)
    acc0 = jnp.zeros((VT_ROWS, tq), F32)
    state = [m0, acc0, m0, acc0]
    cms = [functools.reduce(jnp.maximum, [scores(hh, 0, sa_ref, r) for r in range(ATT_SPLIT)])
           for hh in range(2)]
    if n_chunks == 1:
        res, _ = step(0, sa_ref, sb_ref, state, cms, False)
    else:
        unroll = ATT_UNROLL if (n_chunks % ATT_UNROLL == 0 and n_chunks >= ATT_MIN_GROUPS * ATT_UNROLL) else 2

        def body(i, carry):
            st, cm = run_chunks(unroll * i, list(carry[:4]), list(carry[4:]), last=False)
            return tuple(st) + tuple(cm)

        carry = lax.fori_loop(0, n_chunks // unroll - 1, body, tuple(state) + tuple(cms))
        res, _ = run_chunks(n_chunks - unroll, list(carry[:4]), list(carry[4:]), last=True)
    o_t = jnp.concatenate([res[1][:V_DIM] / res[1][V_DIM:V_DIM + 1],
                           res[3][:V_DIM] / res[3][V_DIM:V_DIM + 1]], axis=0)
    o_ref[0] = o_t.T


def _attn_call(qt, k, vt, tq):
    B, _, L, _ = k.shape
    _, _, n_chunks, _, tk = vt.shape
    assert n_chunks == 1 or n_chunks % 2 == 0, n_chunks
    assert qt.shape == (B, N_HEADS, n_chunks, LANES, tk) and tq % tk == 0
    return pl.pallas_call(
        functools.partial(_attn_kernel, tq=tq, tk=tk, n_chunks=n_chunks),
        grid=(B, N_HEADS // 2, L // tq),
        in_specs=[pl.BlockSpec((1, 2, tq // tk, LANES, tk), lambda b, hp, i: (b, hp, i, 0, 0)),
                  pl.BlockSpec((1, 2, L, LANES), lambda b, hp, i: (b, hp, 0, 0)),
                  pl.BlockSpec((1, 2, n_chunks, VT_ROWS, tk), lambda b, hp, i: (b, hp, 0, 0, 0))],
        out_specs=pl.BlockSpec((1, tq, LANES), lambda b, hp, i: (b, i, hp)),
        out_shape=jax.ShapeDtypeStruct((B, L, N_HEADS * V_DIM), F32),
        scratch_shapes=[pltpu.VMEM((2, tk, tq), F32), pltpu.VMEM((2, tk, tq), F32)],
        compiler_params=_cparams(("parallel", "parallel", "arbitrary")),
        name="attention",
    )(qt, k, vt)


def _row_parts(tm):
    part = tm // MLP_PARTS if tm % (MLP_PARTS * HALO) == 0 else tm
    return [slice(r, r + part) for r in range(0, tm, part)]


def _mlp(x, g_ref, w1_ref, w2_ref):
    h = _rms(x, g_ref[...]).astype(BF16)
    acc = x
    chunk = D_MODEL
    for c in range(D_FF // chunk):
        hid = jnp.dot(h, w1_ref[:, c * chunk:(c + 1) * chunk], preferred_element_type=F32)
        hid = jnp.square(jnp.maximum(hid, 0.0)).astype(BF16)
        acc = acc + jnp.dot(hid, w2_ref[c * chunk:(c + 1) * chunk, :], preferred_element_type=F32)
    return acc


def _out_kernel(x_ref, yh_ref, ya_ref, wo_ref, g_ref, w1_ref, w2_ref, o_ref):
    tm = x_ref.shape[1]
    for rows in _row_parts(tm):
        y = jnp.concatenate([yh_ref[0, c, rows, :] for c in range(D_HY // HY_CB)]
                            + [ya_ref[0, rows, :].astype(BF16)], axis=1)
        x = x_ref[0, rows, :] + jnp.dot(y, wo_ref[...], preferred_element_type=F32)
        o_ref[0, rows, :] = _mlp(x, g_ref, w1_ref, w2_ref)


def _out_call(x, yh, ya, wo, g, w1, w2, tm):
    B, L, _ = x.shape
    tok = lambda b, i: (b, i, 0)
    return pl.pallas_call(
        _out_kernel,
        grid=(B, L // tm),
        in_specs=[pl.BlockSpec((1, tm, D_MODEL), tok),
                  pl.BlockSpec((1, D_HY // HY_CB, tm, HY_CB), lambda b, i: (b, 0, i, 0)),
                  pl.BlockSpec((1, tm, D_HY), tok), _const_spec(wo.shape), _const_spec(g.shape),
                  _const_spec(w1.shape), _const_spec(w2.shape)],
        out_specs=pl.BlockSpec((1, tm, D_MODEL), tok),
        out_shape=jax.ShapeDtypeStruct(x.shape, F32),
        compiler_params=_cparams(("parallel", "arbitrary")),
        name="even_out_mlp",
    )(x, yh, ya, wo, g, w1, w2)


def _odd_kernel(xp_ref, xc_ref, xn_ref, gm_ref, pw_ref, ps_ref, g_ref, w1_ref, w2_ref, fn_ref,
                o_ref, *, tm, n_tiles, seq, final):
    i = pl.program_id(1)
    gm = gm_ref[...]
    x = xc_ref[0]
    hp = _rms(xp_ref[0], gm) * jnp.where(i > 0, 1.0, 0.0)
    hn = _rms(xn_ref[0], gm) * jnp.where(i < n_tiles - 1, 1.0, 0.0)
    hc = _rms(x, gm)
    e = jnp.concatenate([hp, hc, hn], axis=0)
    ps = ps_ref[...]
    for rows in _row_parts(tm):
        part = rows.stop - rows.start
        n = part + 2 * HALO
        ep = e[rows.start:rows.start + n]
        t = i * tm + rows.start + lax.broadcasted_iota(jnp.int32, (part, 1), 0)
        mixed = []
        for gi, w in enumerate(POOL_WINDOWS):
            sl = slice(gi * POOL_GRP, (gi + 1) * POOL_GRP)
            eg = ep[:, sl]
            s = pltpu.roll(eg, 1, 0) + eg
            half = 1
            while 2 * half < w:
                s = pltpu.roll(s, half, 0) + pltpu.roll(s, n - half, 0)
                half *= 2
            cnt = (jnp.minimum(t + w // 2, seq) - jnp.maximum(t - w // 2, 0)).astype(F32)
            d = s[HALO:HALO + part] / cnt - eg[HALO:HALO + part]
            mixed.append(_bdot(d, pw_ref[gi]) * ps[:, sl])
        y = _mlp(x[rows] + jnp.concatenate(mixed, axis=1), g_ref, w1_ref, w2_ref)
        if final:
            y = _rms(y, fn_ref[...])
        o_ref[0, rows, :] = y


def _odd_call(x, gm, pw, ps, g, w1, w2, fn, tm, final):
    B, L, _ = x.shape
    n_tiles = L // tm
    hb = tm // HALO
    n_hb = L // HALO
    tok = lambda b, i: (b, i, 0)
    return pl.pallas_call(
        functools.partial(_odd_kernel, tm=tm, n_tiles=n_tiles, seq=L, final=final),
        grid=(B, n_tiles),
        in_specs=[
            pl.BlockSpec((1, HALO, D_MODEL), lambda b, i: (b, jnp.maximum(i * hb - 1, 0), 0)),
            pl.BlockSpec((1, tm, D_MODEL), tok),
            pl.BlockSpec((1, HALO, D_MODEL), lambda b, i: (b, jnp.minimum((i + 1) * hb, n_hb - 1), 0)),
            _const_spec(gm.shape), _const_spec(pw.shape), _const_spec(ps.shape), _const_spec(g.shape),
            _const_spec(w1.shape), _const_spec(w2.shape), _const_spec(fn.shape)],
        out_specs=pl.BlockSpec((1, tm, D_MODEL), tok),
        out_shape=jax.ShapeDtypeStruct(x.shape, F32),
        compiler_params=_cparams(("parallel", "arbitrary")),
        name="odd_pool_mlp",
    )(x, x, x, gm, pw, ps, g, w1, w2, fn)


def _coarse_fwd_kernel(x_ref, g_ref, o_ref):
    x = x_ref[0, :, 0]
    a, _, cb = x.shape
    xb = x.reshape(a * SUBLANES, cb).astype(BF16)
    y = jnp.dot(g_ref[...], xb, preferred_element_type=F32)
    o_ref[0, :, 0] = y.reshape(y.shape[0] // SUBLANES, SUBLANES, cb)


def _coarse_fwd_call(x, g_a, cb):
    B, L, C = x.shape
    a = L // FFT_N2
    n1 = 2 * a
    nb = FFT_N2 // SUBLANES
    x5 = x.reshape(B, a, nb, SUBLANES, C)
    out = pl.pallas_call(
        _coarse_fwd_kernel,
        grid=(B, nb, C // cb),
        in_specs=[pl.BlockSpec((1, a, 1, SUBLANES, cb), lambda b, r, c: (b, 0, r, 0, c)),
                  _const_spec(g_a.shape)],
        out_specs=pl.BlockSpec((1, n1, 1, SUBLANES, cb), lambda b, r, c: (b, 0, r, 0, c)),
        out_shape=jax.ShapeDtypeStruct((B, n1, nb, SUBLANES, C), F32),
        compiler_params=_cparams(("parallel", "parallel", "arbitrary")),
        name="hyena_coarse_fwd",
    )(x5, g_a)
    return out.reshape(B, n1, FFT_N2, C)


def _cmul(x, kr, ki):
    xr, xi = x[:FFT_N2], x[FFT_N2:]
    return jnp.concatenate([xr * kr - xi * ki, xr * ki + xi * kr], axis=0)


def _conv_kernel(xf_ref, ga_ref, t_ref, ta_ref, kf_ref, kfn_ref, gb_ref, u_ref, gate_ref, bias_ref,
                 o_ref, s_ref, *, nb, nj, jb):
    t = pl.program_id(2)
    n1, _, cb = s_ref.shape

    def rows(step, g):
        return pl.ds(pl.multiple_of((step * HY_RG + g) * SUBLANES, SUBLANES), SUBLANES)

    @pl.when(t < nb)
    def _():
        x_all = xf_ref[0, 0, :, 0].astype(F32)
        for g in range(HY_RG):
            x = x_all[:, g * SUBLANES:(g + 1) * SUBLANES, :]
            xb = x.reshape(x.shape[0] * SUBLANES, cb).astype(BF16)
            y = jnp.dot(ga_ref[...], xb, preferred_element_type=F32)
            s_ref[:, rows(t, g), :] = y.reshape(n1, SUBLANES, cb)

    def pair(jj):
        return (t - nb) * jb + jj

    def load_pair(jj):
        return s_ref[pl.ds(2 * pair(jj), 2)].reshape(2 * FFT_N2, cb)

    def store_pair(jj, z):
        s_ref[pl.ds(2 * pair(jj), 2)] = z.reshape(2, FFT_N2, cb)

    def fine(jjs):
        xs = [_bdot(t_ref[pair(jj)], load_pair(jj)) for jj in jjs]
        ys = [_cmul(x, kf_ref[0, pair(jj)], kf_ref[1, pair(jj)]) for jj, x in zip(jjs, xs)]
        for jj, y in zip(jjs, ys):
            store_pair(jj, _bdot(ta_ref[pair(jj)], y))

    def fine_first_pair():
        s = load_pair(0)
        lo = lax.broadcasted_iota(jnp.int32, (2 * FFT_N2, 1), 0) < FFT_N2
        xa = _bdot(t_ref[0], jnp.where(lo, s, 0.0))
        xb = _bdot(t_ref[0], jnp.where(lo, 0.0, s))
        ya = _cmul(xa, kf_ref[0, 0], kf_ref[1, 0])
        yb = _cmul(xb, kfn_ref[0, 0], kfn_ref[1, 0])
        store_pair(0, jnp.where(lo, _bdot(ta_ref[0], ya), _bdot(ta_ref[0], yb)))

    @pl.when(t == nb)
    def _():
        fine_first_pair()
        fine(list(range(1, jb)))

    @pl.when((t > nb) & (t < nb + nj))
    def _():
        fine(list(range(jb)))

    @pl.when(t >= nb + nj)
    def _():
        u_all = u_ref[0, 0, :, 0].astype(F32)
        gate_all = gate_ref[0, 0, :, 0].astype(F32)
        outs = []
        for g in range(HY_RG):
            sub = slice(g * SUBLANES, (g + 1) * SUBLANES)
            z = s_ref[:, rows(t - nb - nj, g), :]
            y = jnp.dot(gb_ref[...], z.reshape(n1 * SUBLANES, cb).astype(BF16), preferred_element_type=F32)
            y = y.reshape(y.shape[0] // SUBLANES, SUBLANES, cb)
            outs.append((y + u_all[:, sub, :] * bias_ref[...]) * gate_all[:, sub, :])
        o_ref[0, 0, :, 0] = jnp.concatenate(outs, axis=1).astype(o_ref.dtype)


def _conv_call(u, gate, bias, kf, kfn, filt, tabs, jb):
    g_a, g_b, t_f, t_i = tabs
    B, nc, L, cb = u.shape
    a = L // FFT_N2
    n1 = 2 * a
    rg_rows = HY_RG * SUBLANES
    assert u.dtype == BF16 and rg_rows % (2 * SUBLANES) == 0
    nb = FFT_N2 // rg_rows
    nj = (n1 // 2) // jb
    koff = filt * nc
    u5 = u.reshape(B, nc, a, nb, rg_rows, cb)
    fwd_r = lambda c, b, t: (b, c, 0, jnp.minimum(t, nb - 1), 0, 0)
    inv_r = lambda c, b, t: (b, c, 0, jnp.clip(t - nb - nj, 0, nb - 1), 0, 0)
    tok = (1, 1, a, 1, rg_rows, cb)
    kspec = lambda n: pl.BlockSpec((2, n, FFT_N2, cb), lambda c, b, t: (0, 0, 0, koff + c),
                                   pipeline_mode=pl.Buffered(1))
    out = pl.pallas_call(
        functools.partial(_conv_kernel, nb=nb, nj=nj, jb=jb),
        grid=(nc, B, 2 * nb + nj),
        in_specs=[pl.BlockSpec(tok, fwd_r), _const_spec(g_a.shape), _const_spec(t_f.shape),
                  _const_spec(t_i.shape), kspec(n1 // 2), kspec(1),
                  _const_spec(g_b.shape), pl.BlockSpec(tok, inv_r), pl.BlockSpec(tok, inv_r),
                  pl.BlockSpec((1, cb), lambda c, b, t: (0, c))],
        out_specs=pl.BlockSpec(tok, inv_r),
        out_shape=jax.ShapeDtypeStruct(u5.shape, u.dtype),
        scratch_shapes=[pltpu.VMEM((n1, FFT_N2, cb), F32)],
        compiler_params=_cparams(("arbitrary", "arbitrary", "arbitrary")),
        name="hyena_conv",
    )(u5, g_a, t_f, t_i, kf, kfn, g_b, u5, gate.reshape(u5.shape), bias)
    return out.reshape(u.shape)


def _filter_kernel(z_ref, w1_ref, b1_ref, fr_ref, w2_ref, b2_ref, w3_ref, dec_ref, o_ref, *, tl):
    hi = lax.Precision.HIGHEST
    fr = fr_ref[...]
    h = jnp.sin(fr * (jnp.dot(z_ref[...], w1_ref[...], precision=hi, preferred_element_type=F32) + b1_ref[...]))
    h = jnp.sin(fr * (jnp.dot(h, w2_ref[...], precision=hi, preferred_element_type=F32) + b2_ref[...]))
    h_hi = h.astype(BF16)
    h_lo = (h - h_hi.astype(F32)).astype(BF16)
    w_hi = w3_ref[0]
    h = (jnp.dot(h_hi, w_hi, preferred_element_type=F32) + jnp.dot(h_lo, w_hi, preferred_element_type=F32)
         + jnp.dot(h_hi, w3_ref[1], preferred_element_type=F32))
    dec = dec_ref[...]
    h = h * jnp.concatenate([dec] * (2 * N_FILT), axis=1)
    row = pl.program_id(0) * tl + lax.broadcasted_iota(jnp.int32, h.shape, 0)
    lane = lax.broadcasted_iota(jnp.int32, h.shape, 1)
    o_ref[...] = jnp.where((row == 0) & (lane >= N_FILT * D_HY), 0.0, h)


def _filter_call(z, w1, b1, fr, w2, b2, w3, dec, tl):
    L = z.shape[0]
    return pl.pallas_call(
        functools.partial(_filter_kernel, tl=tl),
        grid=(L // tl,),
        in_specs=[pl.BlockSpec((tl, LANES), lambda i: (i, 0)),
                  _const_spec(w1.shape), _const_spec(b1.shape), _const_spec(fr.shape),
                  _const_spec(w2.shape), _const_spec(b2.shape), _const_spec(w3.shape),
                  pl.BlockSpec((tl, D_HY), lambda i: (i, 0))],
        out_specs=pl.BlockSpec((tl, 2 * N_FILT * D_HY), lambda i: (i, 0)),
        out_shape=jax.ShapeDtypeStruct((L, 2 * N_FILT * D_HY), F32),
        compiler_params=_cparams(("arbitrary",)),
        name="hyena_filter_mlp",
    )(z, w1, b1, fr, w2, b2, w3, dec)


def _spec_kernel(sf_ref, sb_ref, t_ref, kf_ref, kfn_ref, *, jb):
    step = pl.program_id(1)
    cb = sf_ref.shape[-1]

    def combine(xf, xb):
        return jnp.concatenate([xf[:FFT_N2] + xb[:FFT_N2], xf[FFT_N2:] - xb[FFT_N2:]], axis=0)

    def generic(jj):
        xf = _bdot(t_ref[jj], sf_ref[0, 2 * jj:2 * jj + 2].reshape(2 * FFT_N2, cb))
        xb = _bdot(t_ref[jj], sb_ref[0, 2 * jj:2 * jj + 2].reshape(2 * FFT_N2, cb))
        kf_ref[:, jj] = combine(xf, xb).reshape(2, FFT_N2, cb).astype(kf_ref.dtype)

    def special():
        row = lax.broadcasted_iota(jnp.int32, (2 * FFT_N2, 1), 0)
        lo = row < FFT_N2
        sf = sf_ref[0, 0:2].reshape(2 * FFT_N2, cb)
        sb = sb_ref[0, 0:2].reshape(2 * FFT_N2, cb)
        k0 = combine(_bdot(t_ref[0], jnp.where(lo, sf, 0.0)), _bdot(t_ref[0], jnp.where(lo, sb, 0.0)))
        kn = combine(_bdot(t_ref[0], jnp.where(lo, 0.0, sf)), _bdot(t_ref[0], jnp.where(lo, 0.0, sb)))
        kf_ref[:, 0] = k0.reshape(2, FFT_N2, cb).astype(kf_ref.dtype)
        kfn_ref[:, 0] = kn.reshape(2, FFT_N2, cb).astype(kfn_ref.dtype)

    pl.when(step == 0)(special)
    pl.when(step != 0)(lambda: generic(0))
    for jj in range(1, jb):
        generic(jj)


def _spec_call(s, t_f, cb, jb):
    _, n1, _, c2 = s.shape
    npair = n1 // 2
    C = c2 // 2
    nc = C // cb
    return pl.pallas_call(
        functools.partial(_spec_kernel, jb=jb),
        grid=(nc, npair // jb),
        in_specs=[pl.BlockSpec((1, 2 * jb, FFT_N2, cb), lambda c, j: (0, j, 0, c)),
                  pl.BlockSpec((1, 2 * jb, FFT_N2, cb), lambda c, j: (0, j, 0, nc + c)),
                  pl.BlockSpec((jb, 2 * FFT_N2, 2 * FFT_N2), lambda c, j: (j, 0, 0))],
        out_specs=[pl.BlockSpec((2, jb, FFT_N2, cb), lambda c, j: (0, j, 0, c)),
                   pl.BlockSpec((2, 1, FFT_N2, cb), lambda c, j: (0, 0, 0, c))],
        out_shape=[jax.ShapeDtypeStruct((2, npair, FFT_N2, C), BF16),
                   jax.ShapeDtypeStruct((2, 1, FFT_N2, C), BF16)],
        compiler_params=_cparams(("arbitrary", "arbitrary")),
        name="hyena_filter_spectrum",
    )(s, s, t_f)


def _dft_tables(L):
    a_n = L // FFT_N2
    n1 = 2 * a_n
    n = 2 * L
    npair = n1 // 2
    two_pi = 2.0 * math.pi

    a = jnp.arange(a_n, dtype=jnp.int32)[None, :]
    slot = jnp.arange(n1, dtype=jnp.int32)[:, None]
    j = slot // 2
    ang = two_pi * ((a * j) % n1).astype(F32) / n1
    sign = jnp.where(a % 2 == 0, 1.0, -1.0)
    is_im = (slot % 2) == 1
    f1 = jnp.where(is_im, -jnp.sin(ang), jnp.cos(ang))
    f1 = jnp.where(slot == 0, 1.0, jnp.where(slot == 1, sign, f1))
    fb = jnp.where(is_im, -2.0 * jnp.sin(ang), 2.0 * jnp.cos(ang))
    fb = jnp.where(slot == 0, 1.0, jnp.where(slot == 1, sign, fb)) / n
    eye = jnp.eye(SUBLANES, dtype=F32)
    g_a = jnp.kron(f1, eye).astype(BF16)
    g_b = jnp.kron(fb.T, eye).astype(BF16)

    b = jnp.arange(FFT_N2, dtype=jnp.int32)[None, None, :]
    d = jnp.arange(FFT_N2, dtype=jnp.int32)[None, :, None]
    c = jnp.arange(npair + 1, dtype=jnp.int32)[:, None, None]
    phi = two_pi * ((b * (d * n1 + c)) % n).astype(F32) / n
    co, si = jnp.cos(phi), jnp.sin(phi)
    t_f = jnp.concatenate([jnp.concatenate([co, si], axis=2),
                           jnp.concatenate([-si, co], axis=2)], axis=1)[:npair]
    t0 = jnp.concatenate([jnp.concatenate([co[0], co[npair]], axis=1),
                          jnp.concatenate([-si[0], -si[npair]], axis=1)], axis=0)
    t_f = t_f.at[0].set(t0)
    t_i = jnp.swapaxes(t_f, 1, 2)
    return g_a, g_b, t_f.astype(BF16), t_i.astype(BF16)


def _filter_tables(L):
    t = jnp.linspace(0.0, 1.0, L, dtype=F32)[:, None]
    wpos = (2.0 * math.pi / L) * jnp.arange(L, dtype=F32)[:, None]
    bands = jnp.linspace(1e-4, N_BANDS - 1, N_BANDS, dtype=F32)[None, :]
    z = jnp.concatenate([t, jnp.cos(bands * wpos), -jnp.sin(bands * wpos)], axis=-1)
    z = jnp.pad(z, ((0, 0), (0, LANES - EMB_DIM)))
    max_decay = math.log(DECAY_TARGET) / DECAY_FAST
    min_decay = math.log(DECAY_TARGET) / DECAY_SLOW
    deltas = jnp.linspace(min_decay, max_decay, D_HY, dtype=F32)
    decay = jnp.exp(-t * jnp.abs(deltas))
    return z, decay


def _rope_tables(L):
    inv_freq = 1.0 / (ROPE_THETA ** (jnp.arange(0, ROPE_DIM, 2, dtype=F32) / ROPE_DIM))
    ang = jnp.arange(L, dtype=F32)[:, None] * inv_freq[None, :]
    cos, sin = jnp.cos(ang), jnp.sin(ang)
    one = jnp.ones((L, NOPE_DIM), F32)
    zero_n = jnp.zeros((L, NOPE_DIM), F32)
    zero_t = jnp.zeros((L, LANES - NOPE_DIM - ROPE_DIM), F32)
    cos_t = jnp.concatenate([one, cos, cos, zero_t], axis=1)
    sin_t = jnp.concatenate([zero_n, -sin, sin, zero_t], axis=1)
    return cos_t, sin_t, cos_t.T, sin_t.T


def _swap_halves(w):
    half = w.shape[-1] // 2
    return jnp.concatenate([w[..., half:], w[..., :half]], axis=-1)


def _place(w, width, offset):
    return jnp.pad(w, ((0, 0), (offset, width - offset - w.shape[1])))


def _even_weights(w_in, w_uq, w_ukv):
    why = w_in[:, :HY_COLS].astype(BF16)
    w_kr = w_in[:, HY_COLS + Q_LORA + KV_LORA:]
    wlat = jnp.concatenate([
        w_in[:, HY_COLS:HY_COLS + Q_LORA + KV_LORA],
        _place(w_kr, LANES, NOPE_DIM),
        _place(_swap_halves(w_kr), LANES, NOPE_DIM)], axis=1).astype(BF16)
    qh = w_uq.reshape(Q_LORA, N_HEADS, NOPE_DIM + ROPE_DIM)
    q_main = jnp.pad(qh, ((0, 0), (0, 0), (0, LANES - NOPE_DIM - ROPE_DIM)))
    q_sw = jnp.pad(_swap_halves(qh[..., NOPE_DIM:]),
                   ((0, 0), (0, 0), (NOPE_DIM, LANES - NOPE_DIM - ROPE_DIM)))
    wq = jnp.concatenate([q_main.reshape(Q_LORA, -1), q_sw.reshape(Q_LORA, -1)], axis=1).T.astype(BF16)
    kvh = w_ukv.reshape(KV_LORA, N_HEADS, NOPE_DIM + V_DIM)
    k_main = jnp.pad(kvh[..., :NOPE_DIM], ((0, 0), (0, 0), (0, LANES - NOPE_DIM)))
    wk = k_main.reshape(KV_LORA, -1).astype(BF16)
    v_rows = jnp.pad(kvh[..., NOPE_DIM:], ((0, 0), (0, 0), (0, VT_ROWS - V_DIM)))
    wvt = v_rows.reshape(KV_LORA, -1).T.astype(BF16)
    vone = (jnp.arange(N_HEADS * VT_ROWS) % VT_ROWS == V_DIM).astype(F32)[:, None]
    return why, wlat, wq, wk, wvt, vone


def _tile(L, want):
    return min(want, L)


def _hyena(x1, x2, v, kf, kfn, hy_bias, tabs, jb):
    z = v
    for f, gate in enumerate((x1, x2)):
        z = _conv_call(z, gate, hy_bias[f][None, :], kf, kfn, f, tabs, jb)
    return z


def _filter_spectrum(L, w1, b1, fr, w2, b2, w3, tabs, ftabs, cb, jb):
    g_a, _, t_f, _ = tabs
    z, decay = ftabs
    pad_h = LANES - FILT_HID
    w1p = jnp.pad(w1, ((0, LANES - EMB_DIM), (0, pad_h)))
    w2p = jnp.pad(w2, ((0, pad_h), (0, pad_h)))
    w3p = jnp.pad(w3, ((0, pad_h), (0, 0)))
    w3_hi = w3p.astype(BF16)
    w3p = jnp.stack([w3_hi, (w3p - w3_hi.astype(F32)).astype(BF16)])
    pad1 = lambda a: jnp.pad(a, (0, pad_h))[None, :]
    hf = _filter_call(z, w1p, pad1(b1), pad1(fr), w2p, pad1(b2), w3p, decay, _tile(L, 512))
    s = _coarse_fwd_call(hf[None], g_a, cb)
    return _spec_call(s, t_f, cb, jb)


def _trunk(x, mix_norm, w_in, hy_conv_w, hy_conv_b, hf_w1, hf_b1, hf_freq, hf_w2, hf_b2, hf_w3, hy_bias,
           q_norm, w_uq, kv_norm, w_ukv, w_out, pool_w, pool_scale, mlp_norm, mlp_w1, mlp_w2, final_norm):
    B, L, _ = x.shape
    depth = mix_norm.shape[0]
    tm = _tile(L, 512)
    cb = 512
    jb = min(4, L // FFT_N2)
    tabs = _dft_tables(L)
    ftabs = _filter_tables(L)
    rope_tabs = _rope_tables(L)
    row = lambda a: a[None, :]
    for i in range(depth):
        w1 = mlp_w1[i].astype(BF16)
        w2 = mlp_w2[i].astype(BF16)
        g = row(mlp_norm[i])
        if i % 2 == 0:
            e = i // 2
            why, wlat, wq, wk, wvt, vone = _even_weights(w_in[e], w_uq[e], w_ukv[e])
            x1, x2, v, q, k, vt = _in_call(
                x, row(mix_norm[i]), why, wlat, hy_conv_w[e], row(hy_conv_b[e]),
                row(q_norm[e]), wq, row(kv_norm[e]), wk, wvt, vone, rope_tabs, tm)
            kf, kfn = _filter_spectrum(L, hf_w1[e], hf_b1[e], hf_freq[e], hf_w2[e], hf_b2[e], hf_w3[e],
                                       tabs, ftabs, cb, jb)
            y_hy = _hyena(x1, x2, v, kf, kfn, hy_bias[e], tabs, min(HY_JB, L // FFT_N2))
            y_att = _attn_call(q, k, vt, _tile(L, 1024))
            x = _out_call(x, y_hy, y_att, w_out[e].astype(BF16), g, w1, w2, tm)
        else:
            o = i // 2
            x = _odd_call(x, row(mix_norm[i]), pool_w[o].astype(BF16), row(pool_scale[o]), g, w1, w2,
                          row(final_norm), tm, final=(i == depth - 1))
    if depth % 2 == 1:
        raise NotImplementedError("final norm is fused into the last odd layer")
    return x


def kernel(x_prompt, x_sample, mix_norm, w_in, hy_conv_w, hy_conv_b, hf_w1, hf_b1, hf_freq, hf_w2, hf_b2,
           hf_w3, hy_bias, q_norm, w_uq, kv_norm, w_ukv, w_out, pool_w, pool_scale, mlp_norm, mlp_w1,
           mlp_w2, final_norm):
    weights = (mix_norm, w_in, hy_conv_w, hy_conv_b, hf_w1, hf_b1, hf_freq, hf_w2, hf_b2, hf_w3, hy_bias,
               q_norm, w_uq, kv_norm, w_ukv, w_out, pool_w, pool_scale, mlp_norm, mlp_w1, mlp_w2, final_norm)
    return (_trunk(x_prompt, *weights), _trunk(x_sample, *weights))
```

```python
import functools
import math

import jax
import jax.numpy as jnp
from jax import lax
from jax.experimental import pallas as pl
from jax.experimental.pallas import tpu as pltpu

F32 = jnp.float32
BF16 = jnp.bfloat16

D_MODEL = 1024
D_HY = 512
N_FILT = 2
HY_COLS = (N_FILT + 1) * D_HY
EMB_DIM = 33
N_BANDS = (EMB_DIM - 1) // 2
FILT_HID = 64
DECAY_FAST = 0.3
DECAY_SLOW = 1.5
DECAY_TARGET = 1e-2
N_HEADS = 8
NOPE_DIM = 64
ROPE_DIM = 32
V_DIM = 64
Q_LORA = 384
KV_LORA = 256
ROPE_THETA = 10000.0
POOL_WINDOWS = (2, 4, 8, 16)
POOL_GRP = D_MODEL // len(POOL_WINDOWS)
D_FF = 4 * D_MODEL
EPS = 1e-6
ATT_SCALE = (NOPE_DIM + ROPE_DIM) ** -0.5
Q_SCALE = ATT_SCALE * math.log2(math.e)
VT_ROWS = 80
HY_CB = 256
HY_RG = 4
HY_JB = 16
MLP_CHUNK = 1024
ATT_UNROLL = 4
ATT_MIN_GROUPS = 4
TOKEN_TILE = 512
ODD_TILE = 1024
ATT_TQ = 1024
FILT_CB = 512
FILT_JB = 4

LANES = 128
SUBLANES = 8
HALO = 16
FFT_N2 = 128
VMEM_LIMIT = 56 * 1024 * 1024


def _cparams(sem):
    return pltpu.CompilerParams(dimension_semantics=sem, vmem_limit_bytes=VMEM_LIMIT)


def _const_spec(shape):
    nd = len(shape)
    return pl.BlockSpec(shape, lambda *_: (0,) * nd, pipeline_mode=pl.Buffered(1))


def _rms(xf, g):
    ms = jnp.mean(xf * xf, axis=-1, keepdims=True)
    return xf * lax.rsqrt(ms + EPS) * g


def _bdot(a, b):
    return jnp.dot(a.astype(BF16), b.astype(BF16), preferred_element_type=F32)


def _in_kernel(xp_ref, xc_ref, xn_ref, g_ref, why_ref, wlat_ref, cw_ref, cb_ref,
               qn_ref, wq_ref, kvn_ref, wk_ref, wvt_ref, vone_ref, cos_ref, sin_ref, cost_ref, sint_ref,
               x1_ref, x2_ref, v_ref, q_ref, k_ref, vt_ref, *, tm, n_tiles):
    i = pl.program_id(1)
    g = g_ref[...]
    hc = _rms(xc_ref[0], g)
    hp = _rms(xp_ref[0], g) * jnp.where(i > 0, 1.0, 0.0)
    hn = _rms(xn_ref[0], g) * jnp.where(i < n_tiles - 1, 1.0, 0.0)
    hcb = hc.astype(BF16)
    h_ext = jnp.concatenate([hp.astype(BF16), hcb, hn.astype(BF16)], axis=0)
    p = jnp.dot(h_ext, why_ref[...], preferred_element_type=F32)
    n = tm + 2 * HALO
    cw = cw_ref[...]
    u = (pltpu.roll(p, 1, 0)[HALO:HALO + tm] * cw[0:1]
         + p[HALO:HALO + tm] * cw[1:2]
         + pltpu.roll(p, n - 1, 0)[HALO:HALO + tm] * cw[2:3]
         + cb_ref[...])
    for i_out, ref in enumerate((x1_ref, x2_ref, v_ref)):
        for c in range(D_HY // HY_CB):
            lo = i_out * D_HY + c * HY_CB
            ref[0, c] = u[:, lo:lo + HY_CB].astype(ref.dtype)

    lat = jnp.dot(hcb, wlat_ref[...], preferred_element_type=F32)
    cq = lat[:, :Q_LORA]
    ckv = lat[:, Q_LORA:Q_LORA + KV_LORA]
    kr = lat[:, Q_LORA + KV_LORA:Q_LORA + KV_LORA + LANES]
    krs = lat[:, Q_LORA + KV_LORA + LANES:]
    q2_t = lax.dot_general(wq_ref[...], _rms(cq, qn_ref[...]).astype(BF16), (((1,), (1,)), ((), ())),
                           preferred_element_type=F32)
    ckvn = _rms(ckv, kvn_ref[...]).astype(BF16)
    k2 = jnp.dot(ckvn, wk_ref[...], preferred_element_type=F32)
    v_t = lax.dot_general(wvt_ref[...], ckvn, (((1,), (1,)), ((), ())),
                          preferred_element_type=F32) + vone_ref[...]
    hw = N_HEADS * LANES
    kr_r = kr * cos_ref[...] + krs * sin_ref[...]
    cos_tt = cost_ref[...]
    sin_tt = sint_ref[...]
    for h in range(N_HEADS):
        sl = slice(h * LANES, (h + 1) * LANES)
        qh = (q2_t[sl] * cos_tt + q2_t[hw + h * LANES:hw + (h + 1) * LANES] * sin_tt) * Q_SCALE
        q_ref[0, h, 0] = qh.astype(BF16)
        k_ref[0, h] = (k2[:, sl] + kr_r).astype(BF16)
        vt_ref[0, h, 0] = v_t[h * VT_ROWS:(h + 1) * VT_ROWS, :].astype(BF16)


def _in_call(x, g, why, wlat, cw, cb, qn, wq, kvn, wk, wvt, vone, rope_tabs, tm):
    B, L, _ = x.shape
    n_tiles = L // tm
    hb = tm // HALO
    n_hb = L // HALO
    tok = lambda b, i: (b, i, 0)
    in_specs = [
        pl.BlockSpec((1, HALO, D_MODEL), lambda b, i: (b, jnp.maximum(i * hb - 1, 0), 0)),
        pl.BlockSpec((1, tm, D_MODEL), tok),
        pl.BlockSpec((1, HALO, D_MODEL), lambda b, i: (b, jnp.minimum((i + 1) * hb, n_hb - 1), 0)),
        _const_spec(g.shape), _const_spec(why.shape), _const_spec(wlat.shape),
        _const_spec(cw.shape), _const_spec(cb.shape), _const_spec(qn.shape),
        _const_spec(wq.shape), _const_spec(kvn.shape), _const_spec(wk.shape),
        _const_spec(wvt.shape), _const_spec(vone.shape),
    ] + [pl.BlockSpec((tm, LANES), lambda b, i: (i, 0)) for _ in rope_tabs[:2]
         ] + [pl.BlockSpec((LANES, tm), lambda b, i: (0, i)) for _ in rope_tabs[2:]]
    hy = jax.ShapeDtypeStruct((B, D_HY // HY_CB, L, HY_CB), BF16)
    hd = jax.ShapeDtypeStruct((B, N_HEADS, L, LANES), BF16)
    hy_spec = pl.BlockSpec((1, D_HY // HY_CB, tm, HY_CB), lambda b, i: (b, 0, i, 0))
    hd_spec = pl.BlockSpec((1, N_HEADS, tm, LANES), lambda b, i: (b, 0, i, 0))
    qt = jax.ShapeDtypeStruct((B, N_HEADS, n_tiles, LANES, tm), BF16)
    qt_spec = pl.BlockSpec((1, N_HEADS, 1, LANES, tm), lambda b, i: (b, 0, i, 0, 0))
    vt = jax.ShapeDtypeStruct((B, N_HEADS, n_tiles, VT_ROWS, tm), BF16)
    vt_spec = pl.BlockSpec((1, N_HEADS, 1, VT_ROWS, tm), lambda b, i: (b, 0, i, 0, 0))
    return pl.pallas_call(
        functools.partial(_in_kernel, tm=tm, n_tiles=n_tiles),
        grid=(B, n_tiles),
        in_specs=in_specs,
        out_specs=[hy_spec, hy_spec, hy_spec, qt_spec, hd_spec, vt_spec],
        out_shape=[hy, hy, hy, qt, hd, vt],
        compiler_params=_cparams(("parallel", "arbitrary")),
        name="even_in",
    )(x, x, x, g, why, wlat, cw, cb, qn, wq, kvn, wk, wvt, vone, *rope_tabs)


def _attn_kernel(q_ref, k_ref, vt_ref, o_ref, sa_ref, sb_ref, *, tq, tk, n_chunks):
    qs = [jnp.concatenate([q_ref[0, hh, c] for c in range(q_ref.shape[2])], axis=1) for hh in range(2)]

    def scores(hh, j, s_ref):
        kc = k_ref[0, hh, pl.ds(pl.multiple_of(j * tk, tk), tk), :]
        s = jnp.dot(kc, qs[hh], preferred_element_type=F32)
        s_ref[hh] = s
        return jnp.max(s, axis=0, keepdims=True)

    def step(j, src_ref, dst_ref, state, cms, fetch):
        new_state, new_cms = [], []
        for hh in range(2):
            m, acc = state[2 * hh], state[2 * hh + 1]
            if fetch:
                new_cms.append(scores(hh, j + 1, dst_ref))
            m_new = jnp.maximum(m, cms[hh])
            p = jnp.exp2(src_ref[hh] - m_new).astype(BF16)
            acc = acc * jnp.exp2(m - m_new) + jnp.dot(vt_ref[0, hh, j], p, preferred_element_type=F32)
            new_state += [m_new, acc]
        return new_state, new_cms

    def run_chunks(j, state, cms, last):
        for c in range(unroll):
            src, dst = (sa_ref, sb_ref) if c % 2 == 0 else (sb_ref, sa_ref)
            state, cms = step(j + c, src, dst, state, cms, not (last and c == unroll - 1))
        return state, cms

    m0 = jnp.full((1, tq), -1e30, F32)
    acc0 = jnp.zeros((VT_ROWS, tq), F32)
    state = [m0, acc0, m0, acc0]
    cms = [scores(hh, 0, sa_ref) for hh in range(2)]
    if n_chunks == 1:
        res, _ = step(0, sa_ref, sb_ref, state, cms, False)
    else:
        unroll = ATT_UNROLL if (n_chunks % ATT_UNROLL == 0 and n_chunks >= ATT_MIN_GROUPS * ATT_UNROLL) else 2

        def body(i, carry):
            st, cm = run_chunks(unroll * i, list(carry[:4]), list(carry[4:]), last=False)
            return tuple(st) + tuple(cm)

        carry = lax.fori_loop(0, n_chunks // unroll - 1, body, tuple(state) + tuple(cms))
        res, _ = run_chunks(n_chunks - unroll, list(carry[:4]), list(carry[4:]), last=True)
    o_t = jnp.concatenate([res[1][:V_DIM] / res[1][V_DIM:V_DIM + 1],
                           res[3][:V_DIM] / res[3][V_DIM:V_DIM + 1]], axis=0)
    o_ref[0] = o_t.T


def _attn_call(qt, k, vt, tq):
    B, _, L, _ = k.shape
    _, _, n_chunks, _, tk = vt.shape
    assert n_chunks == 1 or n_chunks % 2 == 0, n_chunks
    assert qt.shape == (B, N_HEADS, n_chunks, LANES, tk) and tq % tk == 0
    return pl.pallas_call(
        functools.partial(_attn_kernel, tq=tq, tk=tk, n_chunks=n_chunks),
        grid=(B, N_HEADS // 2, L // tq),
        in_specs=[pl.BlockSpec((1, 2, tq // tk, LANES, tk), lambda b, hp, i: (b, hp, i, 0, 0)),
                  pl.BlockSpec((1, 2, L, LANES), lambda b, hp, i: (b, hp, 0, 0)),
                  pl.BlockSpec((1, 2, n_chunks, VT_ROWS, tk), lambda b, hp, i: (b, hp, 0, 0, 0))],
        out_specs=pl.BlockSpec((1, tq, LANES), lambda b, hp, i: (b, i, hp)),
        out_shape=jax.ShapeDtypeStruct((B, L, N_HEADS * V_DIM), F32),
        scratch_shapes=[pltpu.VMEM((2, tk, tq), F32), pltpu.VMEM((2, tk, tq), F32)],
        compiler_params=_cparams(("parallel", "parallel", "arbitrary")),
        name="attention",
    )(qt, k, vt)


def _mlp(x, g_ref, w1_ref, w2_ref, between=()):
    h = _rms(x, g_ref[...]).astype(BF16)
    acc = x
    for c in range(D_FF // MLP_CHUNK):
        cols = slice(c * MLP_CHUNK, (c + 1) * MLP_CHUNK)
        hid = jnp.dot(h, w1_ref[:, cols], preferred_element_type=F32)
        hid = jnp.square(jnp.maximum(hid, 0.0)).astype(BF16)
        acc = acc + jnp.dot(hid, w2_ref[cols, :], preferred_element_type=F32)
        if c < len(between):
            between[c]()
    return acc


def _out_kernel(x_ref, yh_ref, ya_ref, wo_ref, g_ref, w1_ref, w2_ref, o_ref):
    y = jnp.concatenate([yh_ref[0, c] for c in range(D_HY // HY_CB)] + [ya_ref[0].astype(BF16)], axis=1)
    x = x_ref[0] + jnp.dot(y, wo_ref[...], preferred_element_type=F32)
    o_ref[0] = _mlp(x, g_ref, w1_ref, w2_ref)


def _out_call(x, yh, ya, wo, g, w1, w2, tm):
    B, L, _ = x.shape
    tok = lambda b, i: (b, i, 0)
    return pl.pallas_call(
        _out_kernel,
        grid=(B, L // tm),
        in_specs=[pl.BlockSpec((1, tm, D_MODEL), tok),
                  pl.BlockSpec((1, D_HY // HY_CB, tm, HY_CB), lambda b, i: (b, 0, i, 0)),
                  pl.BlockSpec((1, tm, D_HY), tok), _const_spec(wo.shape), _const_spec(g.shape),
                  _const_spec(w1.shape), _const_spec(w2.shape)],
        out_specs=pl.BlockSpec((1, tm, D_MODEL), tok),
        out_shape=jax.ShapeDtypeStruct(x.shape, F32),
        compiler_params=_cparams(("parallel", "arbitrary")),
        name="even_out_mlp",
    )(x, yh, ya, wo, g, w1, w2)


def _odd_kernel(xp_ref, xc_ref, xn_ref, gm_ref, pw_ref, ps_ref, g_ref, w1_ref, w2_ref, fn_ref,
                o_ref, *, tm, n_tiles, seq, final):
    i = pl.program_id(1)
    gm = gm_ref[...]
    ps = ps_ref[...]
    x_ext = jnp.concatenate([xp_ref[0] * jnp.where(i > 0, 1.0, 0.0), xc_ref[0],
                             xn_ref[0] * jnp.where(i < n_tiles - 1, 1.0, 0.0)], axis=0)
    part = tm // 2 if tm % (4 * HALO) == 0 else tm
    n = part + 2 * HALO

    def normed(r0):
        return _rms(x_ext[r0:r0 + n], gm)

    def pooled(ep, r0, gi):
        w = POOL_WINDOWS[gi]
        sl = slice(gi * POOL_GRP, (gi + 1) * POOL_GRP)
        eg = ep[:, sl]
        s = pltpu.roll(eg, 1, 0) + eg
        half = 1
        while 2 * half < w:
            s = pltpu.roll(s, half, 0) + pltpu.roll(s, n - half, 0)
            half *= 2
        t = i * tm + r0 + lax.broadcasted_iota(jnp.int32, (part, 1), 0)
        cnt = (jnp.minimum(t + w // 2, seq) - jnp.maximum(t - w // 2, 0)).astype(F32)
        d = s[HALO:HALO + part] / cnt - eg[HALO:HALO + part]
        return _bdot(d, pw_ref[gi]) * ps[:, sl]

    def finish(r0, x_mixed, between=()):
        y = _mlp(x_mixed, g_ref, w1_ref, w2_ref, between)
        if final:
            y = _rms(y, fn_ref[...])
        o_ref[0, r0:r0 + part, :] = y

    groups = range(len(POOL_WINDOWS))
    ep = normed(0)
    x_mixed = x_ext[HALO:HALO + part] + jnp.concatenate([pooled(ep, 0, gi) for gi in groups], axis=1)
    for r0 in range(part, tm, part):
        nxt = {}

        def piece(gi, r0=r0, nxt=nxt):
            if gi == 0:
                nxt["e"] = normed(r0)
            nxt[gi] = pooled(nxt["e"], r0, gi)

        finish(r0 - part, x_mixed, [functools.partial(piece, gi) for gi in groups])
        x_mixed = x_ext[HALO + r0:HALO + r0 + part] + jnp.concatenate([nxt[gi] for gi in groups], axis=1)
    finish(tm - part, x_mixed)


def _odd_call(x, gm, pw, ps, g, w1, w2, fn, tm, final):
    B, L, _ = x.shape
    n_tiles = L // tm
    hb = tm // HALO
    n_hb = L // HALO
    tok = lambda b, i: (b, i, 0)
    return pl.pallas_call(
        functools.partial(_odd_kernel, tm=tm, n_tiles=n_tiles, seq=L, final=final),
        grid=(B, n_tiles),
        in_specs=[
            pl.BlockSpec((1, HALO, D_MODEL), lambda b, i: (b, jnp.maximum(i * hb - 1, 0), 0)),
            pl.BlockSpec((1, tm, D_MODEL), tok),
            pl.BlockSpec((1, HALO, D_MODEL), lambda b, i: (b, jnp.minimum((i + 1) * hb, n_hb - 1), 0)),
            _const_spec(gm.shape), _const_spec(pw.shape), _const_spec(ps.shape), _const_spec(g.shape),
            _const_spec(w1.shape), _const_spec(w2.shape), _const_spec(fn.shape)],
        out_specs=pl.BlockSpec((1, tm, D_MODEL), tok),
        out_shape=jax.ShapeDtypeStruct(x.shape, F32),
        compiler_params=_cparams(("parallel", "arbitrary")),
        name="odd_pool_mlp",
    )(x, x, x, gm, pw, ps, g, w1, w2, fn)


def _coarse_fwd_kernel(x_ref, g_ref, o_ref):
    x = x_ref[0, :, 0]
    a, _, cb = x.shape
    xb = x.reshape(a * SUBLANES, cb).astype(BF16)
    y = jnp.dot(g_ref[...], xb, preferred_element_type=F32)
    o_ref[0, :, 0] = y.reshape(y.shape[0] // SUBLANES, SUBLANES, cb)


def _coarse_fwd_call(x, g_a, cb):
    B, L, C = x.shape
    a = L // FFT_N2
    n1 = 2 * a
    nb = FFT_N2 // SUBLANES
    x5 = x.reshape(B, a, nb, SUBLANES, C)
    out = pl.pallas_call(
        _coarse_fwd_kernel,
        grid=(B, nb, C // cb),
        in_specs=[pl.BlockSpec((1, a, 1, SUBLANES, cb), lambda b, r, c: (b, 0, r, 0, c)),
                  _const_spec(g_a.shape)],
        out_specs=pl.BlockSpec((1, n1, 1, SUBLANES, cb), lambda b, r, c: (b, 0, r, 0, c)),
        out_shape=jax.ShapeDtypeStruct((B, n1, nb, SUBLANES, C), F32),
        compiler_params=_cparams(("parallel", "parallel", "arbitrary")),
        name="hyena_coarse_fwd",
    )(x5, g_a)
    return out.reshape(B, n1, FFT_N2, C)


def _cmul(x, kr, ki):
    xr, xi = x[:FFT_N2], x[FFT_N2:]
    return jnp.concatenate([xr * kr - xi * ki, xr * ki + xi * kr], axis=0)


def _conv_kernel(xf_ref, ga_ref, t_ref, ta_ref, kf_ref, kfn_ref, gb_ref, u_ref, gate_ref, bias_ref,
                 o_ref, s_ref, *, nb, nj, jb):
    t = pl.program_id(2)
    n1, _, cb = s_ref.shape

    def rows(step, g):
        return pl.ds(pl.multiple_of((step * HY_RG + g) * SUBLANES, SUBLANES), SUBLANES)

    @pl.when(t < nb)
    def _():
        x_all = xf_ref[0, 0, :, 0].astype(F32)
        for g in range(HY_RG):
            x = x_all[:, g * SUBLANES:(g + 1) * SUBLANES, :]
            xb = x.reshape(x.shape[0] * SUBLANES, cb).astype(BF16)
            y = jnp.dot(ga_ref[...], xb, preferred_element_type=F32)
            s_ref[:, rows(t, g), :] = y.reshape(n1, SUBLANES, cb)

    def pair(jj):
        return (t - nb) * jb + jj

    def load_pair(jj):
        return s_ref[pl.ds(2 * pair(jj), 2)].reshape(2 * FFT_N2, cb)

    def store_pair(jj, z):
        s_ref[pl.ds(2 * pair(jj), 2)] = z.reshape(2, FFT_N2, cb)

    def fine(jjs):
        xs = [_bdot(t_ref[pair(jj)], load_pair(jj)) for jj in jjs]
        ys = [_cmul(x, kf_ref[0, pair(jj)], kf_ref[1, pair(jj)]) for jj, x in zip(jjs, xs)]
        for jj, y in zip(jjs, ys):
            store_pair(jj, _bdot(ta_ref[pair(jj)], y))

    def fine_first_pair():
        s = load_pair(0)
        lo = lax.broadcasted_iota(jnp.int32, (2 * FFT_N2, 1), 0) < FFT_N2
        xa = _bdot(t_ref[0], jnp.where(lo, s, 0.0))
        xb = _bdot(t_ref[0], jnp.where(lo, 0.0, s))
        ya = _cmul(xa, kf_ref[0, 0], kf_ref[1, 0])
        yb = _cmul(xb, kfn_ref[0, 0], kfn_ref[1, 0])
        store_pair(0, jnp.where(lo, _bdot(ta_ref[0], ya), _bdot(ta_ref[0], yb)))

    @pl.when(t == nb)
    def _():
        fine_first_pair()
        fine(list(range(1, jb)))

    @pl.when((t > nb) & (t < nb + nj))
    def _():
        fine(list(range(jb)))

    @pl.when(t >= nb + nj)
    def _():
        u_all = u_ref[0, 0, :, 0].astype(F32)
        gate_all = gate_ref[0, 0, :, 0].astype(F32)
        outs = []
        for g in range(HY_RG):
            sub = slice(g * SUBLANES, (g + 1) * SUBLANES)
            z = s_ref[:, rows(t - nb - nj, g), :]
            y = jnp.dot(gb_ref[...], z.reshape(n1 * SUBLANES, cb).astype(BF16), preferred_element_type=F32)
            y = y.reshape(y.shape[0] // SUBLANES, SUBLANES, cb)
            outs.append((y + u_all[:, sub, :] * bias_ref[...]) * gate_all[:, sub, :])
        o_ref[0, 0, :, 0] = jnp.concatenate(outs, axis=1).astype(o_ref.dtype)


def _conv_call(u, gate, bias, kf, kfn, filt, tabs, jb):
    g_a, g_b, t_f, t_i = tabs
    B, nc, L, cb = u.shape
    a = L // FFT_N2
    n1 = 2 * a
    rg_rows = HY_RG * SUBLANES
    assert u.dtype == BF16 and rg_rows % (2 * SUBLANES) == 0
    nb = FFT_N2 // rg_rows
    nj = (n1 // 2) // jb
    koff = filt * nc
    u5 = u.reshape(B, nc, a, nb, rg_rows, cb)
    fwd_r = lambda c, b, t: (b, c, 0, jnp.minimum(t, nb - 1), 0, 0)
    inv_r = lambda c, b, t: (b, c, 0, jnp.clip(t - nb - nj, 0, nb - 1), 0, 0)
    tok = (1, 1, a, 1, rg_rows, cb)
    kspec = lambda n: pl.BlockSpec((2, n, FFT_N2, cb), lambda c, b, t: (0, 0, 0, koff + c),
                                   pipeline_mode=pl.Buffered(1))
    out = pl.pallas_call(
        functools.partial(_conv_kernel, nb=nb, nj=nj, jb=jb),
        grid=(nc, B, 2 * nb + nj),
        in_specs=[pl.BlockSpec(tok, fwd_r), _const_spec(g_a.shape), _const_spec(t_f.shape),
                  _const_spec(t_i.shape), kspec(n1 // 2), kspec(1),
                  _const_spec(g_b.shape), pl.BlockSpec(tok, inv_r), pl.BlockSpec(tok, inv_r),
                  pl.BlockSpec((1, cb), lambda c, b, t: (0, c))],
        out_specs=pl.BlockSpec(tok, inv_r),
        out_shape=jax.ShapeDtypeStruct(u5.shape, u.dtype),
        scratch_shapes=[pltpu.VMEM((n1, FFT_N2, cb), F32)],
        compiler_params=_cparams(("arbitrary", "arbitrary", "arbitrary")),
        name="hyena_conv",
    )(u5, g_a, t_f, t_i, kf, kfn, g_b, u5, gate.reshape(u5.shape), bias)
    return out.reshape(u.shape)


def _filter_kernel(z_ref, w1_ref, b1_ref, fr_ref, w2_ref, b2_ref, w3_ref, dec_ref, o_ref, *, tl):
    hi = lax.Precision.HIGHEST
    fr = fr_ref[...]
    h = jnp.sin(fr * (jnp.dot(z_ref[...], w1_ref[...], precision=hi, preferred_element_type=F32) + b1_ref[...]))
    h = jnp.sin(fr * (jnp.dot(h, w2_ref[...], precision=hi, preferred_element_type=F32) + b2_ref[...]))
    h_hi = h.astype(BF16)
    h_lo = (h - h_hi.astype(F32)).astype(BF16)
    w_hi = w3_ref[0]
    h = (jnp.dot(h_hi, w_hi, preferred_element_type=F32) + jnp.dot(h_lo, w_hi, preferred_element_type=F32)
         + jnp.dot(h_hi, w3_ref[1], preferred_element_type=F32))
    dec = dec_ref[...]
    h = h * jnp.concatenate([dec] * (2 * N_FILT), axis=1)
    row = pl.program_id(0) * tl + lax.broadcasted_iota(jnp.int32, h.shape, 0)
    lane = lax.broadcasted_iota(jnp.int32, h.shape, 1)
    o_ref[...] = jnp.where((row == 0) & (lane >= N_FILT * D_HY), 0.0, h)


def _filter_call(z, w1, b1, fr, w2, b2, w3, dec, tl):
    L = z.shape[0]
    return pl.pallas_call(
        functools.partial(_filter_kernel, tl=tl),
        grid=(L // tl,),
        in_specs=[pl.BlockSpec((tl, LANES), lambda i: (i, 0)),
                  _const_spec(w1.shape), _const_spec(b1.shape), _const_spec(fr.shape),
                  _const_spec(w2.shape), _const_spec(b2.shape), _const_spec(w3.shape),
                  pl.BlockSpec((tl, D_HY), lambda i: (i, 0))],
        out_specs=pl.BlockSpec((tl, 2 * N_FILT * D_HY), lambda i: (i, 0)),
        out_shape=jax.ShapeDtypeStruct((L, 2 * N_FILT * D_HY), F32),
        compiler_params=_cparams(("arbitrary",)),
        name="hyena_filter_mlp",
    )(z, w1, b1, fr, w2, b2, w3, dec)


def _spec_kernel(sf_ref, sb_ref, t_ref, kf_ref, kfn_ref, *, jb):
    step = pl.program_id(1)
    cb = sf_ref.shape[-1]

    def combine(xf, xb):
        return jnp.concatenate([xf[:FFT_N2] + xb[:FFT_N2], xf[FFT_N2:] - xb[FFT_N2:]], axis=0)

    def generic(jj):
        xf = _bdot(t_ref[jj], sf_ref[0, 2 * jj:2 * jj + 2].reshape(2 * FFT_N2, cb))
        xb = _bdot(t_ref[jj], sb_ref[0, 2 * jj:2 * jj + 2].reshape(2 * FFT_N2, cb))
        kf_ref[:, jj] = combine(xf, xb).reshape(2, FFT_N2, cb).astype(kf_ref.dtype)

    def special():
        row = lax.broadcasted_iota(jnp.int32, (2 * FFT_N2, 1), 0)
        lo = row < FFT_N2
        sf = sf_ref[0, 0:2].reshape(2 * FFT_N2, cb)
        sb = sb_ref[0, 0:2].reshape(2 * FFT_N2, cb)
        k0 = combine(_bdot(t_ref[0], jnp.where(lo, sf, 0.0)), _bdot(t_ref[0], jnp.where(lo, sb, 0.0)))
        kn = combine(_bdot(t_ref[0], jnp.where(lo, 0.0, sf)), _bdot(t_ref[0], jnp.where(lo, 0.0, sb)))
        kf_ref[:, 0] = k0.reshape(2, FFT_N2, cb).astype(kf_ref.dtype)
        kfn_ref[:, 0] = kn.reshape(2, FFT_N2, cb).astype(kfn_ref.dtype)

    pl.when(step == 0)(special)
    pl.when(step != 0)(lambda: generic(0))
    for jj in range(1, jb):
        generic(jj)


def _spec_call(s, t_f, cb, jb):
    _, n1, _, c2 = s.shape
    npair = n1 // 2
    C = c2 // 2
    nc = C // cb
    return pl.pallas_call(
        functools.partial(_spec_kernel, jb=jb),
        grid=(nc, npair // jb),
        in_specs=[pl.BlockSpec((1, 2 * jb, FFT_N2, cb), lambda c, j: (0, j, 0, c)),
                  pl.BlockSpec((1, 2 * jb, FFT_N2, cb), lambda c, j: (0, j, 0, nc + c)),
                  pl.BlockSpec((jb, 2 * FFT_N2, 2 * FFT_N2), lambda c, j: (j, 0, 0))],
        out_specs=[pl.BlockSpec((2, jb, FFT_N2, cb), lambda c, j: (0, j, 0, c)),
                   pl.BlockSpec((2, 1, FFT_N2, cb), lambda c, j: (0, 0, 0, c))],
        out_shape=[jax.ShapeDtypeStruct((2, npair, FFT_N2, C), BF16),
                   jax.ShapeDtypeStruct((2, 1, FFT_N2, C), BF16)],
        compiler_params=_cparams(("arbitrary", "arbitrary")),
        name="hyena_filter_spectrum",
    )(s, s, t_f)


def _dft_tables(L):
    a_n = L // FFT_N2
    n1 = 2 * a_n
    n = 2 * L
    npair = n1 // 2
    two_pi = 2.0 * math.pi

    a = jnp.arange(a_n, dtype=jnp.int32)[None, :]
    slot = jnp.arange(n1, dtype=jnp.int32)[:, None]
    j = slot // 2
    ang = two_pi * ((a * j) % n1).astype(F32) / n1
    sign = jnp.where(a % 2 == 0, 1.0, -1.0)
    is_im = (slot % 2) == 1
    f1 = jnp.where(is_im, -jnp.sin(ang), jnp.cos(ang))
    f1 = jnp.where(slot == 0, 1.0, jnp.where(slot == 1, sign, f1))
    fb = jnp.where(is_im, -2.0 * jnp.sin(ang), 2.0 * jnp.cos(ang))
    fb = jnp.where(slot == 0, 1.0, jnp.where(slot == 1, sign, fb)) / n
    eye = jnp.eye(SUBLANES, dtype=F32)
    g_a = jnp.kron(f1, eye).astype(BF16)
    g_b = jnp.kron(fb.T, eye).astype(BF16)

    b = jnp.arange(FFT_N2, dtype=jnp.int32)[None, None, :]
    d = jnp.arange(FFT_N2, dtype=jnp.int32)[None, :, None]
    c = jnp.arange(npair + 1, dtype=jnp.int32)[:, None, None]
    phi = two_pi * ((b * (d * n1 + c)) % n).astype(F32) / n
    co, si = jnp.cos(phi), jnp.sin(phi)
    t_f = jnp.concatenate([jnp.concatenate([co, si], axis=2),
                           jnp.concatenate([-si, co], axis=2)], axis=1)[:npair]
    t0 = jnp.concatenate([jnp.concatenate([co[0], co[npair]], axis=1),
                          jnp.concatenate([-si[0], -si[npair]], axis=1)], axis=0)
    t_f = t_f.at[0].set(t0)
    t_i = jnp.swapaxes(t_f, 1, 2)
    return g_a, g_b, t_f.astype(BF16), t_i.astype(BF16)


def _filter_tables(L):
    t = jnp.linspace(0.0, 1.0, L, dtype=F32)[:, None]
    wpos = (2.0 * math.pi / L) * jnp.arange(L, dtype=F32)[:, None]
    bands = jnp.linspace(1e-4, N_BANDS - 1, N_BANDS, dtype=F32)[None, :]
    z = jnp.concatenate([t, jnp.cos(bands * wpos), -jnp.sin(bands * wpos)], axis=-1)
    z = jnp.pad(z, ((0, 0), (0, LANES - EMB_DIM)))
    max_decay = math.log(DECAY_TARGET) / DECAY_FAST
    min_decay = math.log(DECAY_TARGET) / DECAY_SLOW
    deltas = jnp.linspace(min_decay, max_decay, D_HY, dtype=F32)
    decay = jnp.exp(-t * jnp.abs(deltas))
    return z, decay


def _rope_tables(L):
    inv_freq = 1.0 / (ROPE_THETA ** (jnp.arange(0, ROPE_DIM, 2, dtype=F32) / ROPE_DIM))
    ang = jnp.arange(L, dtype=F32)[:, None] * inv_freq[None, :]
    cos, sin = jnp.cos(ang), jnp.sin(ang)
    one = jnp.ones((L, NOPE_DIM), F32)
    zero_n = jnp.zeros((L, NOPE_DIM), F32)
    zero_t = jnp.zeros((L, LANES - NOPE_DIM - ROPE_DIM), F32)
    cos_t = jnp.concatenate([one, cos, cos, zero_t], axis=1)
    sin_t = jnp.concatenate([zero_n, -sin, sin, zero_t], axis=1)
    return cos_t, sin_t, cos_t.T, sin_t.T


def _swap_halves(w):
    half = w.shape[-1] // 2
    return jnp.concatenate([w[..., half:], w[..., :half]], axis=-1)


def _place(w, width, offset):
    return jnp.pad(w, ((0, 0), (offset, width - offset - w.shape[1])))


def _even_weights(w_in, w_uq, w_ukv):
    why = w_in[:, :HY_COLS].astype(BF16)
    w_kr = w_in[:, HY_COLS + Q_LORA + KV_LORA:]
    wlat = jnp.concatenate([
        w_in[:, HY_COLS:HY_COLS + Q_LORA + KV_LORA],
        _place(w_kr, LANES, NOPE_DIM),
        _place(_swap_halves(w_kr), LANES, NOPE_DIM)], axis=1).astype(BF16)
    qh = w_uq.reshape(Q_LORA, N_HEADS, NOPE_DIM + ROPE_DIM)
    q_main = jnp.pad(qh, ((0, 0), (0, 0), (0, LANES - NOPE_DIM - ROPE_DIM)))
    q_sw = jnp.pad(_swap_halves(qh[..., NOPE_DIM:]),
                   ((0, 0), (0, 0), (NOPE_DIM, LANES - NOPE_DIM - ROPE_DIM)))
    wq = jnp.concatenate([q_main.reshape(Q_LORA, -1), q_sw.reshape(Q_LORA, -1)], axis=1).T.astype(BF16)
    kvh = w_ukv.reshape(KV_LORA, N_HEADS, NOPE_DIM + V_DIM)
    k_main = jnp.pad(kvh[..., :NOPE_DIM], ((0, 0), (0, 0), (0, LANES - NOPE_DIM)))
    wk = k_main.reshape(KV_LORA, -1).astype(BF16)
    v_rows = jnp.pad(kvh[..., NOPE_DIM:], ((0, 0), (0, 0), (0, VT_ROWS - V_DIM)))
    wvt = v_rows.reshape(KV_LORA, -1).T.astype(BF16)
    vone = (jnp.arange(N_HEADS * VT_ROWS) % VT_ROWS == V_DIM).astype(F32)[:, None]
    return why, wlat, wq, wk, wvt, vone


def _tile(L, want):
    return min(want, L)


def _hyena(x1, x2, v, kf, kfn, hy_bias, tabs, jb):
    z = v
    for f, gate in enumerate((x1, x2)):
        z = _conv_call(z, gate, hy_bias[f][None, :], kf, kfn, f, tabs, jb)
    return z


def _filter_spectrum(L, w1, b1, fr, w2, b2, w3, tabs, ftabs):
    g_a, _, t_f, _ = tabs
    z, decay = ftabs
    pad_h = LANES - FILT_HID
    w1p = jnp.pad(w1, ((0, LANES - EMB_DIM), (0, pad_h)))
    w2p = jnp.pad(w2, ((0, pad_h), (0, pad_h)))
    w3p = jnp.pad(w3, ((0, pad_h), (0, 0)))
    w3_hi = w3p.astype(BF16)
    w3p = jnp.stack([w3_hi, (w3p - w3_hi.astype(F32)).astype(BF16)])
    pad1 = lambda a: jnp.pad(a, (0, pad_h))[None, :]
    hf = _filter_call(z, w1p, pad1(b1), pad1(fr), w2p, pad1(b2), w3p, decay, _tile(L, TOKEN_TILE))
    s = _coarse_fwd_call(hf[None], g_a, FILT_CB)
    return _spec_call(s, t_f, FILT_CB, min(FILT_JB, L // FFT_N2))


def _trunk(x, mix_norm, w_in, hy_conv_w, hy_conv_b, hf_w1, hf_b1, hf_freq, hf_w2, hf_b2, hf_w3, hy_bias,
           q_norm, w_uq, kv_norm, w_ukv, w_out, pool_w, pool_scale, mlp_norm, mlp_w1, mlp_w2, final_norm):
    B, L, _ = x.shape
    depth = mix_norm.shape[0]
    tm = _tile(L, TOKEN_TILE)
    tabs = _dft_tables(L)
    ftabs = _filter_tables(L)
    rope_tabs = _rope_tables(L)
    row = lambda a: a[None, :]
    for i in range(depth):
        w1 = mlp_w1[i].astype(BF16)
        w2 = mlp_w2[i].astype(BF16)
        g = row(mlp_norm[i])
        if i % 2 == 0:
            e = i // 2
            why, wlat, wq, wk, wvt, vone = _even_weights(w_in[e], w_uq[e], w_ukv[e])
            x1, x2, v, q, k, vt = _in_call(
                x, row(mix_norm[i]), why, wlat, hy_conv_w[e], row(hy_conv_b[e]),
                row(q_norm[e]), wq, row(kv_norm[e]), wk, wvt, vone, rope_tabs, tm)
            kf, kfn = _filter_spectrum(L, hf_w1[e], hf_b1[e], hf_freq[e], hf_w2[e], hf_b2[e], hf_w3[e],
                                       tabs, ftabs)
            y_hy = _hyena(x1, x2, v, kf, kfn, hy_bias[e], tabs, min(HY_JB, L // FFT_N2))
            y_att = _attn_call(q, k, vt, _tile(L, ATT_TQ))
            x = _out_call(x, y_hy, y_att, w_out[e].astype(BF16), g, w1, w2, tm)
        else:
            o = i // 2
            x = _odd_call(x, row(mix_norm[i]), pool_w[o].astype(BF16), row(pool_scale[o]), g, w1, w2,
                          row(final_norm), _tile(L, ODD_TILE), final=(i == depth - 1))
    if depth % 2 == 1:
        raise NotImplementedError("final norm is fused into the last odd layer")
    return x


def kernel(x_prompt, x_sample, mix_norm, w_in, hy_conv_w, hy_conv_b, hf_w1, hf_b1, hf_freq, hf_w2, hf_b2,
           hf_w3, hy_bias, q_norm, w_uq, kv_norm, w_ukv, w_out, pool_w, pool_scale, mlp_norm, mlp_w1,
           mlp_w2, final_norm):
    weights = (mix_norm, w_in, hy_conv_w, hy_conv_b, hf_w1, hf_b1, hf_freq, hf_w2, hf_b2, hf_w3, hy_bias,
               q_norm, w_uq, kv_norm, w_ukv, w_out, pool_w, pool_scale, mlp_norm, mlp_w1, mlp_w2, final_norm)
    return (_trunk(x_prompt, *weights), _trunk(x_sample, *weights))
```

```python
import functools
import math

import jax
import jax.numpy as jnp
from jax import lax
from jax.experimental import pallas as pl
from jax.experimental.pallas import tpu as pltpu

F32 = jnp.float32
BF16 = jnp.bfloat16

D_MODEL = 1024
D_HY = 512
N_FILT = 2
HY_COLS = (N_FILT + 1) * D_HY
EMB_DIM = 33
N_BANDS = (EMB_DIM - 1) // 2
FILT_HID = 64
DECAY_FAST = 0.3
DECAY_SLOW = 1.5
DECAY_TARGET = 1e-2
N_HEADS = 8
NOPE_DIM = 64
ROPE_DIM = 32
V_DIM = 64
Q_LORA = 384
KV_LORA = 256
ROPE_THETA = 10000.0
POOL_WINDOWS = (2, 4, 8, 16)
POOL_GRP = D_MODEL // len(POOL_WINDOWS)
D_FF = 4 * D_MODEL
EPS = 1e-6
ATT_SCALE = (NOPE_DIM + ROPE_DIM) ** -0.5
Q_SCALE = ATT_SCALE * math.log2(math.e)
VT_ROWS = 80
HY_CB = 256
HY_RG = 4
HY_JB = 16
FINE_LAG = 2
MLP_CHUNK = 1024
ATT_UNROLL = 4
ATT_MIN_GROUPS = 4
TOKEN_TILE = 512
ODD_TILE = 1024
ATT_TQ = 1024
FILT_CB = 512
FILT_JB = 4

LANES = 128
SUBLANES = 8
HALO = 16
FFT_N2 = 128
VMEM_LIMIT = 56 * 1024 * 1024


def _cparams(sem):
    return pltpu.CompilerParams(dimension_semantics=sem, vmem_limit_bytes=VMEM_LIMIT)


def _const_spec(shape):
    nd = len(shape)
    return pl.BlockSpec(shape, lambda *_: (0,) * nd, pipeline_mode=pl.Buffered(1))


def _rms(xf, g):
    ms = jnp.mean(xf * xf, axis=-1, keepdims=True)
    return xf * lax.rsqrt(ms + EPS) * g


def _bdot(a, b):
    return jnp.dot(a.astype(BF16), b.astype(BF16), preferred_element_type=F32)


def _in_kernel(xp_ref, xc_ref, xn_ref, g_ref, why_ref, wlat_ref, cw_ref, cb_ref,
               qn_ref, wq_ref, kvn_ref, wk_ref, wvt_ref, vone_ref, cos_ref, sin_ref, cost_ref, sint_ref,
               x1_ref, x2_ref, v_ref, q_ref, k_ref, vt_ref, *, tm, n_tiles):
    i = pl.program_id(1)
    g = g_ref[...]
    hc = _rms(xc_ref[0], g)
    hp = _rms(xp_ref[0], g) * jnp.where(i > 0, 1.0, 0.0)
    hn = _rms(xn_ref[0], g) * jnp.where(i < n_tiles - 1, 1.0, 0.0)
    hcb = hc.astype(BF16)
    h_ext = jnp.concatenate([hp.astype(BF16), hcb, hn.astype(BF16)], axis=0)
    p = jnp.dot(h_ext, why_ref[...], preferred_element_type=F32)
    n = tm + 2 * HALO
    cw = cw_ref[...]
    u = (pltpu.roll(p, 1, 0)[HALO:HALO + tm] * cw[0:1]
         + p[HALO:HALO + tm] * cw[1:2]
         + pltpu.roll(p, n - 1, 0)[HALO:HALO + tm] * cw[2:3]
         + cb_ref[...])
    for i_out, ref in enumerate((x1_ref, x2_ref, v_ref)):
        for c in range(D_HY // HY_CB):
            lo = i_out * D_HY + c * HY_CB
            ref[0, c] = u[:, lo:lo + HY_CB].astype(ref.dtype)

    lat = jnp.dot(hcb, wlat_ref[...], preferred_element_type=F32)
    cq = lat[:, :Q_LORA]
    ckv = lat[:, Q_LORA:Q_LORA + KV_LORA]
    kr = lat[:, Q_LORA + KV_LORA:Q_LORA + KV_LORA + LANES]
    krs = lat[:, Q_LORA + KV_LORA + LANES:]
    q2_t = lax.dot_general(wq_ref[...], _rms(cq, qn_ref[...]).astype(BF16), (((1,), (1,)), ((), ())),
                           preferred_element_type=F32)
    ckvn = _rms(ckv, kvn_ref[...]).astype(BF16)
    k2 = jnp.dot(ckvn, wk_ref[...], preferred_element_type=F32)
    v_t = lax.dot_general(wvt_ref[...], ckvn, (((1,), (1,)), ((), ())),
                          preferred_element_type=F32) + vone_ref[...]
    hw = N_HEADS * LANES
    kr_r = kr * cos_ref[...] + krs * sin_ref[...]
    cos_tt = cost_ref[...]
    sin_tt = sint_ref[...]
    for h in range(N_HEADS):
        sl = slice(h * LANES, (h + 1) * LANES)
        qh = (q2_t[sl] * cos_tt + q2_t[hw + h * LANES:hw + (h + 1) * LANES] * sin_tt) * Q_SCALE
        q_ref[0, h, 0] = qh.astype(BF16)
        k_ref[0, h] = (k2[:, sl] + kr_r).astype(BF16)
        vt_ref[0, h, 0] = v_t[h * VT_ROWS:(h + 1) * VT_ROWS, :].astype(BF16)


def _in_call(x, g, why, wlat, cw, cb, qn, wq, kvn, wk, wvt, vone, rope_tabs, tm):
    B, L, _ = x.shape
    n_tiles = L // tm
    hb = tm // HALO
    n_hb = L // HALO
    tok = lambda b, i: (b, i, 0)
    in_specs = [
        pl.BlockSpec((1, HALO, D_MODEL), lambda b, i: (b, jnp.maximum(i * hb - 1, 0), 0)),
        pl.BlockSpec((1, tm, D_MODEL), tok),
        pl.BlockSpec((1, HALO, D_MODEL), lambda b, i: (b, jnp.minimum((i + 1) * hb, n_hb - 1), 0)),
        _const_spec(g.shape), _const_spec(why.shape), _const_spec(wlat.shape),
        _const_spec(cw.shape), _const_spec(cb.shape), _const_spec(qn.shape),
        _const_spec(wq.shape), _const_spec(kvn.shape), _const_spec(wk.shape),
        _const_spec(wvt.shape), _const_spec(vone.shape),
    ] + [pl.BlockSpec((tm, LANES), lambda b, i: (i, 0)) for _ in rope_tabs[:2]
         ] + [pl.BlockSpec((LANES, tm), lambda b, i: (0, i)) for _ in rope_tabs[2:]]
    hy = jax.ShapeDtypeStruct((B, D_HY // HY_CB, L, HY_CB), BF16)
    hd = jax.ShapeDtypeStruct((B, N_HEADS, L, LANES), BF16)
    hy_spec = pl.BlockSpec((1, D_HY // HY_CB, tm, HY_CB), lambda b, i: (b, 0, i, 0))
    hd_spec = pl.BlockSpec((1, N_HEADS, tm, LANES), lambda b, i: (b, 0, i, 0))
    qt = jax.ShapeDtypeStruct((B, N_HEADS, n_tiles, LANES, tm), BF16)
    qt_spec = pl.BlockSpec((1, N_HEADS, 1, LANES, tm), lambda b, i: (b, 0, i, 0, 0))
    vt = jax.ShapeDtypeStruct((B, N_HEADS, n_tiles, VT_ROWS, tm), BF16)
    vt_spec = pl.BlockSpec((1, N_HEADS, 1, VT_ROWS, tm), lambda b, i: (b, 0, i, 0, 0))
    return pl.pallas_call(
        functools.partial(_in_kernel, tm=tm, n_tiles=n_tiles),
        grid=(B, n_tiles),
        in_specs=in_specs,
        out_specs=[hy_spec, hy_spec, hy_spec, qt_spec, hd_spec, vt_spec],
        out_shape=[hy, hy, hy, qt, hd, vt],
        compiler_params=_cparams(("parallel", "arbitrary")),
        name="even_in",
    )(x, x, x, g, why, wlat, cw, cb, qn, wq, kvn, wk, wvt, vone, *rope_tabs)


def _attn_kernel(q_ref, k_ref, vt_ref, o_ref, sa_ref, sb_ref, *, tq, tk, n_chunks):
    qs = [jnp.concatenate([q_ref[0, hh, c] for c in range(q_ref.shape[2])], axis=1) for hh in range(2)]

    def scores(hh, j, s_ref):
        kc = k_ref[0, hh, pl.ds(pl.multiple_of(j * tk, tk), tk), :]
        s = jnp.dot(kc, qs[hh], preferred_element_type=F32)
        s_ref[hh] = s
        return jnp.max(s, axis=0, keepdims=True)

    def step(j, src_ref, dst_ref, state, cms, fetch):
        new_state, new_cms = [], []
        for hh in range(2):
            m, acc = state[2 * hh], state[2 * hh + 1]
            if fetch:
                new_cms.append(scores(hh, j + 1, dst_ref))
            m_new = jnp.maximum(m, cms[hh])
            p = jnp.exp2(src_ref[hh] - m_new).astype(BF16)
            acc = acc * jnp.exp2(m - m_new) + jnp.dot(vt_ref[0, hh, j], p, preferred_element_type=F32)
            new_state += [m_new, acc]
        return new_state, new_cms

    def run_chunks(j, state, cms, last):
        for c in range(unroll):
            src, dst = (sa_ref, sb_ref) if c % 2 == 0 else (sb_ref, sa_ref)
            state, cms = step(j + c, src, dst, state, cms, not (last and c == unroll - 1))
        return state, cms

    m0 = jnp.full((1, tq), -1e30, F32)
    acc0 = jnp.zeros((VT_ROWS, tq), F32)
    state = [m0, acc0, m0, acc0]
    cms = [scores(hh, 0, sa_ref) for hh in range(2)]
    if n_chunks == 1:
        res, _ = step(0, sa_ref, sb_ref, state, cms, False)
    else:
        unroll = ATT_UNROLL if (n_chunks % ATT_UNROLL == 0 and n_chunks >= ATT_MIN_GROUPS * ATT_UNROLL) else 2

        def body(i, carry):
            st, cm = run_chunks(unroll * i, list(carry[:4]), list(carry[4:]), last=False)
            return tuple(st) + tuple(cm)

        carry = lax.fori_loop(0, n_chunks // unroll - 1, body, tuple(state) + tuple(cms))
        res, _ = run_chunks(n_chunks - unroll, list(carry[:4]), list(carry[4:]), last=True)
    o_t = jnp.concatenate([res[1][:V_DIM] / res[1][V_DIM:V_DIM + 1],
                           res[3][:V_DIM] / res[3][V_DIM:V_DIM + 1]], axis=0)
    o_ref[0] = o_t.T


def _attn_call(qt, k, vt, tq):
    B, _, L, _ = k.shape
    _, _, n_chunks, _, tk = vt.shape
    assert n_chunks == 1 or n_chunks % 2 == 0, n_chunks
    assert qt.shape == (B, N_HEADS, n_chunks, LANES, tk) and tq % tk == 0
    return pl.pallas_call(
        functools.partial(_attn_kernel, tq=tq, tk=tk, n_chunks=n_chunks),
        grid=(B, N_HEADS // 2, L // tq),
        in_specs=[pl.BlockSpec((1, 2, tq // tk, LANES, tk), lambda b, hp, i: (b, hp, i, 0, 0)),
                  pl.BlockSpec((1, 2, L, LANES), lambda b, hp, i: (b, hp, 0, 0)),
                  pl.BlockSpec((1, 2, n_chunks, VT_ROWS, tk), lambda b, hp, i: (b, hp, 0, 0, 0))],
        out_specs=pl.BlockSpec((1, tq, LANES), lambda b, hp, i: (b, i, hp)),
        out_shape=jax.ShapeDtypeStruct((B, L, N_HEADS * V_DIM), F32),
        scratch_shapes=[pltpu.VMEM((2, tk, tq), F32), pltpu.VMEM((2, tk, tq), F32)],
        compiler_params=_cparams(("parallel", "parallel", "arbitrary")),
        name="attention",
    )(qt, k, vt)


def _mlp(x, g_ref, w1_ref, w2_ref, between=()):
    h = _rms(x, g_ref[...]).astype(BF16)
    acc = x
    for c in range(D_FF // MLP_CHUNK):
        cols = slice(c * MLP_CHUNK, (c + 1) * MLP_CHUNK)
        hid = jnp.dot(h, w1_ref[:, cols], preferred_element_type=F32)
        hid = jnp.square(jnp.maximum(hid, 0.0)).astype(BF16)
        acc = acc + jnp.dot(hid, w2_ref[cols, :], preferred_element_type=F32)
        if c < len(between):
            between[c]()
    return acc


def _out_kernel(x_ref, yh_ref, ya_ref, wo_ref, g_ref, w1_ref, w2_ref, o_ref):
    y = jnp.concatenate([yh_ref[0, c] for c in range(D_HY // HY_CB)] + [ya_ref[0].astype(BF16)], axis=1)
    x = x_ref[0] + jnp.dot(y, wo_ref[...], preferred_element_type=F32)
    o_ref[0] = _mlp(x, g_ref, w1_ref, w2_ref)


def _out_call(x, yh, ya, wo, g, w1, w2, tm):
    B, L, _ = x.shape
    tok = lambda b, i: (b, i, 0)
    return pl.pallas_call(
        _out_kernel,
        grid=(B, L // tm),
        in_specs=[pl.BlockSpec((1, tm, D_MODEL), tok),
                  pl.BlockSpec((1, D_HY // HY_CB, tm, HY_CB), lambda b, i: (b, 0, i, 0)),
                  pl.BlockSpec((1, tm, D_HY), tok), _const_spec(wo.shape), _const_spec(g.shape),
                  _const_spec(w1.shape), _const_spec(w2.shape)],
        out_specs=pl.BlockSpec((1, tm, D_MODEL), tok),
        out_shape=jax.ShapeDtypeStruct(x.shape, F32),
        compiler_params=_cparams(("parallel", "arbitrary")),
        name="even_out_mlp",
    )(x, yh, ya, wo, g, w1, w2)


def _odd_kernel(xp_ref, xc_ref, xn_ref, gm_ref, pw_ref, ps_ref, g_ref, w1_ref, w2_ref, fn_ref,
                o_ref, *, tm, n_tiles, seq, final):
    i = pl.program_id(1)
    gm = gm_ref[...]
    ps = ps_ref[...]
    x_ext = jnp.concatenate([xp_ref[0] * jnp.where(i > 0, 1.0, 0.0), xc_ref[0],
                             xn_ref[0] * jnp.where(i < n_tiles - 1, 1.0, 0.0)], axis=0)
    part = tm // 2 if tm % (4 * HALO) == 0 else tm
    n = part + 2 * HALO

    def normed(r0):
        return _rms(x_ext[r0:r0 + n], gm)

    def pooled(ep, r0, gi):
        w = POOL_WINDOWS[gi]
        sl = slice(gi * POOL_GRP, (gi + 1) * POOL_GRP)
        eg = ep[:, sl]
        s = pltpu.roll(eg, 1, 0) + eg
        half = 1
        while 2 * half < w:
            s = pltpu.roll(s, half, 0) + pltpu.roll(s, n - half, 0)
            half *= 2
        t = i * tm + r0 + lax.broadcasted_iota(jnp.int32, (part, 1), 0)
        cnt = (jnp.minimum(t + w // 2, seq) - jnp.maximum(t - w // 2, 0)).astype(F32)
        d = s[HALO:HALO + part] / cnt - eg[HALO:HALO + part]
        return _bdot(d, pw_ref[gi]) * ps[:, sl]

    def finish(r0, x_mixed, between=()):
        y = _mlp(x_mixed, g_ref, w1_ref, w2_ref, between)
        if final:
            y = _rms(y, fn_ref[...])
        o_ref[0, r0:r0 + part, :] = y

    groups = range(len(POOL_WINDOWS))
    ep = normed(0)
    x_mixed = x_ext[HALO:HALO + part] + jnp.concatenate([pooled(ep, 0, gi) for gi in groups], axis=1)
    for r0 in range(part, tm, part):
        nxt = {}

        def piece(gi, r0=r0, nxt=nxt):
            if gi == 0:
                nxt["e"] = normed(r0)
            nxt[gi] = pooled(nxt["e"], r0, gi)

        finish(r0 - part, x_mixed, [functools.partial(piece, gi) for gi in groups])
        x_mixed = x_ext[HALO + r0:HALO + r0 + part] + jnp.concatenate([nxt[gi] for gi in groups], axis=1)
    finish(tm - part, x_mixed)


def _odd_call(x, gm, pw, ps, g, w1, w2, fn, tm, final):
    B, L, _ = x.shape
    n_tiles = L // tm
    hb = tm // HALO
    n_hb = L // HALO
    tok = lambda b, i: (b, i, 0)
    return pl.pallas_call(
        functools.partial(_odd_kernel, tm=tm, n_tiles=n_tiles, seq=L, final=final),
        grid=(B, n_tiles),
        in_specs=[
            pl.BlockSpec((1, HALO, D_MODEL), lambda b, i: (b, jnp.maximum(i * hb - 1, 0), 0)),
            pl.BlockSpec((1, tm, D_MODEL), tok),
            pl.BlockSpec((1, HALO, D_MODEL), lambda b, i: (b, jnp.minimum((i + 1) * hb, n_hb - 1), 0)),
            _const_spec(gm.shape), _const_spec(pw.shape), _const_spec(ps.shape), _const_spec(g.shape),
            _const_spec(w1.shape), _const_spec(w2.shape), _const_spec(fn.shape)],
        out_specs=pl.BlockSpec((1, tm, D_MODEL), tok),
        out_shape=jax.ShapeDtypeStruct(x.shape, F32),
        compiler_params=_cparams(("parallel", "arbitrary")),
        name="odd_pool_mlp",
    )(x, x, x, gm, pw, ps, g, w1, w2, fn)


def _coarse_fwd_kernel(x_ref, g_ref, o_ref):
    x = x_ref[0, :, 0]
    a, _, cb = x.shape
    xb = x.reshape(a * SUBLANES, cb).astype(BF16)
    y = jnp.dot(g_ref[...], xb, preferred_element_type=F32)
    o_ref[0, :, 0] = y.reshape(y.shape[0] // SUBLANES, SUBLANES, cb)


def _coarse_fwd_call(x, g_a, cb):
    B, L, C = x.shape
    a = L // FFT_N2
    n1 = 2 * a
    nb = FFT_N2 // SUBLANES
    x5 = x.reshape(B, a, nb, SUBLANES, C)
    out = pl.pallas_call(
        _coarse_fwd_kernel,
        grid=(B, nb, C // cb),
        in_specs=[pl.BlockSpec((1, a, 1, SUBLANES, cb), lambda b, r, c: (b, 0, r, 0, c)),
                  _const_spec(g_a.shape)],
        out_specs=pl.BlockSpec((1, n1, 1, SUBLANES, cb), lambda b, r, c: (b, 0, r, 0, c)),
        out_shape=jax.ShapeDtypeStruct((B, n1, nb, SUBLANES, C), F32),
        compiler_params=_cparams(("parallel", "parallel", "arbitrary")),
        name="hyena_coarse_fwd",
    )(x5, g_a)
    return out.reshape(B, n1, FFT_N2, C)


def _cmul(x, kr, ki):
    xr, xi = x[:FFT_N2], x[FFT_N2:]
    return jnp.concatenate([xr * kr - xi * ki, xr * ki + xi * kr], axis=0)


def _conv_kernel(xf_ref, ga_ref, t_ref, ta_ref, kf_ref, kfn_ref, gb_ref, u_ref, gate_ref, bias_ref,
                 o_ref, s_ref, *, nb, nj, jb):
    t = pl.program_id(2)
    n1, _, cb = s_ref.shape

    def rows(step, g):
        return pl.ds(pl.multiple_of((step * HY_RG + g) * SUBLANES, SUBLANES), SUBLANES)

    @pl.when(t < nb)
    def _():
        x_all = xf_ref[0, 0, :, 0].astype(F32)
        for g in range(HY_RG):
            x = x_all[:, g * SUBLANES:(g + 1) * SUBLANES, :]
            xb = x.reshape(x.shape[0] * SUBLANES, cb).astype(BF16)
            y = jnp.dot(ga_ref[...], xb, preferred_element_type=F32)
            s_ref[:, rows(t, g), :] = y.reshape(n1, SUBLANES, cb)

    def pair(jj):
        return (t - nb) * jb + jj

    def load_pair(jj):
        return s_ref[pl.ds(2 * pair(jj), 2)].reshape(2 * FFT_N2, cb)

    def store_pair(jj, z):
        s_ref[pl.ds(2 * pair(jj), 2)] = z.reshape(2, FFT_N2, cb)

    def fine(jjs):
        ys = {}
        for n_done, jj in enumerate(list(jjs) + [None] * FINE_LAG):
            if jj is not None:
                x = _bdot(t_ref[pair(jj)], load_pair(jj))
            if n_done >= FINE_LAG:
                old = jjs[n_done - FINE_LAG]
                store_pair(old, _bdot(ta_ref[pair(old)], ys.pop(old)))
            if jj is not None:
                ys[jj] = _cmul(x, kf_ref[0, pair(jj)], kf_ref[1, pair(jj)])

    def fine_first_pair():
        s = load_pair(0)
        lo = lax.broadcasted_iota(jnp.int32, (2 * FFT_N2, 1), 0) < FFT_N2
        xa = _bdot(t_ref[0], jnp.where(lo, s, 0.0))
        xb = _bdot(t_ref[0], jnp.where(lo, 0.0, s))
        ya = _cmul(xa, kf_ref[0, 0], kf_ref[1, 0])
        yb = _cmul(xb, kfn_ref[0, 0], kfn_ref[1, 0])
        store_pair(0, jnp.where(lo, _bdot(ta_ref[0], ya), _bdot(ta_ref[0], yb)))

    @pl.when(t == nb)
    def _():
        fine_first_pair()
        fine(list(range(1, jb)))

    @pl.when((t > nb) & (t < nb + nj))
    def _():
        fine(list(range(jb)))

    @pl.when(t >= nb + nj)
    def _():
        u_all = u_ref[0, 0, :, 0].astype(F32)
        gate_all = gate_ref[0, 0, :, 0].astype(F32)
        outs = []
        for g in range(HY_RG):
            sub = slice(g * SUBLANES, (g + 1) * SUBLANES)
            z = s_ref[:, rows(t - nb - nj, g), :]
            y = jnp.dot(gb_ref[...], z.reshape(n1 * SUBLANES, cb).astype(BF16), preferred_element_type=F32)
            y = y.reshape(y.shape[0] // SUBLANES, SUBLANES, cb)
            outs.append((y + u_all[:, sub, :] * bias_ref[...]) * gate_all[:, sub, :])
        o_ref[0, 0, :, 0] = jnp.concatenate(outs, axis=1).astype(o_ref.dtype)


def _conv_call(u, gate, bias, kf, kfn, filt, tabs, jb):
    g_a, g_b, t_f, t_i = tabs
    B, nc, L, cb = u.shape
    a = L // FFT_N2
    n1 = 2 * a
    rg_rows = HY_RG * SUBLANES
    assert u.dtype == BF16 and rg_rows % (2 * SUBLANES) == 0
    nb = FFT_N2 // rg_rows
    nj = (n1 // 2) // jb
    koff = filt * nc
    u5 = u.reshape(B, nc, a, nb, rg_rows, cb)
    fwd_r = lambda c, b, t: (b, c, 0, jnp.minimum(t, nb - 1), 0, 0)
    inv_r = lambda c, b, t: (b, c, 0, jnp.clip(t - nb - nj, 0, nb - 1), 0, 0)
    tok = (1, 1, a, 1, rg_rows, cb)
    kspec = lambda n: pl.BlockSpec((2, n, FFT_N2, cb), lambda c, b, t: (0, 0, 0, koff + c),
                                   pipeline_mode=pl.Buffered(1))
    out = pl.pallas_call(
        functools.partial(_conv_kernel, nb=nb, nj=nj, jb=jb),
        grid=(nc, B, 2 * nb + nj),
        in_specs=[pl.BlockSpec(tok, fwd_r), _const_spec(g_a.shape), _const_spec(t_f.shape),
                  _const_spec(t_i.shape), kspec(n1 // 2), kspec(1),
                  _const_spec(g_b.shape), pl.BlockSpec(tok, inv_r), pl.BlockSpec(tok, inv_r),
                  pl.BlockSpec((1, cb), lambda c, b, t: (0, c))],
        out_specs=pl.BlockSpec(tok, inv_r),
        out_shape=jax.ShapeDtypeStruct(u5.shape, u.dtype),
        scratch_shapes=[pltpu.VMEM((n1, FFT_N2, cb), F32)],
        compiler_params=_cparams(("arbitrary", "arbitrary", "arbitrary")),
        name="hyena_conv",
    )(u5, g_a, t_f, t_i, kf, kfn, g_b, u5, gate.reshape(u5.shape), bias)
    return out.reshape(u.shape)


def _filter_kernel(z_ref, w1_ref, b1_ref, fr_ref, w2_ref, b2_ref, w3_ref, dec_ref, o_ref, *, tl):
    hi = lax.Precision.HIGHEST
    fr = fr_ref[...]
    h = jnp.sin(fr * (jnp.dot(z_ref[...], w1_ref[...], precision=hi, preferred_element_type=F32) + b1_ref[...]))
    h = jnp.sin(fr * (jnp.dot(h, w2_ref[...], precision=hi, preferred_element_type=F32) + b2_ref[...]))
    h_hi = h.astype(BF16)
    h_lo = (h - h_hi.astype(F32)).astype(BF16)
    w_hi = w3_ref[0]
    h = (jnp.dot(h_hi, w_hi, preferred_element_type=F32) + jnp.dot(h_lo, w_hi, preferred_element_type=F32)
         + jnp.dot(h_hi, w3_ref[1], preferred_element_type=F32))
    dec = dec_ref[...]
    h = h * jnp.concatenate([dec] * (2 * N_FILT), axis=1)
    row = pl.program_id(0) * tl + lax.broadcasted_iota(jnp.int32, h.shape, 0)
    lane = lax.broadcasted_iota(jnp.int32, h.shape, 1)
    o_ref[...] = jnp.where((row == 0) & (lane >= N_FILT * D_HY), 0.0, h)


def _filter_call(z, w1, b1, fr, w2, b2, w3, dec, tl):
    L = z.shape[0]
    return pl.pallas_call(
        functools.partial(_filter_kernel, tl=tl),
        grid=(L // tl,),
        in_specs=[pl.BlockSpec((tl, LANES), lambda i: (i, 0)),
                  _const_spec(w1.shape), _const_spec(b1.shape), _const_spec(fr.shape),
                  _const_spec(w2.shape), _const_spec(b2.shape), _const_spec(w3.shape),
                  pl.BlockSpec((tl, D_HY), lambda i: (i, 0))],
        out_specs=pl.BlockSpec((tl, 2 * N_FILT * D_HY), lambda i: (i, 0)),
        out_shape=jax.ShapeDtypeStruct((L, 2 * N_FILT * D_HY), F32),
        compiler_params=_cparams(("arbitrary",)),
        name="hyena_filter_mlp",
    )(z, w1, b1, fr, w2, b2, w3, dec)


def _spec_kernel(sf_ref, sb_ref, t_ref, kf_ref, kfn_ref, *, jb):
    step = pl.program_id(1)
    cb = sf_ref.shape[-1]

    def combine(xf, xb):
        return jnp.concatenate([xf[:FFT_N2] + xb[:FFT_N2], xf[FFT_N2:] - xb[FFT_N2:]], axis=0)

    def generic(jj):
        xf = _bdot(t_ref[jj], sf_ref[0, 2 * jj:2 * jj + 2].reshape(2 * FFT_N2, cb))
        xb = _bdot(t_ref[jj], sb_ref[0, 2 * jj:2 * jj + 2].reshape(2 * FFT_N2, cb))
        kf_ref[:, jj] = combine(xf, xb).reshape(2, FFT_N2, cb).astype(kf_ref.dtype)

    def special():
        row = lax.broadcasted_iota(jnp.int32, (2 * FFT_N2, 1), 0)
        lo = row < FFT_N2
        sf = sf_ref[0, 0:2].reshape(2 * FFT_N2, cb)
        sb = sb_ref[0, 0:2].reshape(2 * FFT_N2, cb)
        k0 = combine(_bdot(t_ref[0], jnp.where(lo, sf, 0.0)), _bdot(t_ref[0], jnp.where(lo, sb, 0.0)))
        kn = combine(_bdot(t_ref[0], jnp.where(lo, 0.0, sf)), _bdot(t_ref[0], jnp.where(lo, 0.0, sb)))
        kf_ref[:, 0] = k0.reshape(2, FFT_N2, cb).astype(kf_ref.dtype)
        kfn_ref[:, 0] = kn.reshape(2, FFT_N2, cb).astype(kfn_ref.dtype)

    pl.when(step == 0)(special)
    pl.when(step != 0)(lambda: generic(0))
    for jj in range(1, jb):
        generic(jj)


def _spec_call(s, t_f, cb, jb):
    _, n1, _, c2 = s.shape
    npair = n1 // 2
    C = c2 // 2
    nc = C // cb
    return pl.pallas_call(
        functools.partial(_spec_kernel, jb=jb),
        grid=(nc, npair // jb),
        in_specs=[pl.BlockSpec((1, 2 * jb, FFT_N2, cb), lambda c, j: (0, j, 0, c)),
                  pl.BlockSpec((1, 2 * jb, FFT_N2, cb), lambda c, j: (0, j, 0, nc + c)),
                  pl.BlockSpec((jb, 2 * FFT_N2, 2 * FFT_N2), lambda c, j: (j, 0, 0))],
        out_specs=[pl.BlockSpec((2, jb, FFT_N2, cb), lambda c, j: (0, j, 0, c)),
                   pl.BlockSpec((2, 1, FFT_N2, cb), lambda c, j: (0, 0, 0, c))],
        out_shape=[jax.ShapeDtypeStruct((2, npair, FFT_N2, C), BF16),
                   jax.ShapeDtypeStruct((2, 1, FFT_N2, C), BF16)],
        compiler_params=_cparams(("arbitrary", "arbitrary")),
        name="hyena_filter_spectrum",
    )(s, s, t_f)


def _dft_tables(L):
    a_n = L // FFT_N2
    n1 = 2 * a_n
    n = 2 * L
    npair = n1 // 2
    two_pi = 2.0 * math.pi

    a = jnp.arange(a_n, dtype=jnp.int32)[None, :]
    slot = jnp.arange(n1, dtype=jnp.int32)[:, None]
    j = slot // 2
    ang = two_pi * ((a * j) % n1).astype(F32) / n1
    sign = jnp.where(a % 2 == 0, 1.0, -1.0)
    is_im = (slot % 2) == 1
    f1 = jnp.where(is_im, -jnp.sin(ang), jnp.cos(ang))
    f1 = jnp.where(slot == 0, 1.0, jnp.where(slot == 1, sign, f1))
    fb = jnp.where(is_im, -2.0 * jnp.sin(ang), 2.0 * jnp.cos(ang))
    fb = jnp.where(slot == 0, 1.0, jnp.where(slot == 1, sign, fb)) / n
    eye = jnp.eye(SUBLANES, dtype=F32)
    g_a = jnp.kron(f1, eye).astype(BF16)
    g_b = jnp.kron(fb.T, eye).astype(BF16)

    b = jnp.arange(FFT_N2, dtype=jnp.int32)[None, None, :]
    d = jnp.arange(FFT_N2, dtype=jnp.int32)[None, :, None]
    c = jnp.arange(npair + 1, dtype=jnp.int32)[:, None, None]
    phi = two_pi * ((b * (d * n1 + c)) % n).astype(F32) / n
    co, si = jnp.cos(phi), jnp.sin(phi)
    t_f = jnp.concatenate([jnp.concatenate([co, si], axis=2),
                           jnp.concatenate([-si, co], axis=2)], axis=1)[:npair]
    t0 = jnp.concatenate([jnp.concatenate([co[0], co[npair]], axis=1),
                          jnp.concatenate([-si[0], -si[npair]], axis=1)], axis=0)
    t_f = t_f.at[0].set(t0)
    t_i = jnp.swapaxes(t_f, 1, 2)
    return g_a, g_b, t_f.astype(BF16), t_i.astype(BF16)


def _filter_tables(L):
    t = jnp.linspace(0.0, 1.0, L, dtype=F32)[:, None]
    wpos = (2.0 * math.pi / L) * jnp.arange(L, dtype=F32)[:, None]
    bands = jnp.linspace(1e-4, N_BANDS - 1, N_BANDS, dtype=F32)[None, :]
    z = jnp.concatenate([t, jnp.cos(bands * wpos), -jnp.sin(bands * wpos)], axis=-1)
    z = jnp.pad(z, ((0, 0), (0, LANES - EMB_DIM)))
    max_decay = math.log(DECAY_TARGET) / DECAY_FAST
    min_decay = math.log(DECAY_TARGET) / DECAY_SLOW
    deltas = jnp.linspace(min_decay, max_decay, D_HY, dtype=F32)
    decay = jnp.exp(-t * jnp.abs(deltas))
    return z, decay


def _rope_tables(L):
    inv_freq = 1.0 / (ROPE_THETA ** (jnp.arange(0, ROPE_DIM, 2, dtype=F32) / ROPE_DIM))
    ang = jnp.arange(L, dtype=F32)[:, None] * inv_freq[None, :]
    cos, sin = jnp.cos(ang), jnp.sin(ang)
    one = jnp.ones((L, NOPE_DIM), F32)
    zero_n = jnp.zeros((L, NOPE_DIM), F32)
    zero_t = jnp.zeros((L, LANES - NOPE_DIM - ROPE_DIM), F32)
    cos_t = jnp.concatenate([one, cos, cos, zero_t], axis=1)
    sin_t = jnp.concatenate([zero_n, -sin, sin, zero_t], axis=1)
    return cos_t, sin_t, cos_t.T, sin_t.T


def _swap_halves(w):
    half = w.shape[-1] // 2
    return jnp.concatenate([w[..., half:], w[..., :half]], axis=-1)


def _place(w, width, offset):
    return jnp.pad(w, ((0, 0), (offset, width - offset - w.shape[1])))


def _even_weights(w_in, w_uq, w_ukv):
    why = w_in[:, :HY_COLS].astype(BF16)
    w_kr = w_in[:, HY_COLS + Q_LORA + KV_LORA:]
    wlat = jnp.concatenate([
        w_in[:, HY_COLS:HY_COLS + Q_LORA + KV_LORA],
        _place(w_kr, LANES, NOPE_DIM),
        _place(_swap_halves(w_kr), LANES, NOPE_DIM)], axis=1).astype(BF16)
    qh = w_uq.reshape(Q_LORA, N_HEADS, NOPE_DIM + ROPE_DIM)
    q_main = jnp.pad(qh, ((0, 0), (0, 0), (0, LANES - NOPE_DIM - ROPE_DIM)))
    q_sw = jnp.pad(_swap_halves(qh[..., NOPE_DIM:]),
                   ((0, 0), (0, 0), (NOPE_DIM, LANES - NOPE_DIM - ROPE_DIM)))
    wq = jnp.concatenate([q_main.reshape(Q_LORA, -1), q_sw.reshape(Q_LORA, -1)], axis=1).T.astype(BF16)
    kvh = w_ukv.reshape(KV_LORA, N_HEADS, NOPE_DIM + V_DIM)
    k_main = jnp.pad(kvh[..., :NOPE_DIM], ((0, 0), (0, 0), (0, LANES - NOPE_DIM)))
    wk = k_main.reshape(KV_LORA, -1).astype(BF16)
    v_rows = jnp.pad(kvh[..., NOPE_DIM:], ((0, 0), (0, 0), (0, VT_ROWS - V_DIM)))
    wvt = v_rows.reshape(KV_LORA, -1).T.astype(BF16)
    vone = (jnp.arange(N_HEADS * VT_ROWS) % VT_ROWS == V_DIM).astype(F32)[:, None]
    return why, wlat, wq, wk, wvt, vone


def _tile(L, want):
    return min(want, L)


def _hyena(x1, x2, v, kf, kfn, hy_bias, tabs, jb):
    z = v
    for f, gate in enumerate((x1, x2)):
        z = _conv_call(z, gate, hy_bias[f][None, :], kf, kfn, f, tabs, jb)
    return z


def _filter_spectrum(L, w1, b1, fr, w2, b2, w3, tabs, ftabs):
    g_a, _, t_f, _ = tabs
    z, decay = ftabs
    pad_h = LANES - FILT_HID
    w1p = jnp.pad(w1, ((0, LANES - EMB_DIM), (0, pad_h)))
    w2p = jnp.pad(w2, ((0, pad_h), (0, pad_h)))
    w3p = jnp.pad(w3, ((0, pad_h), (0, 0)))
    w3_hi = w3p.astype(BF16)
    w3p = jnp.stack([w3_hi, (w3p - w3_hi.astype(F32)).astype(BF16)])
    pad1 = lambda a: jnp.pad(a, (0, pad_h))[None, :]
    hf = _filter_call(z, w1p, pad1(b1), pad1(fr), w2p, pad1(b2), w3p, decay, _tile(L, TOKEN_TILE))
    s = _coarse_fwd_call(hf[None], g_a, FILT_CB)
    return _spec_call(s, t_f, FILT_CB, min(FILT_JB, L // FFT_N2))


def _trunk(x, mix_norm, w_in, hy_conv_w, hy_conv_b, hf_w1, hf_b1, hf_freq, hf_w2, hf_b2, hf_w3, hy_bias,
           q_norm, w_uq, kv_norm, w_ukv, w_out, pool_w, pool_scale, mlp_norm, mlp_w1, mlp_w2, final_norm):
    B, L, _ = x.shape
    depth = mix_norm.shape[0]
    tm = _tile(L, TOKEN_TILE)
    tabs = _dft_tables(L)
    ftabs = _filter_tables(L)
    rope_tabs = _rope_tables(L)
    row = lambda a: a[None, :]
    for i in range(depth):
        w1 = mlp_w1[i].astype(BF16)
        w2 = mlp_w2[i].astype(BF16)
        g = row(mlp_norm[i])
        if i % 2 == 0:
            e = i // 2
            why, wlat, wq, wk, wvt, vone = _even_weights(w_in[e], w_uq[e], w_ukv[e])
            x1, x2, v, q, k, vt = _in_call(
                x, row(mix_norm[i]), why, wlat, hy_conv_w[e], row(hy_conv_b[e]),
                row(q_norm[e]), wq, row(kv_norm[e]), wk, wvt, vone, rope_tabs, tm)
            kf, kfn = _filter_spectrum(L, hf_w1[e], hf_b1[e], hf_freq[e], hf_w2[e], hf_b2[e], hf_w3[e],
                                       tabs, ftabs)
            y_hy = _hyena(x1, x2, v, kf, kfn, hy_bias[e], tabs, min(HY_JB, L // FFT_N2))
            y_att = _attn_call(q, k, vt, _tile(L, ATT_TQ))
            x = _out_call(x, y_hy, y_att, w_out[e].astype(BF16), g, w1, w2, tm)
        else:
            o = i // 2
            x = _odd_call(x, row(mix_norm[i]), pool_w[o].astype(BF16), row(pool_scale[o]), g, w1, w2,
                          row(final_norm), _tile(L, ODD_TILE), final=(i == depth - 1))
    if depth % 2 == 1:
        raise NotImplementedError("final norm is fused into the last odd layer")
    return x


def kernel(x_prompt, x_sample, mix_norm, w_in, hy_conv_w, hy_conv_b, hf_w1, hf_b1, hf_freq, hf_w2, hf_b2,
           hf_w3, hy_bias, q_norm, w_uq, kv_norm, w_ukv, w_out, pool_w, pool_scale, mlp_norm, mlp_w1,
           mlp_w2, final_norm):
    weights = (mix_norm, w_in, hy_conv_w, hy_conv_b, hf_w1, hf_b1, hf_freq, hf_w2, hf_b2, hf_w3, hy_bias,
               q_norm, w_uq, kv_norm, w_ukv, w_out, pool_w, pool_scale, mlp_norm, mlp_w1, mlp_w2, final_norm)
    return (_trunk(x_prompt, *weights), _trunk(x_sample, *weights))
```

```python
import functools
import math

import jax
import jax.numpy as jnp
from jax import lax
from jax.experimental import pallas as pl
from jax.experimental.pallas import tpu as pltpu

F32 = jnp.float32
BF16 = jnp.bfloat16

D_MODEL = 1024
D_HY = 512
N_FILT = 2
HY_COLS = (N_FILT + 1) * D_HY
EMB_DIM = 33
N_BANDS = (EMB_DIM - 1) // 2
FILT_HID = 64
DECAY_FAST = 0.3
DECAY_SLOW = 1.5
DECAY_TARGET = 1e-2
N_HEADS = 8
NOPE_DIM = 64
ROPE_DIM = 32
V_DIM = 64
Q_LORA = 384
KV_LORA = 256
ROPE_THETA = 10000.0
POOL_WINDOWS = (2, 4, 8, 16)
POOL_GRP = D_MODEL // len(POOL_WINDOWS)
D_FF = 4 * D_MODEL
EPS = 1e-6
ATT_SCALE = (NOPE_DIM + ROPE_DIM) ** -0.5
Q_SCALE = ATT_SCALE * math.log2(math.e)
VT_ROWS = 80
HY_CB = 256
HY_RG = 4
HY_JB = 16
FINE_LAG = 2
MLP_CHUNK = 512
ATT_UNROLL = 4
ATT_MIN_GROUPS = 4
TOKEN_TILE = 512
ODD_TILE = 1024
ATT_TQ = 1024
FILT_CB = 512
FILT_JB = 4

LANES = 128
SUBLANES = 8
HALO = 16
FFT_N2 = 128
VMEM_LIMIT = 56 * 1024 * 1024


def _cparams(sem):
    return pltpu.CompilerParams(dimension_semantics=sem, vmem_limit_bytes=VMEM_LIMIT)


def _const_spec(shape):
    nd = len(shape)
    return pl.BlockSpec(shape, lambda *_: (0,) * nd, pipeline_mode=pl.Buffered(1))


def _rms(xf, g):
    ms = jnp.mean(xf * xf, axis=-1, keepdims=True)
    return xf * lax.rsqrt(ms + EPS) * g


def _bdot(a, b):
    return jnp.dot(a.astype(BF16), b.astype(BF16), preferred_element_type=F32)


def _in_kernel(xp_ref, xc_ref, xn_ref, g_ref, why_ref, wlat_ref, cw_ref, cb_ref,
               qn_ref, wq_ref, kvn_ref, wk_ref, wvt_ref, vone_ref, cos_ref, sin_ref, cost_ref, sint_ref,
               x1_ref, x2_ref, v_ref, q_ref, k_ref, vt_ref, *, tm, n_tiles):
    i = pl.program_id(1)
    g = g_ref[...]
    hc = _rms(xc_ref[0], g)
    hp = _rms(xp_ref[0], g) * jnp.where(i > 0, 1.0, 0.0)
    hn = _rms(xn_ref[0], g) * jnp.where(i < n_tiles - 1, 1.0, 0.0)
    hcb = hc.astype(BF16)
    h_ext = jnp.concatenate([hp.astype(BF16), hcb, hn.astype(BF16)], axis=0)
    p = jnp.dot(h_ext, why_ref[...], preferred_element_type=F32)
    n = tm + 2 * HALO
    cw = cw_ref[...]
    u = (pltpu.roll(p, 1, 0)[HALO:HALO + tm] * cw[0:1]
         + p[HALO:HALO + tm] * cw[1:2]
         + pltpu.roll(p, n - 1, 0)[HALO:HALO + tm] * cw[2:3]
         + cb_ref[...])
    for i_out, ref in enumerate((x1_ref, x2_ref, v_ref)):
        for c in range(D_HY // HY_CB):
            lo = i_out * D_HY + c * HY_CB
            ref[0, c] = u[:, lo:lo + HY_CB].astype(ref.dtype)

    lat = jnp.dot(hcb, wlat_ref[...], preferred_element_type=F32)
    cq = lat[:, :Q_LORA]
    ckv = lat[:, Q_LORA:Q_LORA + KV_LORA]
    kr = lat[:, Q_LORA + KV_LORA:Q_LORA + KV_LORA + LANES]
    krs = lat[:, Q_LORA + KV_LORA + LANES:]
    q2_t = lax.dot_general(wq_ref[...], _rms(cq, qn_ref[...]).astype(BF16), (((1,), (1,)), ((), ())),
                           preferred_element_type=F32)
    ckvn = _rms(ckv, kvn_ref[...]).astype(BF16)
    k2 = jnp.dot(ckvn, wk_ref[...], preferred_element_type=F32)
    v_t = lax.dot_general(wvt_ref[...], ckvn, (((1,), (1,)), ((), ())),
                          preferred_element_type=F32) + vone_ref[...]
    hw = N_HEADS * LANES
    kr_r = kr * cos_ref[...] + krs * sin_ref[...]
    cos_tt = cost_ref[...]
    sin_tt = sint_ref[...]
    for h in range(N_HEADS):
        sl = slice(h * LANES, (h + 1) * LANES)
        qh = (q2_t[sl] * cos_tt + q2_t[hw + h * LANES:hw + (h + 1) * LANES] * sin_tt) * Q_SCALE
        q_ref[0, h, 0] = qh.astype(BF16)
        k_ref[0, h] = (k2[:, sl] + kr_r).astype(BF16)
        vt_ref[0, h, 0] = v_t[h * VT_ROWS:(h + 1) * VT_ROWS, :].astype(BF16)


def _in_call(x, g, why, wlat, cw, cb, qn, wq, kvn, wk, wvt, vone, rope_tabs, tm):
    B, L, _ = x.shape
    n_tiles = L // tm
    hb = tm // HALO
    n_hb = L // HALO
    tok = lambda b, i: (b, i, 0)
    in_specs = [
        pl.BlockSpec((1, HALO, D_MODEL), lambda b, i: (b, jnp.maximum(i * hb - 1, 0), 0)),
        pl.BlockSpec((1, tm, D_MODEL), tok),
        pl.BlockSpec((1, HALO, D_MODEL), lambda b, i: (b, jnp.minimum((i + 1) * hb, n_hb - 1), 0)),
        _const_spec(g.shape), _const_spec(why.shape), _const_spec(wlat.shape),
        _const_spec(cw.shape), _const_spec(cb.shape), _const_spec(qn.shape),
        _const_spec(wq.shape), _const_spec(kvn.shape), _const_spec(wk.shape),
        _const_spec(wvt.shape), _const_spec(vone.shape),
    ] + [pl.BlockSpec((tm, LANES), lambda b, i: (i, 0)) for _ in rope_tabs[:2]
         ] + [pl.BlockSpec((LANES, tm), lambda b, i: (0, i)) for _ in rope_tabs[2:]]
    hy = jax.ShapeDtypeStruct((B, D_HY // HY_CB, L, HY_CB), BF16)
    hd = jax.ShapeDtypeStruct((B, N_HEADS, L, LANES), BF16)
    hy_spec = pl.BlockSpec((1, D_HY // HY_CB, tm, HY_CB), lambda b, i: (b, 0, i, 0))
    hd_spec = pl.BlockSpec((1, N_HEADS, tm, LANES), lambda b, i: (b, 0, i, 0))
    qt = jax.ShapeDtypeStruct((B, N_HEADS, n_tiles, LANES, tm), BF16)
    qt_spec = pl.BlockSpec((1, N_HEADS, 1, LANES, tm), lambda b, i: (b, 0, i, 0, 0))
    vt = jax.ShapeDtypeStruct((B, N_HEADS, n_tiles, VT_ROWS, tm), BF16)
    vt_spec = pl.BlockSpec((1, N_HEADS, 1, VT_ROWS, tm), lambda b, i: (b, 0, i, 0, 0))
    return pl.pallas_call(
        functools.partial(_in_kernel, tm=tm, n_tiles=n_tiles),
        grid=(B, n_tiles),
        in_specs=in_specs,
        out_specs=[hy_spec, hy_spec, hy_spec, qt_spec, hd_spec, vt_spec],
        out_shape=[hy, hy, hy, qt, hd, vt],
        compiler_params=_cparams(("parallel", "arbitrary")),
        name="even_in",
    )(x, x, x, g, why, wlat, cw, cb, qn, wq, kvn, wk, wvt, vone, *rope_tabs)


def _attn_kernel(q_ref, k_ref, vt_ref, o_ref, sa_ref, sb_ref, *, tq, tk, n_chunks):
    qs = [jnp.concatenate([q_ref[0, hh, c] for c in range(q_ref.shape[2])], axis=1) for hh in range(2)]

    def scores(hh, j, s_ref):
        kc = k_ref[0, hh, pl.ds(pl.multiple_of(j * tk, tk), tk), :]
        s = jnp.dot(kc, qs[hh], preferred_element_type=F32)
        s_ref[hh] = s
        return jnp.max(s, axis=0, keepdims=True)

    def step(j, src_ref, dst_ref, state, cms, fetch):
        new_state, new_cms = [], []
        for hh in range(2):
            m, acc = state[2 * hh], state[2 * hh + 1]
            if fetch:
                new_cms.append(scores(hh, j + 1, dst_ref))
            m_new = jnp.maximum(m, cms[hh])
            p = jnp.exp2(src_ref[hh] - m_new).astype(BF16)
            acc = acc * jnp.exp2(m - m_new) + jnp.dot(vt_ref[0, hh, j], p, preferred_element_type=F32)
            new_state += [m_new, acc]
        return new_state, new_cms

    def run_chunks(j, state, cms, last):
        for c in range(unroll):
            src, dst = (sa_ref, sb_ref) if c % 2 == 0 else (sb_ref, sa_ref)
            state, cms = step(j + c, src, dst, state, cms, not (last and c == unroll - 1))
        return state, cms

    m0 = jnp.full((1, tq), -1e30, F32)
    acc0 = jnp.zeros((VT_ROWS, tq), F32)
    state = [m0, acc0, m0, acc0]
    cms = [scores(hh, 0, sa_ref) for hh in range(2)]
    if n_chunks == 1:
        res, _ = step(0, sa_ref, sb_ref, state, cms, False)
    else:
        unroll = ATT_UNROLL if (n_chunks % ATT_UNROLL == 0 and n_chunks >= ATT_MIN_GROUPS * ATT_UNROLL) else 2

        def body(i, carry):
            st, cm = run_chunks(unroll * i, list(carry[:4]), list(carry[4:]), last=False)
            return tuple(st) + tuple(cm)

        carry = lax.fori_loop(0, n_chunks // unroll - 1, body, tuple(state) + tuple(cms))
        res, _ = run_chunks(n_chunks - unroll, list(carry[:4]), list(carry[4:]), last=True)
    o_t = jnp.concatenate([res[1][:V_DIM] / res[1][V_DIM:V_DIM + 1],
                           res[3][:V_DIM] / res[3][V_DIM:V_DIM + 1]], axis=0)
    o_ref[0] = o_t.T


def _attn_call(qt, k, vt, tq):
    B, _, L, _ = k.shape
    _, _, n_chunks, _, tk = vt.shape
    assert n_chunks == 1 or n_chunks % 2 == 0, n_chunks
    assert qt.shape == (B, N_HEADS, n_chunks, LANES, tk) and tq % tk == 0
    return pl.pallas_call(
        functools.partial(_attn_kernel, tq=tq, tk=tk, n_chunks=n_chunks),
        grid=(B, N_HEADS // 2, L // tq),
        in_specs=[pl.BlockSpec((1, 2, tq // tk, LANES, tk), lambda b, hp, i: (b, hp, i, 0, 0)),
                  pl.BlockSpec((1, 2, L, LANES), lambda b, hp, i: (b, hp, 0, 0)),
                  pl.BlockSpec((1, 2, n_chunks, VT_ROWS, tk), lambda b, hp, i: (b, hp, 0, 0, 0))],
        out_specs=pl.BlockSpec((1, tq, LANES), lambda b, hp, i: (b, i, hp)),
        out_shape=jax.ShapeDtypeStruct((B, L, N_HEADS * V_DIM), F32),
        scratch_shapes=[pltpu.VMEM((2, tk, tq), F32), pltpu.VMEM((2, tk, tq), F32)],
        compiler_params=_cparams(("parallel", "parallel", "arbitrary")),
        name="attention",
    )(qt, k, vt)


def _mlp(x, g_ref, w1_ref, w2_ref, between=()):
    h = _rms(x, g_ref[...]).astype(BF16)
    acc = x
    for c in range(D_FF // MLP_CHUNK):
        cols = slice(c * MLP_CHUNK, (c + 1) * MLP_CHUNK)
        hid = jnp.dot(h, w1_ref[:, cols], preferred_element_type=F32)
        hid = jnp.square(jnp.maximum(hid, 0.0)).astype(BF16)
        acc = acc + jnp.dot(hid, w2_ref[cols, :], preferred_element_type=F32)
        if c < len(between):
            between[c]()
    return acc


def _out_kernel(x_ref, yh_ref, ya_ref, wo_ref, g_ref, w1_ref, w2_ref, o_ref):
    y = jnp.concatenate([yh_ref[0, c] for c in range(D_HY // HY_CB)] + [ya_ref[0].astype(BF16)], axis=1)
    x = x_ref[0] + jnp.dot(y, wo_ref[...], preferred_element_type=F32)
    o_ref[0] = _mlp(x, g_ref, w1_ref, w2_ref)


def _out_call(x, yh, ya, wo, g, w1, w2, tm):
    B, L, _ = x.shape
    tok = lambda b, i: (b, i, 0)
    return pl.pallas_call(
        _out_kernel,
        grid=(B, L // tm),
        in_specs=[pl.BlockSpec((1, tm, D_MODEL), tok),
                  pl.BlockSpec((1, D_HY // HY_CB, tm, HY_CB), lambda b, i: (b, 0, i, 0)),
                  pl.BlockSpec((1, tm, D_HY), tok), _const_spec(wo.shape), _const_spec(g.shape),
                  _const_spec(w1.shape), _const_spec(w2.shape)],
        out_specs=pl.BlockSpec((1, tm, D_MODEL), tok),
        out_shape=jax.ShapeDtypeStruct(x.shape, F32),
        compiler_params=_cparams(("parallel", "arbitrary")),
        name="even_out_mlp",
    )(x, yh, ya, wo, g, w1, w2)


def _odd_kernel(xp_ref, xc_ref, xn_ref, gm_ref, pw_ref, ps_ref, g_ref, w1_ref, w2_ref, fn_ref,
                o_ref, *, tm, n_tiles, seq, final):
    i = pl.program_id(1)
    gm = gm_ref[...]
    ps = ps_ref[...]
    x_ext = jnp.concatenate([xp_ref[0] * jnp.where(i > 0, 1.0, 0.0), xc_ref[0],
                             xn_ref[0] * jnp.where(i < n_tiles - 1, 1.0, 0.0)], axis=0)
    part = tm // 2 if tm % (4 * HALO) == 0 else tm
    n = part + 2 * HALO

    def normed(r0):
        return _rms(x_ext[r0:r0 + n], gm)

    def pooled(ep, r0, gi):
        w = POOL_WINDOWS[gi]
        sl = slice(gi * POOL_GRP, (gi + 1) * POOL_GRP)
        eg = ep[:, sl]
        s = pltpu.roll(eg, 1, 0) + eg
        half = 1
        while 2 * half < w:
            s = pltpu.roll(s, half, 0) + pltpu.roll(s, n - half, 0)
            half *= 2
        t = i * tm + r0 + lax.broadcasted_iota(jnp.int32, (part, 1), 0)
        cnt = (jnp.minimum(t + w // 2, seq) - jnp.maximum(t - w // 2, 0)).astype(F32)
        d = s[HALO:HALO + part] / cnt - eg[HALO:HALO + part]
        return _bdot(d, pw_ref[gi]) * ps[:, sl]

    def finish(r0, x_mixed, between=()):
        y = _mlp(x_mixed, g_ref, w1_ref, w2_ref, between)
        if final:
            y = _rms(y, fn_ref[...])
        o_ref[0, r0:r0 + part, :] = y

    groups = range(len(POOL_WINDOWS))
    ep = normed(0)
    x_mixed = x_ext[HALO:HALO + part] + jnp.concatenate([pooled(ep, 0, gi) for gi in groups], axis=1)
    for r0 in range(part, tm, part):
        nxt = {}

        def piece(gi, r0=r0, nxt=nxt):
            if gi == 0:
                nxt["e"] = normed(r0)
            nxt[gi] = pooled(nxt["e"], r0, gi)

        finish(r0 - part, x_mixed, [functools.partial(piece, gi) for gi in groups])
        x_mixed = x_ext[HALO + r0:HALO + r0 + part] + jnp.concatenate([nxt[gi] for gi in groups], axis=1)
    finish(tm - part, x_mixed)


def _odd_call(x, gm, pw, ps, g, w1, w2, fn, tm, final):
    B, L, _ = x.shape
    n_tiles = L // tm
    hb = tm // HALO
    n_hb = L // HALO
    tok = lambda b, i: (b, i, 0)
    return pl.pallas_call(
        functools.partial(_odd_kernel, tm=tm, n_tiles=n_tiles, seq=L, final=final),
        grid=(B, n_tiles),
        in_specs=[
            pl.BlockSpec((1, HALO, D_MODEL), lambda b, i: (b, jnp.maximum(i * hb - 1, 0), 0)),
            pl.BlockSpec((1, tm, D_MODEL), tok),
            pl.BlockSpec((1, HALO, D_MODEL), lambda b, i: (b, jnp.minimum((i + 1) * hb, n_hb - 1), 0)),
            _const_spec(gm.shape), _const_spec(pw.shape), _const_spec(ps.shape), _const_spec(g.shape),
            _const_spec(w1.shape), _const_spec(w2.shape), _const_spec(fn.shape)],
        out_specs=pl.BlockSpec((1, tm, D_MODEL), tok),
        out_shape=jax.ShapeDtypeStruct(x.shape, F32),
        compiler_params=_cparams(("parallel", "arbitrary")),
        name="odd_pool_mlp",
    )(x, x, x, gm, pw, ps, g, w1, w2, fn)


def _coarse_fwd_kernel(x_ref, g_ref, o_ref):
    x = x_ref[0, :, 0]
    a, _, cb = x.shape
    xb = x.reshape(a * SUBLANES, cb).astype(BF16)
    y = jnp.dot(g_ref[...], xb, preferred_element_type=F32)
    o_ref[0, :, 0] = y.reshape(y.shape[0] // SUBLANES, SUBLANES, cb)


def _coarse_fwd_call(x, g_a, cb):
    B, L, C = x.shape
    a = L // FFT_N2
    n1 = 2 * a
    nb = FFT_N2 // SUBLANES
    x5 = x.reshape(B, a, nb, SUBLANES, C)
    out = pl.pallas_call(
        _coarse_fwd_kernel,
        grid=(B, nb, C // cb),
        in_specs=[pl.BlockSpec((1, a, 1, SUBLANES, cb), lambda b, r, c: (b, 0, r, 0, c)),
                  _const_spec(g_a.shape)],
        out_specs=pl.BlockSpec((1, n1, 1, SUBLANES, cb), lambda b, r, c: (b, 0, r, 0, c)),
        out_shape=jax.ShapeDtypeStruct((B, n1, nb, SUBLANES, C), F32),
        compiler_params=_cparams(("parallel", "parallel", "arbitrary")),
        name="hyena_coarse_fwd",
    )(x5, g_a)
    return out.reshape(B, n1, FFT_N2, C)


def _cmul(x, kr, ki):
    xr, xi = x[:FFT_N2], x[FFT_N2:]
    return jnp.concatenate([xr * kr - xi * ki, xr * ki + xi * kr], axis=0)


def _conv_kernel(xf_ref, ga_ref, t_ref, ta_ref, kf_ref, kfn_ref, gb_ref, u_ref, gate_ref, bias_ref,
                 o_ref, s_ref, *, nb, nj, jb):
    t = pl.program_id(2)
    n1, _, cb = s_ref.shape

    def rows(step, g):
        return pl.ds(pl.multiple_of((step * HY_RG + g) * SUBLANES, SUBLANES), SUBLANES)

    @pl.when(t < nb)
    def _():
        x_all = xf_ref[0, 0, :, 0].astype(F32)
        for g in range(HY_RG):
            x = x_all[:, g * SUBLANES:(g + 1) * SUBLANES, :]
            xb = x.reshape(x.shape[0] * SUBLANES, cb).astype(BF16)
            y = jnp.dot(ga_ref[...], xb, preferred_element_type=F32)
            s_ref[:, rows(t, g), :] = y.reshape(n1, SUBLANES, cb)

    def pair(jj):
        return (t - nb) * jb + jj

    def load_pair(jj):
        return s_ref[pl.ds(2 * pair(jj), 2)].reshape(2 * FFT_N2, cb)

    def store_pair(jj, z):
        s_ref[pl.ds(2 * pair(jj), 2)] = z.reshape(2, FFT_N2, cb)

    def fine(jjs):
        ys = {}
        for n_done, jj in enumerate(list(jjs) + [None] * FINE_LAG):
            if jj is not None:
                x = _bdot(t_ref[pair(jj)], load_pair(jj))
            if n_done >= FINE_LAG:
                old = jjs[n_done - FINE_LAG]
                store_pair(old, _bdot(ta_ref[pair(old)], ys.pop(old)))
            if jj is not None:
                ys[jj] = _cmul(x, kf_ref[0, pair(jj)], kf_ref[1, pair(jj)])

    def fine_first_pair():
        s = load_pair(0)
        lo = lax.broadcasted_iota(jnp.int32, (2 * FFT_N2, 1), 0) < FFT_N2
        xa = _bdot(t_ref[0], jnp.where(lo, s, 0.0))
        xb = _bdot(t_ref[0], jnp.where(lo, 0.0, s))
        ya = _cmul(xa, kf_ref[0, 0], kf_ref[1, 0])
        yb = _cmul(xb, kfn_ref[0, 0], kfn_ref[1, 0])
        store_pair(0, jnp.where(lo, _bdot(ta_ref[0], ya), _bdot(ta_ref[0], yb)))

    @pl.when(t == nb)
    def _():
        fine_first_pair()
        fine(list(range(1, jb)))

    @pl.when((t > nb) & (t < nb + nj))
    def _():
        fine(list(range(jb)))

    @pl.when(t >= nb + nj)
    def _():
        u_all = u_ref[0, 0, :, 0].astype(F32)
        gate_all = gate_ref[0, 0, :, 0].astype(F32)
        outs = []
        for g in range(HY_RG):
            sub = slice(g * SUBLANES, (g + 1) * SUBLANES)
            z = s_ref[:, rows(t - nb - nj, g), :]
            y = jnp.dot(gb_ref[...], z.reshape(n1 * SUBLANES, cb).astype(BF16), preferred_element_type=F32)
            y = y.reshape(y.shape[0] // SUBLANES, SUBLANES, cb)
            outs.append((y + u_all[:, sub, :] * bias_ref[...]) * gate_all[:, sub, :])
        o_ref[0, 0, :, 0] = jnp.concatenate(outs, axis=1).astype(o_ref.dtype)


def _conv_call(u, gate, bias, kf, kfn, filt, tabs, jb):
    g_a, g_b, t_f, t_i = tabs
    B, nc, L, cb = u.shape
    a = L // FFT_N2
    n1 = 2 * a
    rg_rows = HY_RG * SUBLANES
    assert u.dtype == BF16 and rg_rows % (2 * SUBLANES) == 0
    nb = FFT_N2 // rg_rows
    nj = (n1 // 2) // jb
    koff = filt * nc
    u5 = u.reshape(B, nc, a, nb, rg_rows, cb)
    fwd_r = lambda c, b, t: (b, c, 0, jnp.minimum(t, nb - 1), 0, 0)
    inv_r = lambda c, b, t: (b, c, 0, jnp.clip(t - nb - nj, 0, nb - 1), 0, 0)
    tok = (1, 1, a, 1, rg_rows, cb)
    kspec = lambda n: pl.BlockSpec((2, n, FFT_N2, cb), lambda c, b, t: (0, 0, 0, koff + c),
                                   pipeline_mode=pl.Buffered(1))
    out = pl.pallas_call(
        functools.partial(_conv_kernel, nb=nb, nj=nj, jb=jb),
        grid=(nc, B, 2 * nb + nj),
        in_specs=[pl.BlockSpec(tok, fwd_r), _const_spec(g_a.shape), _const_spec(t_f.shape),
                  _const_spec(t_i.shape), kspec(n1 // 2), kspec(1),
                  _const_spec(g_b.shape), pl.BlockSpec(tok, inv_r), pl.BlockSpec(tok, inv_r),
                  pl.BlockSpec((1, cb), lambda c, b, t: (0, c))],
        out_specs=pl.BlockSpec(tok, inv_r),
        out_shape=jax.ShapeDtypeStruct(u5.shape, u.dtype),
        scratch_shapes=[pltpu.VMEM((n1, FFT_N2, cb), F32)],
        compiler_params=_cparams(("arbitrary", "arbitrary", "arbitrary")),
        name="hyena_conv",
    )(u5, g_a, t_f, t_i, kf, kfn, g_b, u5, gate.reshape(u5.shape), bias)
    return out.reshape(u.shape)


def _filter_kernel(z_ref, w1_ref, b1_ref, fr_ref, w2_ref, b2_ref, w3_ref, dec_ref, o_ref, *, tl):
    hi = lax.Precision.HIGHEST
    fr = fr_ref[...]
    h = jnp.sin(fr * (jnp.dot(z_ref[...], w1_ref[...], precision=hi, preferred_element_type=F32) + b1_ref[...]))
    h = jnp.sin(fr * (jnp.dot(h, w2_ref[...], precision=hi, preferred_element_type=F32) + b2_ref[...]))
    h_hi = h.astype(BF16)
    h_lo = (h - h_hi.astype(F32)).astype(BF16)
    w_hi = w3_ref[0]
    h = (jnp.dot(h_hi, w_hi, preferred_element_type=F32) + jnp.dot(h_lo, w_hi, preferred_element_type=F32)
         + jnp.dot(h_hi, w3_ref[1], preferred_element_type=F32))
    dec = dec_ref[...]
    h = h * jnp.concatenate([dec] * (2 * N_FILT), axis=1)
    row = pl.program_id(0) * tl + lax.broadcasted_iota(jnp.int32, h.shape, 0)
    lane = lax.broadcasted_iota(jnp.int32, h.shape, 1)
    o_ref[...] = jnp.where((row == 0) & (lane >= N_FILT * D_HY), 0.0, h)


def _filter_call(z, w1, b1, fr, w2, b2, w3, dec, tl):
    L = z.shape[0]
    return pl.pallas_call(
        functools.partial(_filter_kernel, tl=tl),
        grid=(L // tl,),
        in_specs=[pl.BlockSpec((tl, LANES), lambda i: (i, 0)),
                  _const_spec(w1.shape), _const_spec(b1.shape), _const_spec(fr.shape),
                  _const_spec(w2.shape), _const_spec(b2.shape), _const_spec(w3.shape),
                  pl.BlockSpec((tl, D_HY), lambda i: (i, 0))],
        out_specs=pl.BlockSpec((tl, 2 * N_FILT * D_HY), lambda i: (i, 0)),
        out_shape=jax.ShapeDtypeStruct((L, 2 * N_FILT * D_HY), F32),
        compiler_params=_cparams(("arbitrary",)),
        name="hyena_filter_mlp",
    )(z, w1, b1, fr, w2, b2, w3, dec)


def _spec_kernel(sf_ref, sb_ref, t_ref, kf_ref, kfn_ref, *, jb):
    step = pl.program_id(1)
    cb = sf_ref.shape[-1]

    def combine(xf, xb):
        return jnp.concatenate([xf[:FFT_N2] + xb[:FFT_N2], xf[FFT_N2:] - xb[FFT_N2:]], axis=0)

    def generic(jj):
        xf = _bdot(t_ref[jj], sf_ref[0, 2 * jj:2 * jj + 2].reshape(2 * FFT_N2, cb))
        xb = _bdot(t_ref[jj], sb_ref[0, 2 * jj:2 * jj + 2].reshape(2 * FFT_N2, cb))
        kf_ref[:, jj] = combine(xf, xb).reshape(2, FFT_N2, cb).astype(kf_ref.dtype)

    def special():
        row = lax.broadcasted_iota(jnp.int32, (2 * FFT_N2, 1), 0)
        lo = row < FFT_N2
        sf = sf_ref[0, 0:2].reshape(2 * FFT_N2, cb)
        sb = sb_ref[0, 0:2].reshape(2 * FFT_N2, cb)
        k0 = combine(_bdot(t_ref[0], jnp.where(lo, sf, 0.0)), _bdot(t_ref[0], jnp.where(lo, sb, 0.0)))
        kn = combine(_bdot(t_ref[0], jnp.where(lo, 0.0, sf)), _bdot(t_ref[0], jnp.where(lo, 0.0, sb)))
        kf_ref[:, 0] = k0.reshape(2, FFT_N2, cb).astype(kf_ref.dtype)
        kfn_ref[:, 0] = kn.reshape(2, FFT_N2, cb).astype(kfn_ref.dtype)

    pl.when(step == 0)(special)
    pl.when(step != 0)(lambda: generic(0))
    for jj in range(1, jb):
        generic(jj)


def _spec_call(s, t_f, cb, jb):
    _, n1, _, c2 = s.shape
    npair = n1 // 2
    C = c2 // 2
    nc = C // cb
    return pl.pallas_call(
        functools.partial(_spec_kernel, jb=jb),
        grid=(nc, npair // jb),
        in_specs=[pl.BlockSpec((1, 2 * jb, FFT_N2, cb), lambda c, j: (0, j, 0, c)),
                  pl.BlockSpec((1, 2 * jb, FFT_N2, cb), lambda c, j: (0, j, 0, nc + c)),
                  pl.BlockSpec((jb, 2 * FFT_N2, 2 * FFT_N2), lambda c, j: (j, 0, 0))],
        out_specs=[pl.BlockSpec((2, jb, FFT_N2, cb), lambda c, j: (0, j, 0, c)),
                   pl.BlockSpec((2, 1, FFT_N2, cb), lambda c, j: (0, 0, 0, c))],
        out_shape=[jax.ShapeDtypeStruct((2, npair, FFT_N2, C), BF16),
                   jax.ShapeDtypeStruct((2, 1, FFT_N2, C), BF16)],
        compiler_params=_cparams(("arbitrary", "arbitrary")),
        name="hyena_filter_spectrum",
    )(s, s, t_f)


def _dft_tables(L):
    a_n = L // FFT_N2
    n1 = 2 * a_n
    n = 2 * L
    npair = n1 // 2
    two_pi = 2.0 * math.pi

    a = jnp.arange(a_n, dtype=jnp.int32)[None, :]
    slot = jnp.arange(n1, dtype=jnp.int32)[:, None]
    j = slot // 2
    ang = two_pi * ((a * j) % n1).astype(F32) / n1
    sign = jnp.where(a % 2 == 0, 1.0, -1.0)
    is_im = (slot % 2) == 1
    f1 = jnp.where(is_im, -jnp.sin(ang), jnp.cos(ang))
    f1 = jnp.where(slot == 0, 1.0, jnp.where(slot == 1, sign, f1))
    fb = jnp.where(is_im, -2.0 * jnp.sin(ang), 2.0 * jnp.cos(ang))
    fb = jnp.where(slot == 0, 1.0, jnp.where(slot == 1, sign, fb)) / n
    eye = jnp.eye(SUBLANES, dtype=F32)
    g_a = jnp.kron(f1, eye).astype(BF16)
    g_b = jnp.kron(fb.T, eye).astype(BF16)

    b = jnp.arange(FFT_N2, dtype=jnp.int32)[None, None, :]
    d = jnp.arange(FFT_N2, dtype=jnp.int32)[None, :, None]
    c = jnp.arange(npair + 1, dtype=jnp.int32)[:, None, None]
    phi = two_pi * ((b * (d * n1 + c)) % n).astype(F32) / n
    co, si = jnp.cos(phi), jnp.sin(phi)
    t_f = jnp.concatenate([jnp.concatenate([co, si], axis=2),
                           jnp.concatenate([-si, co], axis=2)], axis=1)[:npair]
    t0 = jnp.concatenate([jnp.concatenate([co[0], co[npair]], axis=1),
                          jnp.concatenate([-si[0], -si[npair]], axis=1)], axis=0)
    t_f = t_f.at[0].set(t0)
    t_i = jnp.swapaxes(t_f, 1, 2)
    return g_a, g_b, t_f.astype(BF16), t_i.astype(BF16)


def _filter_tables(L):
    t = jnp.linspace(0.0, 1.0, L, dtype=F32)[:, None]
    wpos = (2.0 * math.pi / L) * jnp.arange(L, dtype=F32)[:, None]
    bands = jnp.linspace(1e-4, N_BANDS - 1, N_BANDS, dtype=F32)[None, :]
    z = jnp.concatenate([t, jnp.cos(bands * wpos), -jnp.sin(bands * wpos)], axis=-1)
    z = jnp.pad(z, ((0, 0), (0, LANES - EMB_DIM)))
    max_decay = math.log(DECAY_TARGET) / DECAY_FAST
    min_decay = math.log(DECAY_TARGET) / DECAY_SLOW
    deltas = jnp.linspace(min_decay, max_decay, D_HY, dtype=F32)
    decay = jnp.exp(-t * jnp.abs(deltas))
    return z, decay


def _rope_tables(L):
    inv_freq = 1.0 / (ROPE_THETA ** (jnp.arange(0, ROPE_DIM, 2, dtype=F32) / ROPE_DIM))
    ang = jnp.arange(L, dtype=F32)[:, None] * inv_freq[None, :]
    cos, sin = jnp.cos(ang), jnp.sin(ang)
    one = jnp.ones((L, NOPE_DIM), F32)
    zero_n = jnp.zeros((L, NOPE_DIM), F32)
    zero_t = jnp.zeros((L, LANES - NOPE_DIM - ROPE_DIM), F32)
    cos_t = jnp.concatenate([one, cos, cos, zero_t], axis=1)
    sin_t = jnp.concatenate([zero_n, -sin, sin, zero_t], axis=1)
    return cos_t, sin_t, cos_t.T, sin_t.T


def _swap_halves(w):
    half = w.shape[-1] // 2
    return jnp.concatenate([w[..., half:], w[..., :half]], axis=-1)


def _place(w, width, offset):
    return jnp.pad(w, ((0, 0), (offset, width - offset - w.shape[1])))


def _even_weights(w_in, w_uq, w_ukv):
    why = w_in[:, :HY_COLS].astype(BF16)
    w_kr = w_in[:, HY_COLS + Q_LORA + KV_LORA:]
    wlat = jnp.concatenate([
        w_in[:, HY_COLS:HY_COLS + Q_LORA + KV_LORA],
        _place(w_kr, LANES, NOPE_DIM),
        _place(_swap_halves(w_kr), LANES, NOPE_DIM)], axis=1).astype(BF16)
    qh = w_uq.reshape(Q_LORA, N_HEADS, NOPE_DIM + ROPE_DIM)
    q_main = jnp.pad(qh, ((0, 0), (0, 0), (0, LANES - NOPE_DIM - ROPE_DIM)))
    q_sw = jnp.pad(_swap_halves(qh[..., NOPE_DIM:]),
                   ((0, 0), (0, 0), (NOPE_DIM, LANES - NOPE_DIM - ROPE_DIM)))
    wq = jnp.concatenate([q_main.reshape(Q_LORA, -1), q_sw.reshape(Q_LORA, -1)], axis=1).T.astype(BF16)
    kvh = w_ukv.reshape(KV_LORA, N_HEADS, NOPE_DIM + V_DIM)
    k_main = jnp.pad(kvh[..., :NOPE_DIM], ((0, 0), (0, 0), (0, LANES - NOPE_DIM)))
    wk = k_main.reshape(KV_LORA, -1).astype(BF16)
    v_rows = jnp.pad(kvh[..., NOPE_DIM:], ((0, 0), (0, 0), (0, VT_ROWS - V_DIM)))
    wvt = v_rows.reshape(KV_LORA, -1).T.astype(BF16)
    vone = (jnp.arange(N_HEADS * VT_ROWS) % VT_ROWS == V_DIM).astype(F32)[:, None]
    return why, wlat, wq, wk, wvt, vone


def _tile(L, want):
    return min(want, L)


def _hyena(x1, x2, v, kf, kfn, hy_bias, tabs, jb):
    z = v
    for f, gate in enumerate((x1, x2)):
        z = _conv_call(z, gate, hy_bias[f][None, :], kf, kfn, f, tabs, jb)
    return z


def _filter_spectrum(L, w1, b1, fr, w2, b2, w3, tabs, ftabs):
    g_a, _, t_f, _ = tabs
    z, decay = ftabs
    pad_h = LANES - FILT_HID
    w1p = jnp.pad(w1, ((0, LANES - EMB_DIM), (0, pad_h)))
    w2p = jnp.pad(w2, ((0, pad_h), (0, pad_h)))
    w3p = jnp.pad(w3, ((0, pad_h), (0, 0)))
    w3_hi = w3p.astype(BF16)
    w3p = jnp.stack([w3_hi, (w3p - w3_hi.astype(F32)).astype(BF16)])
    pad1 = lambda a: jnp.pad(a, (0, pad_h))[None, :]
    hf = _filter_call(z, w1p, pad1(b1), pad1(fr), w2p, pad1(b2), w3p, decay, _tile(L, TOKEN_TILE))
    s = _coarse_fwd_call(hf[None], g_a, FILT_CB)
    return _spec_call(s, t_f, FILT_CB, min(FILT_JB, L // FFT_N2))


def _trunk(x, mix_norm, w_in, hy_conv_w, hy_conv_b, hf_w1, hf_b1, hf_freq, hf_w2, hf_b2, hf_w3, hy_bias,
           q_norm, w_uq, kv_norm, w_ukv, w_out, pool_w, pool_scale, mlp_norm, mlp_w1, mlp_w2, final_norm):
    B, L, _ = x.shape
    depth = mix_norm.shape[0]
    tm = _tile(L, TOKEN_TILE)
    tabs = _dft_tables(L)
    ftabs = _filter_tables(L)
    rope_tabs = _rope_tables(L)
    row = lambda a: a[None, :]
    for i in range(depth):
        w1 = mlp_w1[i].astype(BF16)
        w2 = mlp_w2[i].astype(BF16)
        g = row(mlp_norm[i])
        if i % 2 == 0:
            e = i // 2
            why, wlat, wq, wk, wvt, vone = _even_weights(w_in[e], w_uq[e], w_ukv[e])
            x1, x2, v, q, k, vt = _in_call(
                x, row(mix_norm[i]), why, wlat, hy_conv_w[e], row(hy_conv_b[e]),
                row(q_norm[e]), wq, row(kv_norm[e]), wk, wvt, vone, rope_tabs, tm)
            kf, kfn = _filter_spectrum(L, hf_w1[e], hf_b1[e], hf_freq[e], hf_w2[e], hf_b2[e], hf_w3[e],
                                       tabs, ftabs)
            y_hy = _hyena(x1, x2, v, kf, kfn, hy_bias[e], tabs, min(HY_JB, L // FFT_N2))
            y_att = _attn_call(q, k, vt, _tile(L, ATT_TQ))
            x = _out_call(x, y_hy, y_att, w_out[e].astype(BF16), g, w1, w2, tm)
        else:
            o = i // 2
            x = _odd_call(x, row(mix_norm[i]), pool_w[o].astype(BF16), row(pool_scale[o]), g, w1, w2,
                          row(final_norm), _tile(L, ODD_TILE), final=(i == depth - 1))
    if depth % 2 == 1:
        raise NotImplementedError("final norm is fused into the last odd layer")
    return x


def kernel(x_prompt, x_sample, mix_norm, w_in, hy_conv_w, hy_conv_b, hf_w1, hf_b1, hf_freq, hf_w2, hf_b2,
           hf_w3, hy_bias, q_norm, w_uq, kv_norm, w_ukv, w_out, pool_w, pool_scale, mlp_norm, mlp_w1,
           mlp_w2, final_norm):
    weights = (mix_norm, w_in, hy_conv_w, hy_conv_b, hf_w1, hf_b1, hf_freq, hf_w2, hf_b2, hf_w3, hy_bias,
               q_norm, w_uq, kv_norm, w_ukv, w_out, pool_w, pool_scale, mlp_norm, mlp_w1, mlp_w2, final_norm)
    return (_trunk(x_prompt, *weights), _trunk(x_sample, *weights))
```
